```python
import jax, jax.numpy as jnp
from jax import lax
import numpy as np

D_MODEL = 1024
BATCH = 8
SEQ = 4096
DEPTH = 2

HEAD_DIM = 64
ROPE_THETA = 10000.0
EPS = 1e-6
NEG = -1e30
D_FF = 4 * D_MODEL
N_MEM = 256
MEM_HEADS = 4
NSA_HEADS = 12
NSA_KV_HEADS = 3
NSA_GQA = NSA_HEADS // NSA_KV_HEADS
CMP_BLOCK = 32
CMP_STRIDE = 16
CMP_HIDDEN = 256
SEL_BLOCK = 64
SEL_TOPK = 16
WINDOW = 512
NSA_Q_BLOCK = 32
SEL_FORCE = 1e9
DIL_PATTERNS = ((128, 1), (512, 4), (2048, 16))
N_DIL_GROUPS = 3
DIL_SLOTS = 8
DIL_Q_BLOCK = 64
N_A_LAYERS = DEPTH // 2
N_B_LAYERS = DEPTH - N_A_LAYERS
MEM_W = MEM_HEADS * HEAD_DIM
A_Q = NSA_HEADS * HEAD_DIM
A_KV = NSA_KV_HEADS * HEAD_DIM
A_GATES = 3 * NSA_HEADS
A_IN = A_Q + 6 * A_KV + MEM_W + A_GATES
A_OUT = A_Q + MEM_W
A_SPLITS = (A_Q, A_Q + A_KV, A_Q + 2 * A_KV, A_Q + 3 * A_KV, A_Q + 4 * A_KV, A_Q + 5 * A_KV, A_Q + 6 * A_KV, A_Q + 6 * A_KV + MEM_W)
B_Q = N_DIL_GROUPS * DIL_SLOTS * HEAD_DIM
B_IN = B_Q + MEM_W
B_OUT = DIL_SLOTS * HEAD_DIM + MEM_W
SHARED_KV = 2 * DIL_SLOTS * HEAD_DIM

kernel_name = "yoco_nsa_dilated_hybrid"


def rmsnorm(x, g):
    x32 = x.astype(jnp.float32)
    y = x32 * lax.rsqrt(jnp.mean(x32 * x32, axis=-1, keepdims=True) + EPS)
    return (y * g.astype(jnp.float32)).astype(x.dtype)


def rope(x, pos):
    half = x.shape[-1] // 2
    inv_freq = ROPE_THETA ** (-jnp.arange(half, dtype=jnp.float32) / half)
    ang = jnp.asarray(pos, jnp.float32)[:, None] * inv_freq[None, :]
    cos = jnp.cos(ang).astype(x.dtype)
    sin = jnp.sin(ang).astype(x.dtype)
    x1, x2 = x[..., :half], x[..., half:]
    return jnp.concatenate([x1 * cos - x2 * sin, x1 * sin + x2 * cos], axis=-1)


def split_heads(x, n):
    b, s, _ = x.shape
    return x.reshape(b, s, n, HEAD_DIM).transpose(0, 2, 1, 3)


def merge_heads(x):
    b, h, s, d = x.shape
    return x.transpose(0, 2, 1, 3).reshape(b, s, h * d)


def chunk_seq(x, axis, size):
    shp = x.shape
    x = x.reshape(shp[:axis] + (shp[axis] // size, size) + shp[axis + 1:])
    return jnp.moveaxis(x, axis, 0)


def unchunk_seq(y, axis):
    y = jnp.moveaxis(y, 0, axis)
    shp = y.shape
    return y.reshape(shp[:axis] + (shp[axis] * shp[axis + 1],) + shp[axis + 2:])


def masked_softmax(s, mask):
    return jax.nn.softmax(jnp.where(mask, s.astype(jnp.float32), NEG), axis=-1)


def nsa_attention(q, k_cmp, v_cmp, k_sel, v_sel, k_win, v_win, gate_logits,
                  q_norm, k_norm, cmp_pos, cmp_w1, cmp_b1, cmp_w2, cmp_b2):
    B, S, _ = q.shape
    G, R, D, QB = NSA_KV_HEADS, NSA_GQA, HEAD_DIM, NSA_Q_BLOCK
    scale = D ** -0.5
    pos = jnp.arange(S)
    q = rope(rmsnorm(split_heads(q, NSA_HEADS), q_norm), pos).reshape(B, G, R, S, D)
    k_sel = rope(rmsnorm(split_heads(k_sel, G), k_norm[1]), pos)
    v_sel = split_heads(v_sel, G)
    k_win = rope(rmsnorm(split_heads(k_win, G), k_norm[2]), pos)
    v_win = split_heads(v_win, G)

    n_cmp = (S - CMP_BLOCK) // CMP_STRIDE + 1
    blk_tok = np.arange(n_cmp)[:, None] * CMP_STRIDE + np.arange(CMP_BLOCK)[None, :]
    cmp_end = jnp.asarray(blk_tok[:, -1], jnp.int32)

    def compress(t, i):
        tb = (t[:, :, blk_tok] + cmp_pos[i]).reshape(B, G, n_cmp, CMP_BLOCK * D)
        return jax.nn.gelu(tb @ cmp_w1[i] + cmp_b1[i]) @ cmp_w2[i] + cmp_b2[i]

    k_c = rope(rmsnorm(compress(split_heads(k_cmp, G), 0), k_norm[0]), cmp_end)
    v_c = compress(split_heads(v_cmp, G), 1)

    n_blk = S // SEL_BLOCK
    n_sel = min(SEL_TOPK, n_blk)
    c_start = np.arange(n_cmp)[:, None] * CMP_STRIDE
    b_start = np.arange(n_blk)[None, :] * SEL_BLOCK
    overlap = jnp.asarray((c_start < b_start + SEL_BLOCK) & (c_start + CMP_BLOCK > b_start), jnp.float32)
    k_sel_blk = k_sel.reshape(B, G, n_blk, SEL_BLOCK, D)
    v_sel_blk = v_sel.reshape(B, G, n_blk, SEL_BLOCK, D)
    k_win_pad = jnp.pad(k_win, ((0, 0), (0, 0), (WINDOW, 0), (0, 0)))
    v_win_pad = jnp.pad(v_win, ((0, 0), (0, 0), (WINDOW, 0), (0, 0)))
    b_ix = jnp.arange(B)[:, None, None, None]
    g_ix = jnp.arange(G)[None, :, None, None]
    blk_ids = jnp.arange(n_blk)

    gates = jax.nn.sigmoid(gate_logits).reshape(B, S, NSA_HEADS, 3)
    gates = gates.transpose(0, 2, 1, 3).reshape(B, G, R, S, 3)

    def block(args):
        c, q_b, g_b = args
        t = c * QB + jnp.arange(QB)
        m_c = cmp_end[None, :] <= t[:, None]
        p_c = masked_softmax(jnp.einsum('bgrqd,bgcd->bgrqc', q_b, k_c) * scale, m_c)
        o_c = jnp.einsum('bgrqc,bgcd->bgrqd', p_c.astype(v_c.dtype), v_c)
        o_c = jnp.where(m_c.any(-1)[:, None], o_c, 0)
        imp = jnp.einsum('bgqc,cj->bgqj', p_c.sum(axis=2), overlap)
        cur = (t // SEL_BLOCK)[:, None]
        forced = (blk_ids == 0) | (blk_ids == cur) | (blk_ids == cur - 1)
        imp = jnp.where(forced, SEL_FORCE, jnp.where(blk_ids <= cur, imp, NEG))
        top_v, top_i = lax.top_k(imp, n_sel)
        k_g = k_sel_blk[b_ix, g_ix, top_i].reshape(B, G, QB, n_sel * SEL_BLOCK, D)
        v_g = v_sel_blk[b_ix, g_ix, top_i].reshape(B, G, QB, n_sel * SEL_BLOCK, D)
        key_pos = top_i[..., None] * SEL_BLOCK + jnp.arange(SEL_BLOCK)
        m_s = (key_pos <= t[:, None, None]) & (top_v > NEG / 2)[..., None]
        m_s = m_s.reshape(B, G, 1, QB, n_sel * SEL_BLOCK)
        p_s = masked_softmax(jnp.einsum('bgrqd,bgqkd->bgrqk', q_b, k_g) * scale, m_s)
        o_s = jnp.einsum('bgrqk,bgqkd->bgrqd', p_s.astype(v_g.dtype), v_g)
        k_b = lax.dynamic_slice_in_dim(k_win_pad, c * QB, QB + WINDOW, axis=2)
        v_b = lax.dynamic_slice_in_dim(v_win_pad, c * QB, QB + WINDOW, axis=2)
        kpos = c * QB - WINDOW + jnp.arange(QB + WINDOW)
        dist = t[:, None] - kpos[None, :]
        m_w = (kpos[None, :] >= 0) & (dist >= 0) & (dist < WINDOW)
        p_w = masked_softmax(jnp.einsum('bgrqd,bgkd->bgrqk', q_b, k_b) * scale, m_w)
        o_w = jnp.einsum('bgrqk,bgkd->bgrqd', p_w.astype(v_b.dtype), v_b)
        return g_b[..., 0:1] * o_c + g_b[..., 1:2] * o_s + g_b[..., 2:3] * o_w

    n_chunks = S // QB
    out = lax.map(block, (jnp.arange(n_chunks), chunk_seq(q, 3, QB), chunk_seq(gates, 3, QB)))
    out = unchunk_seq(out, 3).reshape(B, NSA_HEADS, S, D)
    return merge_heads(out)


def dilated_attention(q, k, v, q_norm):
    B, S, _ = q.shape
    H, D, QB = DIL_SLOTS, HEAD_DIM, DIL_Q_BLOCK
    scale = D ** -0.5
    q = q.reshape(B, S, N_DIL_GROUPS, H, D).transpose(0, 2, 3, 1, 4)
    q = rope(rmsnorm(q, q_norm[:, None, None, :]), jnp.arange(S))

    def block(args):
        c, q_b = args
        t = c * QB + jnp.arange(QB)
        outs, lses = [], []
        for gi, (window, dil) in enumerate(DIL_PATTERNS):
            n_k = window // dil + 1
            idx = t[:, None] - dil * jnp.arange(n_k)[None, :]
            valid = idx >= 0
            idx = jnp.maximum(idx, 0)
            k_g = jnp.take(k, idx, axis=2)
            v_g = jnp.take(v, idx, axis=2)
            s = jnp.einsum('bhqd,bhqkd->bhqk', q_b[:, gi], k_g).astype(jnp.float32) * scale
            s = jnp.where(valid, s, NEG)
            lse = jax.nn.logsumexp(s, axis=-1, keepdims=True)
            p = jnp.exp(s - lse)
            outs.append(jnp.einsum('bhqk,bhqkd->bhqd', p.astype(v.dtype), v_g))
            lses.append(lse)
        wts = jax.nn.softmax(jnp.stack(lses), axis=0).astype(v.dtype)
        return jnp.sum(jnp.stack(outs) * wts, axis=0)

    out = lax.map(block, (jnp.arange(S // QB), chunk_seq(q, 3, QB)))
    return merge_heads(unchunk_seq(out, 2))


def memory_attention(q, mem_kv, q_norm, k_norm):
    scale = HEAD_DIM ** -0.5
    q = rmsnorm(split_heads(q, MEM_HEADS), q_norm)
    k, v = jnp.split(mem_kv, 2, axis=-1)
    k = rmsnorm(split_heads(k, MEM_HEADS), k_norm)
    v = split_heads(v, MEM_HEADS)
    p = jax.nn.softmax(jnp.einsum('bhqd,bhmd->bhqm', q, k).astype(jnp.float32) * scale, axis=-1)
    return merge_heads(jnp.einsum('bhqm,bhmd->bhqd', p.astype(v.dtype), v))


def squared_relu_mlp(h, w_up, w_down):
    return jnp.square(jax.nn.relu(h @ w_up)) @ w_down


def setup_inputs(seed: int = 0) -> dict:
    key = jax.random.key(seed)
    keys = jax.random.split(key, 25)
    cnt = [0]

    def nk():
        k = keys[cnt[0]]
        cnt[0] += 1
        return k

    def normal(shape, scale):
        return scale * jax.random.normal(nk(), shape, jnp.float32)

    def gain(shape):
        return 1.0 + 0.05 * jax.random.normal(nk(), shape, jnp.float32)

    NA, NB = N_A_LAYERS, N_B_LAYERS
    return {
        "x": normal((BATCH, SEQ, D_MODEL), 1.0),
        "mem": normal((BATCH, N_MEM, D_MODEL), 1.0),
        "attn_norm": gain((DEPTH, D_MODEL)),
        "mlp_norm": gain((DEPTH, D_MODEL)),
        "w_up": normal((DEPTH, D_MODEL, D_FF), D_MODEL ** -0.5),
        "w_down": normal((DEPTH, D_FF, D_MODEL), D_FF ** -0.5),
        "mem_norm": gain((DEPTH, D_MODEL)),
        "w_mem_kv": normal((DEPTH, D_MODEL, 2 * MEM_W), D_MODEL ** -0.5),
        "mem_q_norm": gain((DEPTH, HEAD_DIM)),
        "mem_k_norm": gain((DEPTH, HEAD_DIM)),
        "a_w_in": normal((NA, D_MODEL, A_IN), D_MODEL ** -0.5),
        "a_w_out": normal((NA, A_OUT, D_MODEL), A_OUT ** -0.5),
        "a_q_norm": gain((NA, HEAD_DIM)),
        "a_k_norm": gain((NA, 3, HEAD_DIM)),
        "a_cmp_pos": normal((NA, 2, CMP_BLOCK, HEAD_DIM), 0.1),
        "a_cmp_w1": normal((NA, 2, CMP_BLOCK * HEAD_DIM, CMP_HIDDEN), (CMP_BLOCK * HEAD_DIM) ** -0.5),
        "a_cmp_b1": normal((NA, 2, CMP_HIDDEN), 0.01),
        "a_cmp_w2": normal((NA, 2, CMP_HIDDEN, HEAD_DIM), CMP_HIDDEN ** -0.5),
        "a_cmp_b2": normal((NA, 2, HEAD_DIM), 0.01),
        "kv_norm": gain((D_MODEL,)),
        "w_kv_shared": normal((D_MODEL, SHARED_KV), D_MODEL ** -0.5),
        "kv_k_norm": gain((HEAD_DIM,)),
        "b_w_in": normal((NB, D_MODEL, B_IN), D_MODEL ** -0.5),
        "b_w_out": normal((NB, B_OUT, D_MODEL), B_OUT ** -0.5),
        "b_q_norm": gain((NB, N_DIL_GROUPS, HEAD_DIM)),
    }


def reference(x, mem, attn_norm, mlp_norm, w_up, w_down, mem_norm, w_mem_kv, mem_q_norm, mem_k_norm,
              a_w_in, a_w_out, a_q_norm, a_k_norm, a_cmp_pos, a_cmp_w1, a_cmp_b1, a_cmp_w2, a_cmp_b2,
              kv_norm, w_kv_shared, kv_k_norm, b_w_in, b_w_out, b_q_norm):
    pos = jnp.arange(x.shape[1])
    k_shared = None
    v_shared = None
    for layer in range(DEPTH):
        h = rmsnorm(x, attn_norm[layer])
        mem_kv = rmsnorm(mem, mem_norm[layer]) @ w_mem_kv[layer]
        if layer < N_A_LAYERS:
            i = layer
            q, kc, vc, ks, vs, kw, vw, q_mem, gl = jnp.split(h @ a_w_in[i], A_SPLITS, axis=-1)
            o_main = nsa_attention(q, kc, vc, ks, vs, kw, vw, gl, a_q_norm[i], a_k_norm[i],
                                   a_cmp_pos[i], a_cmp_w1[i], a_cmp_b1[i], a_cmp_w2[i], a_cmp_b2[i])
            o_mem = memory_attention(q_mem, mem_kv, mem_q_norm[layer], mem_k_norm[layer])
            x = x + jnp.concatenate([o_main, o_mem], axis=-1) @ a_w_out[i]
        else:
            i = layer - N_A_LAYERS
            if i == 0:
                k_s, v_s = jnp.split(rmsnorm(x, kv_norm) @ w_kv_shared, 2, axis=-1)
                k_shared = rope(rmsnorm(split_heads(k_s, DIL_SLOTS), kv_k_norm), pos)
                v_shared = split_heads(v_s, DIL_SLOTS)
            q, q_mem = jnp.split(h @ b_w_in[i], [B_Q], axis=-1)
            o_main = dilated_attention(q, k_shared, v_shared, b_q_norm[i])
            o_mem = memory_attention(q_mem, mem_kv, mem_q_norm[layer], mem_k_norm[layer])
            x = x + jnp.concatenate([o_main, o_mem], axis=-1) @ b_w_out[i]
        x = x + squared_relu_mlp(rmsnorm(x, mlp_norm[layer]), w_up[layer], w_down[layer])
    return x
```

```python
import functools

import numpy as np
import jax
import jax.numpy as jnp
from jax import lax
from jax.experimental import pallas as pl
from jax.experimental.pallas import tpu as pltpu

D_MODEL = 1024
HEAD_DIM = 64
HALF = HEAD_DIM // 2
ROPE_THETA = 10000.0
EPS = 1e-6
NEG = -1e30
D_FF = 4 * D_MODEL
MEM_HEADS = 4
NSA_HEADS = 12
NSA_KV_HEADS = 3
NSA_GQA = NSA_HEADS // NSA_KV_HEADS
CMP_BLOCK = 32
CMP_STRIDE = 16
CMP_HIDDEN = 256
SEL_BLOCK = 64
SEL_TOPK = 16
WINDOW = 512
SEL_FORCE = 1e9
DIL_PATTERNS = ((128, 1), (512, 4), (2048, 16))
N_DIL_GROUPS = 3
DIL_SLOTS = 8
MEM_W = MEM_HEADS * HEAD_DIM
A_Q = NSA_HEADS * HEAD_DIM
A_KV = NSA_KV_HEADS * HEAD_DIM
B_Q = N_DIL_GROUPS * DIL_SLOTS * HEAD_DIM
SCALE = HEAD_DIM ** -0.5

LANES = 128
SUBLANES = 8
VMEM_LIMIT = 56 * 1024 * 1024
ROW_TILE = 512
Q_TILE = 128
SEL_KEY_TILE = 512
MEM_Q_TILE = 512
DIL_Q_TILE = 256
DIL_BAND = 128
FF_CHUNK = 1024

BF16 = jnp.bfloat16
F32 = jnp.float32


def _cparams(n_grid):
    return pltpu.CompilerParams(dimension_semantics=("arbitrary",) * n_grid,
                                vmem_limit_bytes=VMEM_LIMIT)


def _const_spec(shape):
    nd = len(shape)
    return pl.BlockSpec(shape, lambda *_: (0,) * nd, pipeline_mode=pl.Buffered(1))


def _dot(a, b):
    return jnp.dot(a, b, preferred_element_type=F32)


def _dot_nt(a, b):
    return lax.dot_general(a, b, (((1,), (1,)), ((), ())), preferred_element_type=F32)


def _rms_rows(x, gain):
    return x * lax.rsqrt(jnp.mean(x * x, axis=-1, keepdims=True) + EPS) * gain


def _slot_norm(y, gt, tab):
    z = y * lax.rsqrt(jnp.mean(y * y, axis=-1, keepdims=True) + EPS) * gt
    if tab is not None:
        z = z * tab
        z = z + pltpu.roll(z, HEAD_DIM, 1)
    return z


def _lane_lo(rows):
    return lax.broadcasted_iota(jnp.int32, (rows, LANES), 1) < HEAD_DIM


def _rot_half_cols(w):
    return jnp.concatenate([-w[..., HALF:], w[..., :HALF]], axis=-1)


def _swap_half(g):
    return jnp.concatenate([g[..., HALF:], g[..., :HALF]], axis=-1)


def _slots(w, n, kind):
    k = w.shape[0]
    w = w.reshape(k, n, HEAD_DIM)
    other = _rot_half_cols(w) if kind == "rot" else w
    return jnp.concatenate([w, other], axis=-1).reshape(k, n * LANES)


def _gain_slot(g, kind):
    other = _swap_half(g) if kind == "rot" else g
    return jnp.concatenate([g, other], axis=-1)


def _rope_table(pos):
    inv_freq = ROPE_THETA ** (-jnp.arange(HALF, dtype=F32) / HALF)
    ang = jnp.asarray(pos, F32)[:, None] * inv_freq[None, :]
    cos = jnp.cos(ang)
    sin = jnp.sin(ang)
    return jnp.concatenate([cos, cos, sin, sin], axis=-1)


def _pad_cols(w, width):
    return jnp.pad(w, ((0, 0), (0, width - w.shape[1])))


A_COL_Q = 0
A_COL_KS = A_COL_Q + NSA_HEADS * LANES
A_COL_KW = A_COL_KS + NSA_KV_HEADS * LANES
A_COL_VS = A_COL_KW + NSA_KV_HEADS * LANES
A_COL_VW = A_COL_VS + NSA_KV_HEADS * LANES
A_COL_KCVC = A_COL_VW + NSA_KV_HEADS * LANES
A_COL_QM = A_COL_KCVC + 3 * LANES
A_COL_G = A_COL_QM + MEM_HEADS * LANES
A_COLS = A_COL_G + NSA_KV_HEADS * LANES


def _prep_a_w_in(w):
    o = 0
    q = w[:, o:o + A_Q]; o += A_Q
    kc = w[:, o:o + A_KV]; o += A_KV
    vc = w[:, o:o + A_KV]; o += A_KV
    ks = w[:, o:o + A_KV]; o += A_KV
    vs = w[:, o:o + A_KV]; o += A_KV
    kw = w[:, o:o + A_KV]; o += A_KV
    vw = w[:, o:o + A_KV]; o += A_KV
    qm = w[:, o:o + MEM_W]; o += MEM_W
    gl = w[:, o:]
    gl = gl.reshape(D_MODEL, NSA_KV_HEADS, NSA_GQA, 3).transpose(0, 1, 3, 2)
    gl = gl.reshape(D_MODEL, NSA_KV_HEADS, 3 * NSA_GQA)
    gl = jnp.pad(gl, ((0, 0), (0, 0), (0, LANES - 3 * NSA_GQA))).reshape(D_MODEL, NSA_KV_HEADS * LANES)
    cols = [_slots(q, NSA_HEADS, "rot"), _slots(ks, NSA_KV_HEADS, "rot"), _slots(kw, NSA_KV_HEADS, "rot"),
            _slots(vs, NSA_KV_HEADS, "dup"), _slots(vw, NSA_KV_HEADS, "dup"),
            _pad_cols(jnp.concatenate([kc, vc], axis=1), 3 * LANES),
            _slots(qm, MEM_HEADS, "dup"), gl]
    return jnp.concatenate(cols, axis=1).astype(BF16)


def _a_proj_kernel(x_ref, gain_ref, w_ref, tab_ref, gt_ref,
                   qa_ref, ks_ref, kw_ref, vs_ref, vw_ref, kcvc_ref, qm_ref, g_ref, *, seq, tm):
    x = x_ref[...]
    hn = _rms_rows(x, gain_ref[...]).astype(BF16)
    tab = tab_ref[...]
    lo = _lane_lo(tm)
    lane = lax.broadcasted_iota(jnp.int32, (tm, LANES), 1)
    row = lax.broadcasted_iota(jnp.int32, (tm, LANES), 0)
    tok = (pl.program_id(0) % (seq // tm)) * tm + row
    blk_ind = jnp.where(lane - HEAD_DIM == tok // SEL_BLOCK, 1.0, 0.0)

    yq = _dot(hn, w_ref[:, A_COL_Q:A_COL_KS])
    for h in range(NSA_HEADS):
        z = _slot_norm(yq[:, h * LANES:(h + 1) * LANES], gt_ref[0:1, :], tab)
        qa_ref[:, h * LANES:(h + 1) * LANES] = jnp.where(lo, z * SCALE, 0.0).astype(BF16)

    yk = _dot(hn, w_ref[:, A_COL_KS:A_COL_VS])
    for g in range(NSA_KV_HEADS):
        z = _slot_norm(yk[:, g * LANES:(g + 1) * LANES], gt_ref[1:2, :], tab)
        ks_ref[:, g * LANES:(g + 1) * LANES] = jnp.where(lo, z, blk_ind).astype(BF16)
        c = (NSA_KV_HEADS + g) * LANES
        z = _slot_norm(yk[:, c:c + LANES], gt_ref[2:3, :], tab)
        kw_ref[:, g * LANES:(g + 1) * LANES] = jnp.where(lo, z, 0.0).astype(BF16)

    yv = _dot(hn, w_ref[:, A_COL_VS:A_COL_KCVC])
    for j, ref in enumerate((vs_ref, vw_ref)):
        for g in range(NSA_KV_HEADS):
            c = (j * NSA_KV_HEADS + g) * LANES
            y = yv[:, c:c + LANES]
            ref[:, 2 * g * LANES:(2 * g + 1) * LANES] = jnp.where(lo, y, 0.0).astype(BF16)
            ref[:, (2 * g + 1) * LANES:(2 * g + 2) * LANES] = jnp.where(lo, 0.0, y).astype(BF16)

    yr = _dot(hn, w_ref[:, A_COL_KCVC:A_COLS])
    kcvc_ref[...] = yr[:, :3 * LANES]
    for h in range(MEM_HEADS):
        c = 3 * LANES + h * LANES
        z = _slot_norm(yr[:, c:c + LANES], gt_ref[3:4, :], None)
        qm_ref[:, h * LANES:(h + 1) * LANES] = jnp.where(lo, z * SCALE, 0.0).astype(BF16)
    c = (3 + MEM_HEADS) * LANES
    g_ref[...] = jax.nn.sigmoid(yr[:, c:c + NSA_KV_HEADS * LANES])


def _a_proj(x2d, gain, w, tab, gt, seq):
    n = x2d.shape[0]
    tm = ROW_TILE
    nseq = seq // tm
    row_spec = lambda width: pl.BlockSpec((tm, width), lambda i: (i, 0))
    widths = (NSA_HEADS * LANES, NSA_KV_HEADS * LANES, NSA_KV_HEADS * LANES,
              2 * NSA_KV_HEADS * LANES, 2 * NSA_KV_HEADS * LANES, 3 * LANES, MEM_HEADS * LANES,
              NSA_KV_HEADS * LANES)
    dtypes = (BF16, BF16, BF16, BF16, BF16, F32, BF16, F32)
    return pl.pallas_call(
        functools.partial(_a_proj_kernel, seq=seq, tm=tm),
        grid=(n // tm,),
        in_specs=[row_spec(D_MODEL), _const_spec((1, D_MODEL)), _const_spec((D_MODEL, A_COLS)),
                  pl.BlockSpec((tm, LANES), lambda i: (i % nseq, 0)), _const_spec((SUBLANES, LANES))],
        out_specs=[row_spec(wd) for wd in widths],
        out_shape=[jax.ShapeDtypeStruct((n, wd), dt) for wd, dt in zip(widths, dtypes)],
        compiler_params=_cparams(1), name="a_proj",
    )(x2d, gain, w, tab, gt)


def _compress_kernel(hk_ref, hv_ref, pos_ref, w1_ref, b1_ref, w2_ref, b2_ref, tab_ref, gt_ref,
                     kc_ref, vc_ref):
    nc = hk_ref.shape[0]
    lo = _lane_lo(nc)
    for i, h_ref in enumerate((hk_ref, hv_ref)):
        h = h_ref[...]
        top = _dot((h + pos_ref[i, 0:1, :]).astype(BF16), w1_ref[i, 0])
        bot = _dot((h + pos_ref[i, 1:2, :]).astype(BF16), w1_ref[i, 1])
        pre = top + pltpu.roll(bot, nc - 1, 0) + b1_ref[i:i + 1, :]
        hid = jax.nn.gelu(pre).astype(BF16)
        y = _dot(hid, w2_ref[i]) + b2_ref[i:i + 1, :]
        if i == 0:
            z = _slot_norm(y, gt_ref[0:1, :], tab_ref[...])
            kc_ref[...] = jnp.where(lo, z, 0.0).astype(BF16)
        else:
            vc_ref[:, :LANES] = jnp.where(lo, y, 0.0).astype(BF16)
            vc_ref[:, LANES:] = jnp.where(lo, 0.0, y).astype(BF16)


def _compress(hk, hv, pos, w1, b1, w2, b2, tab_c, gt):
    bg, nc, _ = hk.shape
    hspec = pl.BlockSpec((None, nc, CMP_STRIDE * HEAD_DIM), lambda i: (i, 0, 0))
    return pl.pallas_call(
        _compress_kernel,
        grid=(bg,),
        in_specs=[hspec, hspec, _const_spec(pos.shape), _const_spec(w1.shape), _const_spec(b1.shape),
                  _const_spec(w2.shape), _const_spec(b2.shape), _const_spec(tab_c.shape),
                  _const_spec(gt.shape)],
        out_specs=[pl.BlockSpec((None, nc, LANES), lambda i: (i, 0, 0)),
                   pl.BlockSpec((None, nc, 2 * LANES), lambda i: (i, 0, 0))],
        out_shape=[jax.ShapeDtypeStruct((bg, nc, LANES), BF16),
                   jax.ShapeDtypeStruct((bg, nc, 2 * LANES), BF16)],
        compiler_params=_cparams(1), name="compress",
    )(hk, hv, pos, w1, b1, w2, b2, tab_c, gt)


STACK_ORDER = (0, 2, 1, 3)


def _stack_q(q_ref, extra=None):
    parts = []
    for r in STACK_ORDER:
        q = q_ref[:, r * LANES:(r + 1) * LANES]
        parts.append(q if extra is None else q + extra)
    return jnp.concatenate(parts, axis=0)


def _stack_gate(g_ref, branch, tq):
    parts = []
    for r in STACK_ORDER:
        c = branch * NSA_GQA + r
        parts.append(jnp.broadcast_to(g_ref[:, c:c + 1], (tq, LANES)))
    return jnp.concatenate(parts, axis=0)


def _pair_scale(scale_rep, tq):
    lo = _lane_lo(2 * tq)
    return jnp.where(lo, scale_rep[:2 * tq], scale_rep[2 * tq:])


def _store_pairs(o_ref, out, tq):
    o_ref[:, :LANES] = out[:tq]
    o_ref[:, LANES:] = out[tq:]


def _cmp_kernel(q_ref, kc_ref, vc_ref, g_ref, ov_ref, oc_ref, bias_ref, *, tq, n_cmp, n_blk, n_sel):
    i = pl.program_id(2)
    kc = kc_ref[...]
    nc = kc.shape[0]
    t0 = i * tq

    q4 = _stack_q(q_ref)
    rows = 4 * tq
    t_col = t0 + (lax.broadcasted_iota(jnp.int32, (rows, 1), 0) & (tq - 1))
    c_row = lax.broadcasted_iota(jnp.int32, (1, nc), 1)
    cend_row = jnp.where(c_row < n_cmp, c_row * CMP_STRIDE + (CMP_BLOCK - 1), jnp.int32(2 ** 30))
    mask = cend_row <= t_col
    s = jnp.where(mask, _dot_nt(q4, kc), NEG)
    m = jnp.max(s, axis=-1, keepdims=True)
    p = jnp.where(mask, jnp.exp(s - m), 0.0)
    l = jnp.sum(p, axis=-1, keepdims=True)
    inv = jnp.where(l > 0.0, 1.0 / l, 0.0)
    pb = p.astype(BF16)
    acc = _dot(pb[:2 * tq], vc_ref[:, :LANES]) + _dot(pb[2 * tq:], vc_ref[:, LANES:])
    fac = _stack_gate(g_ref, 0, tq) * jnp.broadcast_to(inv, (rows, LANES))
    _store_pairs(oc_ref, acc * _pair_scale(fac, tq), tq)

    t_row = t0 + lax.broadcasted_iota(jnp.int32, (1, tq), 1)
    c_col = lax.broadcasted_iota(jnp.int32, (nc, 1), 0)
    cend_col = jnp.where(c_col < n_cmp, c_col * CMP_STRIDE + (CMP_BLOCK - 1), jnp.int32(2 ** 30))
    mask_t = cend_col <= t_row
    psum = jnp.zeros((nc, tq), F32)
    for r in range(NSA_GQA):
        st = jnp.where(mask_t, _dot_nt(kc, q_ref[:, r * LANES:(r + 1) * LANES]), NEG)
        mt = jnp.max(st, axis=0, keepdims=True)
        pt = jnp.where(mask_t, jnp.exp(st - mt), 0.0)
        lt = jnp.sum(pt, axis=0, keepdims=True)
        psum = psum + pt * jnp.where(lt > 0.0, 1.0 / lt, 0.0)
    ps_hi = psum.astype(BF16)
    ps_lo = (psum - ps_hi.astype(F32)).astype(BF16)
    ov = ov_ref[...]
    imp = _dot(ov, ps_hi) + _dot(ov, ps_lo)

    cur = t_row // SEL_BLOCK
    ngrp = n_blk // SUBLANES
    jsub = lax.broadcasted_iota(jnp.int32, (SUBLANES, tq), 0)
    vals = []
    for a in range(ngrp):
        j = jsub + a * SUBLANES
        forced = (j == 0) | (j == cur) | (j == cur - 1)
        v = jnp.where(j <= cur, imp[a * SUBLANES:(a + 1) * SUBLANES], NEG)
        vals.append(jnp.where(forced, SEL_FORCE, v))
    ranks = [jnp.zeros((SUBLANES, tq), F32) for _ in range(ngrp)]
    for jj in range(n_blk):
        a0, s0 = divmod(jj, SUBLANES)
        rowb = jnp.broadcast_to(vals[a0][s0:s0 + 1, :], (SUBLANES, tq))
        for a in range(ngrp):
            ge = jnp.where(rowb >= vals[a], 1.0, 0.0)
            gt = jnp.where(rowb > vals[a], 1.0, 0.0)
            if a > a0:
                beats = ge
            elif a < a0:
                beats = gt
            else:
                beats = jnp.where(jsub > s0, ge, gt)
            ranks[a] = ranks[a] + beats
    parts = [jnp.zeros((HEAD_DIM, tq), F32)]
    for a in range(ngrp):
        live = jnp.where(vals[a] > NEG / 2, 0.0, NEG)
        parts.append(jnp.where(ranks[a] < n_sel, live, NEG))
    if n_blk < HEAD_DIM:
        parts.append(jnp.zeros((HEAD_DIM - n_blk, tq), F32))
    bias_t = jnp.concatenate(parts, axis=0)
    bias_ref[...] = bias_t.T.astype(BF16)


def _cmp_select(qa, kc, vc, gates, ov, n_cmp, n_sel):
    b, s, _ = qa.shape
    tq = Q_TILE
    nc = kc.shape[2]
    n_blk = s // SEL_BLOCK
    gw = NSA_GQA * LANES
    kern = functools.partial(_cmp_kernel, tq=tq, n_cmp=n_cmp, n_blk=n_blk, n_sel=n_sel)
    return pl.pallas_call(
        kern,
        grid=(b, NSA_KV_HEADS, s // tq),
        in_specs=[pl.BlockSpec((None, tq, gw), lambda b_, g, i: (b_, i, g)),
                  pl.BlockSpec((None, None, nc, LANES), lambda b_, g, i: (b_, g, 0, 0)),
                  pl.BlockSpec((None, None, nc, 2 * LANES), lambda b_, g, i: (b_, g, 0, 0)),
                  pl.BlockSpec((None, tq, LANES), lambda b_, g, i: (b_, i, g)),
                  _const_spec(ov.shape)],
        out_specs=[pl.BlockSpec((None, tq, 2 * LANES), lambda b_, g, i: (b_, i, g)),
                   pl.BlockSpec((None, None, tq, LANES), lambda b_, g, i: (b_, g, i, 0))],
        out_shape=[jax.ShapeDtypeStruct((b, s, NSA_KV_HEADS * 2 * LANES), F32),
                   jax.ShapeDtypeStruct((b, NSA_KV_HEADS, s, LANES), BF16)],
        compiler_params=_cparams(3), name="cmp_select",
    )(qa, kc, vc, gates, ov)


def _sel_kernel(q_ref, bias_ref, k_ref, v_ref, g_ref, o_ref, m_ref, l_ref, acc_ref, *, tq, tk):
    i = pl.program_id(2)
    rows = 4 * tq
    q4 = _stack_q(q_ref, bias_ref[...])
    m_ref[...] = jnp.full((rows, LANES), NEG, F32)
    l_ref[...] = jnp.zeros((rows, LANES), F32)
    acc_ref[...] = jnp.zeros((2 * tq, LANES), F32)
    lo2 = _lane_lo(2 * tq)
    rep = tk // LANES

    def step(kt, masked):
        k0 = pl.multiple_of(kt * tk, tk)
        s = _dot_nt(q4, k_ref[pl.ds(k0, tk), :])
        if masked:
            t_col = i * tq + (lax.broadcasted_iota(jnp.int32, (rows, 1), 0) & (tq - 1))
            kpos = k0 + lax.broadcasted_iota(jnp.int32, (1, tk), 1)
            s = jnp.where(kpos <= t_col, s, NEG)
        m_prev = m_ref[...]
        m_next = jnp.maximum(m_prev, jnp.max(s, axis=-1, keepdims=True))
        p = jnp.exp(s - pltpu.repeat(m_next, rep, 1))
        alpha = jnp.exp(m_prev - m_next)
        l_ref[...] = alpha * l_ref[...] + jnp.sum(p, axis=-1, keepdims=True)
        m_ref[...] = m_next
        pb = p.astype(BF16)
        v = v_ref[pl.ds(k0, tk), :]
        upd = _dot(pb[:2 * tq], v[:, :LANES]) + _dot(pb[2 * tq:], v[:, LANES:])
        acc_ref[...] = acc_ref[...] * jnp.where(lo2, alpha[:2 * tq], alpha[2 * tq:]) + upd

    kt_diag = (i * tq) // tk
    lax.fori_loop(0, kt_diag, lambda kt, c: (step(kt, False), c)[1], 0)
    step(kt_diag, True)
    fac = _stack_gate(g_ref, 1, tq) / l_ref[...]
    _store_pairs(o_ref, acc_ref[...] * _pair_scale(fac, tq), tq)


def _sel_attention(qa, bias, ks, vs, gates):
    b, s, _ = qa.shape
    tq, tk = Q_TILE, min(SEL_KEY_TILE, s)
    gw = NSA_GQA * LANES
    return pl.pallas_call(
        functools.partial(_sel_kernel, tq=tq, tk=tk),
        grid=(b, NSA_KV_HEADS, s // tq),
        in_specs=[pl.BlockSpec((None, tq, gw), lambda b_, g, i: (b_, i, g)),
                  pl.BlockSpec((None, None, tq, LANES), lambda b_, g, i: (b_, g, i, 0)),
                  pl.BlockSpec((None, s, LANES), lambda b_, g, i: (b_, 0, g)),
                  pl.BlockSpec((None, s, 2 * LANES), lambda b_, g, i: (b_, 0, g)),
                  pl.BlockSpec((None, tq, LANES), lambda b_, g, i: (b_, i, g))],
        out_specs=pl.BlockSpec((None, tq, 2 * LANES), lambda b_, g, i: (b_, i, g)),
        out_shape=jax.ShapeDtypeStruct((b, s, NSA_KV_HEADS * 2 * LANES), F32),
        scratch_shapes=[pltpu.VMEM((4 * tq, LANES), F32), pltpu.VMEM((4 * tq, LANES), F32),
                        pltpu.VMEM((2 * tq, LANES), F32)],
        compiler_params=_cparams(3), name="sel_attention",
    )(qa, bias, ks, vs, gates)


def _win_kernel(q_ref, k_ref, v_ref, g_ref, o_ref, *, tq, nk):
    i = pl.program_id(2)
    rows = 4 * tq
    q4 = _stack_q(q_ref)
    k0 = pl.multiple_of(jnp.maximum(i * tq + tq - nk, 0), tq)
    s = _dot_nt(q4, k_ref[pl.ds(k0, nk), :])
    t_col = i * tq + (lax.broadcasted_iota(jnp.int32, (rows, 1), 0) & (tq - 1))
    dist = t_col - (k0 + lax.broadcasted_iota(jnp.int32, (1, nk), 1))
    s = jnp.where((dist >= 0) & (dist < WINDOW), s, NEG)
    m = jnp.max(s, axis=-1, keepdims=True)
    p = jnp.exp(s - m)
    l = jnp.sum(p, axis=-1, keepdims=True)
    pb = p.astype(BF16)
    v = v_ref[pl.ds(k0, nk), :]
    acc = _dot(pb[:2 * tq], v[:, :LANES]) + _dot(pb[2 * tq:], v[:, LANES:])
    fac = _stack_gate(g_ref, 2, tq) * jnp.broadcast_to(1.0 / l, (rows, LANES))
    _store_pairs(o_ref, acc * _pair_scale(fac, tq), tq)


def _win_attention(qa, kw, vw, gates):
    b, s, _ = qa.shape
    tq = Q_TILE
    nk = min(WINDOW + tq, s)
    gw = NSA_GQA * LANES
    return pl.pallas_call(
        functools.partial(_win_kernel, tq=tq, nk=nk),
        grid=(b, NSA_KV_HEADS, s // tq),
        in_specs=[pl.BlockSpec((None, tq, gw), lambda b_, g, i: (b_, i, g)),
                  pl.BlockSpec((None, s, LANES), lambda b_, g, i: (b_, 0, g)),
                  pl.BlockSpec((None, s, 2 * LANES), lambda b_, g, i: (b_, 0, g)),
                  pl.BlockSpec((None, tq, LANES), lambda b_, g, i: (b_, i, g))],
        out_specs=pl.BlockSpec((None, tq, 2 * LANES), lambda b_, g, i: (b_, i, g)),
        out_shape=jax.ShapeDtypeStruct((b, s, NSA_KV_HEADS * 2 * LANES), F32),
        compiler_params=_cparams(3), name="win_attention",
    )(qa, kw, vw, gates)


def _mem_kv_kernel(mem_ref, gain_ref, w_ref, gt_ref, k_ref, v_ref):
    nm = mem_ref.shape[0]
    lo = _lane_lo(nm)
    hn = _rms_rows(mem_ref[...], gain_ref[...]).astype(BF16)
    y = _dot(hn, w_ref[...])
    for h in range(MEM_HEADS):
        z = _slot_norm(y[:, h * LANES:(h + 1) * LANES], gt_ref[...], None)
        k_ref[:, h * LANES:(h + 1) * LANES] = jnp.where(lo, z, 0.0).astype(BF16)
        yv = y[:, (MEM_HEADS + h) * LANES:(MEM_HEADS + h + 1) * LANES]
        keep = lo if h % 2 == 0 else jnp.logical_not(lo)
        v_ref[:, h * LANES:(h + 1) * LANES] = jnp.where(keep, yv, 0.0).astype(BF16)


def _mem_kv(mem, gains, w, gt):
    b, nm, _ = mem.shape
    nl = gains.shape[0]
    width = MEM_HEADS * LANES
    out_spec = pl.BlockSpec((None, None, nm, width), lambda l, b_: (l, b_, 0, 0))
    return pl.pallas_call(
        _mem_kv_kernel,
        grid=(nl, b),
        in_specs=[pl.BlockSpec((None, nm, D_MODEL), lambda l, b_: (b_, 0, 0)),
                  pl.BlockSpec((None, 1, D_MODEL), lambda l, b_: (l, 0, 0)),
                  pl.BlockSpec((None, D_MODEL, 2 * width), lambda l, b_: (l, 0, 0)),
                  pl.BlockSpec((None, 1, LANES), lambda l, b_: (l, 0, 0))],
        out_specs=[out_spec, out_spec],
        out_shape=[jax.ShapeDtypeStruct((nl, b, nm, width), BF16)] * 2,
        compiler_params=_cparams(2), name="mem_kv",
    )(mem, gains, w, gt)


def _mem_attn_kernel(q_ref, k_ref, v_ref, o_ref, *, tq):
    lo = _lane_lo(tq)
    for pair in range(MEM_HEADS // 2):
        acc = None
        for h in (2 * pair, 2 * pair + 1):
            sl = slice(h * LANES, (h + 1) * LANES)
            s = _dot_nt(q_ref[:, sl], k_ref[:, sl])
            m = jnp.max(s, axis=-1, keepdims=True)
            p = jnp.exp(s - m)
            l = jnp.sum(p, axis=-1, keepdims=True)
            o = _dot(p.astype(BF16), v_ref[:, sl]) * jnp.broadcast_to(1.0 / l, (tq, LANES))
            acc = o if acc is None else acc + o
        o_ref[:, pair * LANES:(pair + 1) * LANES] = acc


def _mem_attention(qm, mk, mv, layer):
    b, s, width = qm.shape
    nm = mk.shape[2]
    tq = min(MEM_Q_TILE, s)
    kv_spec = pl.BlockSpec((None, None, nm, width), lambda b_, i: (layer, b_, 0, 0))
    return pl.pallas_call(
        functools.partial(_mem_attn_kernel, tq=tq),
        grid=(b, s // tq),
        in_specs=[pl.BlockSpec((None, tq, width), lambda b_, i: (b_, i, 0)), kv_spec, kv_spec],
        out_specs=pl.BlockSpec((None, tq, MEM_W), lambda b_, i: (b_, i, 0)),
        out_shape=jax.ShapeDtypeStruct((b, s, MEM_W), F32),
        compiler_params=_cparams(2), name="mem_attention",
    )(qm, mk, mv)


def _mlp_tail(x1, gain_ref, wup_ref, wdn_ref, out_ref):
    hn = _rms_rows(x1, gain_ref[...]).astype(BF16)
    out_ref[...] = x1
    for c in range(D_FF // FF_CHUNK):
        u = _dot(hn, wup_ref[:, c * FF_CHUNK:(c + 1) * FF_CHUNK])
        u = jnp.square(jnp.maximum(u, 0.0)).astype(BF16)
        out_ref[...] += _dot(u, wdn_ref[c * FF_CHUNK:(c + 1) * FF_CHUNK, :])


def _a_out_kernel(oc_ref, os_ref, ow_ref, om_ref, x_ref, wo_ref, gain_ref, wup_ref, wdn_ref, out_ref):
    o_main = (oc_ref[...] + os_ref[...] + ow_ref[...]).astype(BF16)
    x1 = x_ref[...] + _dot(o_main, wo_ref[:A_Q, :]) + _dot(om_ref[...].astype(BF16), wo_ref[A_Q:, :])
    _mlp_tail(x1, gain_ref, wup_ref, wdn_ref, out_ref)


def _b_out_kernel(o0_ref, o1_ref, o2_ref, l0_ref, l1_ref, l2_ref, om_ref, x_ref, wo_ref, gain_ref,
                  wup_ref, wdn_ref, out_ref):
    l0, l1, l2 = l0_ref[...], l1_ref[...], l2_ref[...]
    mx = jnp.maximum(jnp.maximum(l0, l1), l2)
    e0, e1, e2 = jnp.exp(l0 - mx), jnp.exp(l1 - mx), jnp.exp(l2 - mx)
    o_main = (e0 * o0_ref[...] + e1 * o1_ref[...] + e2 * o2_ref[...]) / (e0 + e1 + e2)
    kd = DIL_SLOTS * HEAD_DIM
    x1 = (x_ref[...] + _dot(o_main.astype(BF16), wo_ref[:kd, :])
          + _dot(om_ref[...].astype(BF16), wo_ref[kd:, :]))
    _mlp_tail(x1, gain_ref, wup_ref, wdn_ref, out_ref)


def _out_mlp(kern, acts, x2d, wo, gain, wup, wdn, name):
    n = x2d.shape[0]
    tm = ROW_TILE
    row_spec = lambda width: pl.BlockSpec((tm, width), lambda i: (i, 0))
    return pl.pallas_call(
        kern,
        grid=(n // tm,),
        in_specs=[row_spec(a.shape[1]) for a in acts] + [row_spec(D_MODEL), _const_spec(wo.shape),
                  _const_spec((1, D_MODEL)), _const_spec(wup.shape), _const_spec(wdn.shape)],
        out_specs=row_spec(D_MODEL),
        out_shape=jax.ShapeDtypeStruct((n, D_MODEL), F32),
        compiler_params=_cparams(1), name=name,
    )(*acts, x2d, wo, gain, wup, wdn)


B_QCOLS = N_DIL_GROUPS * DIL_SLOTS * LANES
B_COLS = B_QCOLS + MEM_HEADS * LANES
KV_COLS = 2 * DIL_SLOTS * LANES


def _b_proj_kernel(x_ref, gains_ref, wq_ref, wkv_ref, tab_ref, gt_ref,
                   q0_ref, q1_ref, q2_ref, qm_ref, k_ref, v_ref, *, tm):
    x = x_ref[...]
    xn = x * lax.rsqrt(jnp.mean(x * x, axis=-1, keepdims=True) + EPS)
    tab = tab_ref[...]
    lo = _lane_lo(tm)
    hq = (xn * gains_ref[0:1, :]).astype(BF16)
    width = DIL_SLOTS * LANES
    for gi, ref in enumerate((q0_ref, q1_ref, q2_ref)):
        y = _dot(hq, wq_ref[:, gi * width:(gi + 1) * width])
        for h in range(DIL_SLOTS):
            z = _slot_norm(y[:, h * LANES:(h + 1) * LANES], gt_ref[gi:gi + 1, :], tab)
            ref[:, h * LANES:(h + 1) * LANES] = jnp.where(lo, z * SCALE, 0.0).astype(BF16)
    y = _dot(hq, wq_ref[:, B_QCOLS:B_COLS])
    for h in range(MEM_HEADS):
        z = _slot_norm(y[:, h * LANES:(h + 1) * LANES], gt_ref[3:4, :], None)
        qm_ref[:, h * LANES:(h + 1) * LANES] = jnp.where(lo, z * SCALE, 0.0).astype(BF16)
    hk = (xn * gains_ref[1:2, :]).astype(BF16)
    y = _dot(hk, wkv_ref[...])
    for h in range(DIL_SLOTS):
        z = _slot_norm(y[:, h * LANES:(h + 1) * LANES], gt_ref[4:5, :], tab)
        k_ref[:, h * LANES:(h + 1) * LANES] = jnp.where(lo, z, 0.0).astype(BF16)
        yv = y[:, (DIL_SLOTS + h) * LANES:(DIL_SLOTS + h + 1) * LANES]
        keep = lo if h % 2 == 0 else jnp.logical_not(lo)
        v_ref[:, h * LANES:(h + 1) * LANES] = jnp.where(keep, yv, 0.0).astype(BF16)


def _b_proj(x2d, gains, wq, wkv, tab, gt, seq):
    n = x2d.shape[0]
    tm = ROW_TILE
    nseq = seq // tm
    row_spec = lambda width: pl.BlockSpec((tm, width), lambda i: (i, 0))
    width = DIL_SLOTS * LANES
    widths = (width, width, width, MEM_HEADS * LANES, width, width)
    return pl.pallas_call(
        functools.partial(_b_proj_kernel, tm=tm),
        grid=(n // tm,),
        in_specs=[row_spec(D_MODEL), _const_spec(gains.shape), _const_spec(wq.shape),
                  _const_spec(wkv.shape), pl.BlockSpec((tm, LANES), lambda i: (i % nseq, 0)),
                  _const_spec(gt.shape)],
        out_specs=[row_spec(wd) for wd in widths],
        out_shape=[jax.ShapeDtypeStruct((n, wd), BF16) for wd in widths],
        compiler_params=_cparams(1), name="b_proj",
    )(x2d, gains, wq, wkv, tab, gt)


def _dil_kernel(q_ref, k_ref, v_ref, o_ref, lse_ref, *, tq, nk, length):
    i = pl.program_id(3)
    lo = _lane_lo(tq)
    k0 = pl.multiple_of(jnp.clip(i * tq - DIL_BAND, 0, length - nk), DIL_BAND)
    dist = (i * tq + lax.broadcasted_iota(jnp.int32, (tq, 1), 0)
            - (k0 + lax.broadcasted_iota(jnp.int32, (1, nk), 1)))
    mask = (dist >= 0) & (dist <= DIL_BAND)
    acc = None
    lse = None
    for hd in range(2):
        sl = slice(hd * LANES, (hd + 1) * LANES)
        s = jnp.where(mask, _dot_nt(q_ref[:, sl], k_ref[pl.ds(k0, nk), sl]), NEG)
        m = jnp.max(s, axis=-1, keepdims=True)
        p = jnp.exp(s - m)
        l = jnp.sum(p, axis=-1, keepdims=True)
        o = _dot(p.astype(BF16), v_ref[pl.ds(k0, nk), sl]) * jnp.broadcast_to(1.0 / l, (tq, LANES))
        acc = o if acc is None else acc + o
        lse_h = jnp.broadcast_to(m + jnp.log(l), (tq, LANES))
        lse = lse_h if lse is None else jnp.where(lo, lse, lse_h)
    o_ref[...] = acc
    lse_ref[...] = lse


def _dil_attention(q, k, v, dil):
    b, s, width = q.shape
    length = s // dil
    tq = min(DIL_Q_TILE, length)
    nk = min(tq + DIL_BAND, length)
    npair = DIL_SLOTS // 2
    qv = q.reshape(b, length, dil * width)
    kv = k.reshape(b, length, dil * width)
    vv = v.reshape(b, length, dil * width)
    ow = DIL_SLOTS * HEAD_DIM
    kv_spec = pl.BlockSpec((None, length, 2 * LANES), lambda b_, r, hp, i: (b_, 0, r * npair + hp))
    o_spec = pl.BlockSpec((None, tq, LANES), lambda b_, r, hp, i: (b_, i, r * npair + hp))
    o, lse = pl.pallas_call(
        functools.partial(_dil_kernel, tq=tq, nk=nk, length=length),
        grid=(b, dil, npair, length // tq),
        in_specs=[pl.BlockSpec((None, tq, 2 * LANES), lambda b_, r, hp, i: (b_, i, r * npair + hp)),
                  kv_spec, kv_spec],
        out_specs=[o_spec, o_spec],
        out_shape=[jax.ShapeDtypeStruct((b, length, dil * ow), F32)] * 2,
        compiler_params=_cparams(4), name=f"dil_attention_{dil}",
    )(qv, kv, vv)
    return o.reshape(b * s, ow), lse.reshape(b * s, ow)


def kernel(x, mem, attn_norm, mlp_norm, w_up, w_down, mem_norm, w_mem_kv, mem_q_norm, mem_k_norm,
           a_w_in, a_w_out, a_q_norm, a_k_norm, a_cmp_pos, a_cmp_w1, a_cmp_b1, a_cmp_w2, a_cmp_b2,
           kv_norm, w_kv_shared, kv_k_norm, b_w_in, b_w_out, b_q_norm):
    b, s, _ = x.shape
    n = b * s
    assert s % ROW_TILE == 0 and s % SEL_KEY_TILE == 0 and s // SEL_BLOCK <= HEAD_DIM
    assert s >= WINDOW + Q_TILE and (s // DIL_PATTERNS[-1][1]) % DIL_BAND == 0
    x2d = x.reshape(n, D_MODEL)
    tab = _rope_table(jnp.arange(s))

    wk, wv = w_mem_kv[..., :MEM_W], w_mem_kv[..., MEM_W:]
    w_mem = jnp.concatenate([jax.vmap(lambda w: _slots(w, MEM_HEADS, "dup"))(wk),
                             jax.vmap(lambda w: _slots(w, MEM_HEADS, "dup"))(wv)], axis=-1).astype(BF16)
    mk, mv = _mem_kv(mem, mem_norm[:, None, :], w_mem, _gain_slot(mem_k_norm, "dup")[:, None, :])

    gt_a = jnp.zeros((SUBLANES, LANES), F32)
    gt_a = gt_a.at[0].set(_gain_slot(a_q_norm[0], "rot")).at[1].set(_gain_slot(a_k_norm[0, 1], "rot"))
    gt_a = gt_a.at[2].set(_gain_slot(a_k_norm[0, 2], "rot")).at[3].set(_gain_slot(mem_q_norm[0], "dup"))
    qa, ks, kw, vs, vw, kcvc, qm, gates = _a_proj(
        x2d, attn_norm[0:1], _prep_a_w_in(a_w_in[0]), tab, gt_a, s)

    n_cmp = (s - CMP_BLOCK) // CMP_STRIDE + 1
    nc = s // CMP_STRIDE

    def half_blocks(t):
        t = t.reshape(b, s, NSA_KV_HEADS, HEAD_DIM).transpose(0, 2, 1, 3)
        return t.reshape(b * NSA_KV_HEADS, nc, CMP_STRIDE * HEAD_DIM)

    w2 = jnp.stack([_slots(a_cmp_w2[0, 0], 1, "rot"), _slots(a_cmp_w2[0, 1], 1, "dup")]).astype(BF16)
    b2 = jnp.stack([_slots(a_cmp_b2[0, 0][None], 1, "rot")[0], _slots(a_cmp_b2[0, 1][None], 1, "dup")[0]])
    tab_c = _rope_table(jnp.arange(nc) * CMP_STRIDE + (CMP_BLOCK - 1))
    kc, vc = _compress(
        half_blocks(kcvc[:, :A_KV]), half_blocks(kcvc[:, A_KV:2 * A_KV]),
        a_cmp_pos[0].reshape(2, 2, CMP_STRIDE * HEAD_DIM),
        a_cmp_w1[0].reshape(2, 2, CMP_STRIDE * HEAD_DIM, CMP_HIDDEN).astype(BF16), a_cmp_b1[0],
        w2, b2, tab_c, _gain_slot(a_k_norm[0, 0], "rot")[None])
    kc = kc.reshape(b, NSA_KV_HEADS, nc, LANES)
    vc = vc.reshape(b, NSA_KV_HEADS, nc, 2 * LANES)

    n_blk = s // SEL_BLOCK
    c_start = np.arange(nc)[None, :] * CMP_STRIDE
    b_start = np.arange(n_blk)[:, None] * SEL_BLOCK
    overlap_t = ((c_start < b_start + SEL_BLOCK) & (c_start + CMP_BLOCK > b_start)
                 & (np.arange(nc)[None, :] < n_cmp))
    ov = jnp.asarray(overlap_t, BF16)

    qa3 = qa.reshape(b, s, -1)
    gates3 = gates.reshape(b, s, -1)
    o_c, bias = _cmp_select(qa3, kc, vc, gates3, ov, n_cmp, min(SEL_TOPK, n_blk))
    o_s = _sel_attention(qa3, bias, ks.reshape(b, s, -1), vs.reshape(b, s, -1), gates3)
    o_w = _win_attention(qa3, kw.reshape(b, s, -1), vw.reshape(b, s, -1), gates3)
    o_m = _mem_attention(qm.reshape(b, s, -1), mk, mv, 0)
    x2d = _out_mlp(_a_out_kernel,
                   [o_c.reshape(n, -1), o_s.reshape(n, -1), o_w.reshape(n, -1), o_m.reshape(n, -1)],
                   x2d, a_w_out[0].astype(BF16), mlp_norm[0:1], w_up[0].astype(BF16),
                   w_down[0].astype(BF16), "a_out_mlp")

    wq = jnp.concatenate([_slots(b_w_in[0][:, :B_Q], N_DIL_GROUPS * DIL_SLOTS, "rot"),
                          _slots(b_w_in[0][:, B_Q:], MEM_HEADS, "dup")], axis=1).astype(BF16)
    kd = DIL_SLOTS * HEAD_DIM
    wkv = jnp.concatenate([_slots(w_kv_shared[:, :kd], DIL_SLOTS, "rot"),
                           _slots(w_kv_shared[:, kd:], DIL_SLOTS, "dup")], axis=1).astype(BF16)
    gt_b = jnp.zeros((SUBLANES, LANES), F32)
    for gi in range(N_DIL_GROUPS):
        gt_b = gt_b.at[gi].set(_gain_slot(b_q_norm[0, gi], "rot"))
    gt_b = gt_b.at[3].set(_gain_slot(mem_q_norm[1], "dup")).at[4].set(_gain_slot(kv_k_norm, "rot"))
    q0, q1, q2, qm1, kb, vb = _b_proj(x2d, jnp.stack([attn_norm[1], kv_norm]), wq, wkv, tab, gt_b, s)
    acts = []
    lses = []
    for qg, (_, dil) in zip((q0, q1, q2), DIL_PATTERNS):
        o, lse = _dil_attention(qg.reshape(b, s, -1), kb.reshape(b, s, -1), vb.reshape(b, s, -1), dil)
        acts.append(o)
        lses.append(lse)
    o_m = _mem_attention(qm1.reshape(b, s, -1), mk, mv, 1)
    x2d = _out_mlp(_b_out_kernel, acts + lses + [o_m.reshape(n, -1)], x2d, b_w_out[0].astype(BF16),
                   mlp_norm[1:2], w_up[1].astype(BF16), w_down[1].astype(BF16), "b_out_mlp")
    return x2d.reshape(b, s, D_MODEL)
```

```python
import functools

import numpy as np
import jax
import jax.numpy as jnp
from jax import lax
from jax.experimental import pallas as pl
from jax.experimental.pallas import tpu as pltpu

D_MODEL = 1024
HEAD_DIM = 64
HALF = HEAD_DIM // 2
ROPE_THETA = 10000.0
EPS = 1e-6
NEG = -1e30
D_FF = 4 * D_MODEL
MEM_HEADS = 4
NSA_HEADS = 12
NSA_KV_HEADS = 3
NSA_GQA = NSA_HEADS // NSA_KV_HEADS
CMP_BLOCK = 32
CMP_STRIDE = 16
CMP_HIDDEN = 256
SEL_BLOCK = 64
SEL_TOPK = 16
WINDOW = 512
SEL_FORCE = 1e9
DIL_PATTERNS = ((128, 1), (512, 4), (2048, 16))
N_DIL_GROUPS = 3
DIL_SLOTS = 8
MEM_W = MEM_HEADS * HEAD_DIM
A_Q = NSA_HEADS * HEAD_DIM
A_KV = NSA_KV_HEADS * HEAD_DIM
B_Q = N_DIL_GROUPS * DIL_SLOTS * HEAD_DIM
LOG2E = 1.4426950408889634
Q_SCALE = HEAD_DIM ** -0.5 * LOG2E

LANES = 128
SUBLANES = 8
VMEM_LIMIT = 56 * 1024 * 1024
ROW_TILE = 512
Q_TILE = 256
CMP_Q_TILE = 256
WIN_Q_TILE = 256
DIL_UNROLL = 4
SEL_KEY_TILE = 512
MEM_Q_TILE = 512
DIL_Q_TILE = 128
DIL_BAND = 128
FF_CHUNK = 1024

BF16 = jnp.bfloat16
F32 = jnp.float32


def _cparams(n_grid):
    return pltpu.CompilerParams(dimension_semantics=("arbitrary",) * n_grid,
                                vmem_limit_bytes=VMEM_LIMIT)


def _const_spec(shape):
    nd = len(shape)
    return pl.BlockSpec(shape, lambda *_: (0,) * nd, pipeline_mode=pl.Buffered(1))


def _dot(a, b):
    return jnp.dot(a, b, preferred_element_type=F32)


def _dot_nt(a, b):
    return lax.dot_general(a, b, (((1,), (1,)), ((), ())), preferred_element_type=F32)


def _rms_rows(x, gain):
    return x * lax.rsqrt(jnp.mean(x * x, axis=-1, keepdims=True) + EPS) * gain


def _slot_norm(y, gt, tab):
    z = y * lax.rsqrt(jnp.mean(y * y, axis=-1, keepdims=True) + EPS) * gt
    if tab is not None:
        z = z * tab
        z = z + pltpu.roll(z, HEAD_DIM, 1)
    return z


def _lane_lo(rows):
    return lax.broadcasted_iota(jnp.int32, (rows, LANES), 1) < HEAD_DIM


def _swap_halves(x):
    return pltpu.roll(x, HEAD_DIM, 1)


def _finish_slot(acc):
    return acc / _swap_halves(acc)


def _pack_pair(x_even, x_odd, lo):
    return jnp.where(lo, x_even, _swap_halves(x_odd))


def _rot_half_cols(w):
    return jnp.concatenate([-w[..., HALF:], w[..., :HALF]], axis=-1)


def _swap_half(g):
    return jnp.concatenate([g[..., HALF:], g[..., :HALF]], axis=-1)


def _slots(w, n, kind):
    k = w.shape[0]
    w = w.reshape(k, n, HEAD_DIM)
    other = _rot_half_cols(w) if kind == "rot" else w
    return jnp.concatenate([w, other], axis=-1).reshape(k, n * LANES)


def _gain_slot(g, kind):
    other = _swap_half(g) if kind == "rot" else g
    return jnp.concatenate([g, other], axis=-1)


def _rope_table(pos):
    inv_freq = ROPE_THETA ** (-jnp.arange(HALF, dtype=F32) / HALF)
    ang = jnp.asarray(pos, F32)[:, None] * inv_freq[None, :]
    cos = jnp.cos(ang)
    sin = jnp.sin(ang)
    return jnp.concatenate([cos, cos, sin, sin], axis=-1)


def _pad_cols(w, width):
    return jnp.pad(w, ((0, 0), (0, width - w.shape[1])))


A_COL_Q = 0
A_COL_KS = A_COL_Q + NSA_HEADS * LANES
A_COL_KW = A_COL_KS + NSA_KV_HEADS * LANES
A_COL_VS = A_COL_KW + NSA_KV_HEADS * LANES
A_COL_VW = A_COL_VS + NSA_KV_HEADS * LANES
A_COL_KCVC = A_COL_VW + NSA_KV_HEADS * LANES
A_COL_QM = A_COL_KCVC + 3 * LANES
A_COL_G = A_COL_QM + MEM_HEADS * LANES
A_COLS = A_COL_G + NSA_KV_HEADS * LANES


def _prep_a_w_in(w):
    o = 0
    q = w[:, o:o + A_Q]; o += A_Q
    kc = w[:, o:o + A_KV]; o += A_KV
    vc = w[:, o:o + A_KV]; o += A_KV
    ks = w[:, o:o + A_KV]; o += A_KV
    vs = w[:, o:o + A_KV]; o += A_KV
    kw = w[:, o:o + A_KV]; o += A_KV
    vw = w[:, o:o + A_KV]; o += A_KV
    qm = w[:, o:o + MEM_W]; o += MEM_W
    gl = w[:, o:]
    gl = gl.reshape(D_MODEL, NSA_KV_HEADS, NSA_GQA, 3).transpose(0, 1, 3, 2)
    gl = gl.reshape(D_MODEL, NSA_KV_HEADS, 3 * NSA_GQA)
    gl = jnp.pad(gl, ((0, 0), (0, 0), (0, LANES - 3 * NSA_GQA))).reshape(D_MODEL, NSA_KV_HEADS * LANES)
    cols = [_slots(q, NSA_HEADS, "rot"), _slots(ks, NSA_KV_HEADS, "rot"), _slots(kw, NSA_KV_HEADS, "rot"),
            _slots(vs, NSA_KV_HEADS, "dup"), _slots(vw, NSA_KV_HEADS, "dup"),
            _pad_cols(jnp.concatenate([kc, vc], axis=1), 3 * LANES),
            _slots(qm, MEM_HEADS, "dup"), gl]
    return jnp.concatenate(cols, axis=1).astype(BF16)


def _a_proj_kernel(x_ref, gain_ref, w_ref, tab_ref, gt_ref,
                   qa_ref, ks_ref, kw_ref, vs_ref, vw_ref, kcvc_ref, qm_ref, g_ref, *, seq, tm):
    x = x_ref[...]
    hn = _rms_rows(x, gain_ref[...]).astype(BF16)
    tab = tab_ref[...]
    lo = _lane_lo(tm)
    lane = lax.broadcasted_iota(jnp.int32, (tm, LANES), 1)
    row = lax.broadcasted_iota(jnp.int32, (tm, LANES), 0)
    tok = (pl.program_id(0) % (seq // tm)) * tm + row
    blk_ind = jnp.where(lane - HEAD_DIM == tok // SEL_BLOCK, 1.0, 0.0)

    yq = _dot(hn, w_ref[:, A_COL_Q:A_COL_KS])
    for h in range(NSA_HEADS):
        z = _slot_norm(yq[:, h * LANES:(h + 1) * LANES], gt_ref[0:1, :], tab)
        qa_ref[:, h * LANES:(h + 1) * LANES] = jnp.where(lo, z * Q_SCALE, 0.0).astype(BF16)

    yk = _dot(hn, w_ref[:, A_COL_KS:A_COL_VS])
    for g in range(NSA_KV_HEADS):
        z = _slot_norm(yk[:, g * LANES:(g + 1) * LANES], gt_ref[1:2, :], tab)
        ks_ref[:, g * LANES:(g + 1) * LANES] = jnp.where(lo, z, blk_ind).astype(BF16)
        c = (NSA_KV_HEADS + g) * LANES
        z = _slot_norm(yk[:, c:c + LANES], gt_ref[2:3, :], tab)
        kw_ref[:, g * LANES:(g + 1) * LANES] = jnp.where(lo, z, 0.0).astype(BF16)

    yv = _dot(hn, w_ref[:, A_COL_VS:A_COL_KCVC])
    for j, ref in enumerate((vs_ref, vw_ref)):
        for g in range(NSA_KV_HEADS):
            c = (j * NSA_KV_HEADS + g) * LANES
            ref[:, g * LANES:(g + 1) * LANES] = jnp.where(lo, yv[:, c:c + LANES], 1.0).astype(BF16)

    yr = _dot(hn, w_ref[:, A_COL_KCVC:A_COLS])
    kcvc_ref[...] = yr[:, :3 * LANES]
    for h in range(MEM_HEADS):
        c = 3 * LANES + h * LANES
        z = _slot_norm(yr[:, c:c + LANES], gt_ref[3:4, :], None)
        qm_ref[:, h * LANES:(h + 1) * LANES] = jnp.where(lo, z * Q_SCALE, 0.0).astype(BF16)
    c = (3 + MEM_HEADS) * LANES
    g_ref[...] = jax.nn.sigmoid(yr[:, c:c + NSA_KV_HEADS * LANES])


def _a_proj(x2d, gain, w, tab, gt, seq):
    n = x2d.shape[0]
    tm = ROW_TILE
    nseq = seq // tm
    row_spec = lambda width: pl.BlockSpec((tm, width), lambda i: (i, 0))
    kvw = NSA_KV_HEADS * LANES
    widths = (NSA_HEADS * LANES, kvw, kvw, kvw, kvw, 3 * LANES, MEM_HEADS * LANES, kvw)
    dtypes = (BF16, BF16, BF16, BF16, BF16, F32, BF16, F32)
    return pl.pallas_call(
        functools.partial(_a_proj_kernel, seq=seq, tm=tm),
        grid=(n // tm,),
        in_specs=[row_spec(D_MODEL), _const_spec((1, D_MODEL)), _const_spec((D_MODEL, A_COLS)),
                  pl.BlockSpec((tm, LANES), lambda i: (i % nseq, 0)), _const_spec((SUBLANES, LANES))],
        out_specs=[row_spec(wd) for wd in widths],
        out_shape=[jax.ShapeDtypeStruct((n, wd), dt) for wd, dt in zip(widths, dtypes)],
        compiler_params=_cparams(1), name="a_proj",
    )(x2d, gain, w, tab, gt)


def _compress_kernel(hk_ref, hv_ref, pos_ref, w1_ref, b1_ref, w2_ref, b2_ref, tab_ref, gt_ref,
                     kc_ref, vc_ref):
    nc = hk_ref.shape[0]
    lo = _lane_lo(nc)
    for i, h_ref in enumerate((hk_ref, hv_ref)):
        h = h_ref[...]
        top = _dot((h + pos_ref[i, 0:1, :]).astype(BF16), w1_ref[i, 0])
        bot = _dot((h + pos_ref[i, 1:2, :]).astype(BF16), w1_ref[i, 1])
        pre = top + pltpu.roll(bot, nc - 1, 0) + b1_ref[i:i + 1, :]
        hid = jax.nn.gelu(pre).astype(BF16)
        y = _dot(hid, w2_ref[i]) + b2_ref[i:i + 1, :]
        if i == 0:
            z = _slot_norm(y, gt_ref[0:1, :], tab_ref[...])
            kc_ref[...] = jnp.where(lo, z, 0.0).astype(BF16)
        else:
            vc_ref[...] = jnp.where(lo, y, 1.0).astype(BF16)


def _compress(hk, hv, pos, w1, b1, w2, b2, tab_c, gt):
    bg, nc, _ = hk.shape
    hspec = pl.BlockSpec((None, nc, CMP_STRIDE * HEAD_DIM), lambda i: (i, 0, 0))
    ospec = pl.BlockSpec((None, nc, LANES), lambda i: (i, 0, 0))
    return pl.pallas_call(
        _compress_kernel,
        grid=(bg,),
        in_specs=[hspec, hspec, _const_spec(pos.shape), _const_spec(w1.shape), _const_spec(b1.shape),
                  _const_spec(w2.shape), _const_spec(b2.shape), _const_spec(tab_c.shape),
                  _const_spec(gt.shape)],
        out_specs=[ospec, ospec],
        out_shape=[jax.ShapeDtypeStruct((bg, nc, LANES), BF16)] * 2,
        compiler_params=_cparams(1), name="compress",
    )(hk, hv, pos, w1, b1, w2, b2, tab_c, gt)


def _gate_col(g_ref, branch, r, tq):
    c = branch * NSA_GQA + r
    return jnp.broadcast_to(g_ref[:, c:c + 1], (tq, LANES))


def _store_group(o_ref, xs, tq):
    lo = _lane_lo(tq)
    o_ref[:, :LANES] = _pack_pair(xs[0], xs[1], lo)
    o_ref[:, LANES:] = _pack_pair(xs[2], xs[3], lo)


def _cmp_kernel(q_ref, kc_ref, vc_ref, g_ref, ov_ref, oc_ref, bias_ref, *, tq, n_cmp, n_blk, n_sel):
    i = pl.program_id(2)
    kc = kc_ref[...]
    vc = vc_ref[...]
    nc = kc.shape[0]
    t_row = i * tq + lax.broadcasted_iota(jnp.int32, (1, tq), 1)
    c_col = lax.broadcasted_iota(jnp.int32, (nc, 1), 0)
    cend_col = jnp.where(c_col < n_cmp, c_col * CMP_STRIDE + (CMP_BLOCK - 1), jnp.int32(2 ** 30))
    mask_t = cend_col <= t_row
    psum = jnp.zeros((nc, tq), F32)
    xs = []
    for r in range(NSA_GQA):
        st = jnp.where(mask_t, _dot_nt(kc, q_ref[:, r * LANES:(r + 1) * LANES]), NEG)
        mt = jnp.max(st, axis=0, keepdims=True)
        pt = jnp.where(mask_t, jnp.exp2(st - mt), 0.0)
        lt = jnp.sum(pt, axis=0, keepdims=True)
        pt = pt * jnp.where(lt > 0.0, 1.0 / lt, 0.0)
        psum = psum + pt
        xs.append(_dot(pt.T.astype(BF16), vc) * _gate_col(g_ref, 0, r, tq))
    _store_group(oc_ref, xs, tq)

    ps_hi = psum.astype(BF16)
    ps_lo = (psum - ps_hi.astype(F32)).astype(BF16)
    ov = ov_ref[...]
    imp = _dot(ov, ps_hi) + _dot(ov, ps_lo)

    cur = t_row // SEL_BLOCK
    ngrp = n_blk // SUBLANES
    jsub = lax.broadcasted_iota(jnp.int32, (SUBLANES, tq), 0)
    vals = []
    for a in range(ngrp):
        j = jsub + a * SUBLANES
        forced = (j == 0) | (j == cur) | (j == cur - 1)
        v = jnp.where(j <= cur, imp[a * SUBLANES:(a + 1) * SUBLANES], NEG)
        vals.append(jnp.where(forced, SEL_FORCE, v))
    ranks = [jnp.zeros((SUBLANES, tq), F32) for _ in range(ngrp)]
    for jj in range(n_blk):
        a0, s0 = divmod(jj, SUBLANES)
        rowb = jnp.broadcast_to(vals[a0][s0:s0 + 1, :], (SUBLANES, tq))
        for a in range(ngrp):
            if a > a0:
                beats = jnp.where(rowb >= vals[a], 1.0, 0.0)
            elif a < a0:
                beats = jnp.where(rowb > vals[a], 1.0, 0.0)
            else:
                beats = jnp.where(jsub > s0, jnp.where(rowb >= vals[a], 1.0, 0.0),
                                  jnp.where(rowb > vals[a], 1.0, 0.0))
            ranks[a] = ranks[a] + beats
    parts = [jnp.zeros((HEAD_DIM, tq), F32)]
    for a in range(ngrp):
        live = jnp.where(vals[a] > NEG / 2, 0.0, NEG)
        parts.append(jnp.where(ranks[a] < n_sel, live, NEG))
    if n_blk < HEAD_DIM:
        parts.append(jnp.zeros((HEAD_DIM - n_blk, tq), F32))
    bias_t = jnp.concatenate(parts, axis=0)
    bias_ref[...] = bias_t.T.astype(BF16)


def _cmp_select(qa, kc, vc, gates, ov, n_cmp, n_sel):
    b, s, _ = qa.shape
    tq = CMP_Q_TILE
    nc = kc.shape[2]
    n_blk = s // SEL_BLOCK
    gw = NSA_GQA * LANES
    kern = functools.partial(_cmp_kernel, tq=tq, n_cmp=n_cmp, n_blk=n_blk, n_sel=n_sel)
    kv_spec = pl.BlockSpec((None, None, nc, LANES), lambda b_, g, i: (b_, g, 0, 0))
    return pl.pallas_call(
        kern,
        grid=(b, NSA_KV_HEADS, s // tq),
        in_specs=[pl.BlockSpec((None, tq, gw), lambda b_, g, i: (b_, i, g)), kv_spec, kv_spec,
                  pl.BlockSpec((None, tq, LANES), lambda b_, g, i: (b_, i, g)),
                  _const_spec(ov.shape)],
        out_specs=[pl.BlockSpec((None, tq, 2 * LANES), lambda b_, g, i: (b_, i, g)),
                   pl.BlockSpec((None, None, tq, LANES), lambda b_, g, i: (b_, g, i, 0))],
        out_shape=[jax.ShapeDtypeStruct((b, s, NSA_KV_HEADS * 2 * LANES), F32),
                   jax.ShapeDtypeStruct((b, NSA_KV_HEADS, s, LANES), BF16)],
        compiler_params=_cparams(3), name="cmp_select",
    )(qa, kc, vc, gates, ov)


def _stack_q(q_ref, extra=None):
    parts = []
    for r in range(NSA_GQA):
        q = q_ref[:, r * LANES:(r + 1) * LANES]
        parts.append(q if extra is None else q + extra)
    return jnp.concatenate(parts, axis=0)


def _sel_kernel(q_ref, bias_ref, k_ref, v_ref, g_ref, o_ref, m_ref, acc_ref, *, tq, tk):
    i = pl.program_id(2)
    rows = NSA_GQA * tq
    q4 = _stack_q(q_ref, bias_ref[...])
    m_ref[...] = jnp.full((rows, LANES), NEG, F32)
    acc_ref[...] = jnp.zeros((rows, LANES), F32)
    rep = tk // LANES

    def step(kt, masked):
        k0 = pl.multiple_of(kt * tk, tk)
        s = _dot_nt(q4, k_ref[pl.ds(k0, tk), :])
        if masked:
            t_col = i * tq + (lax.broadcasted_iota(jnp.int32, (rows, 1), 0) & (tq - 1))
            kpos = k0 + lax.broadcasted_iota(jnp.int32, (1, tk), 1)
            s = jnp.where(kpos <= t_col, s, NEG)
        m_prev = m_ref[...]
        m_next = jnp.maximum(m_prev, jnp.max(s, axis=-1, keepdims=True))
        p = jnp.exp2(s - pltpu.repeat(m_next, rep, 1)).astype(BF16)
        m_ref[...] = m_next
        acc_ref[...] = acc_ref[...] * jnp.exp2(m_prev - m_next) + _dot(p, v_ref[pl.ds(k0, tk), :])

    kt_diag = (i * tq) // tk
    lax.fori_loop(0, kt_diag, lambda kt, c: (step(kt, False), c)[1], 0)
    step(kt_diag, True)
    xs = [_finish_slot(acc_ref[r * tq:(r + 1) * tq, :]) * _gate_col(g_ref, 1, r, tq)
          for r in range(NSA_GQA)]
    _store_group(o_ref, xs, tq)


def _sel_attention(qa, bias, ks, vs, gates):
    b, s, _ = qa.shape
    tq, tk = Q_TILE, min(SEL_KEY_TILE, s)
    gw = NSA_GQA * LANES
    kv_spec = pl.BlockSpec((None, s, LANES), lambda b_, g, i: (b_, 0, g))
    return pl.pallas_call(
        functools.partial(_sel_kernel, tq=tq, tk=tk),
        grid=(b, NSA_KV_HEADS, s // tq),
        in_specs=[pl.BlockSpec((None, tq, gw), lambda b_, g, i: (b_, i, g)),
                  pl.BlockSpec((None, None, tq, LANES), lambda b_, g, i: (b_, g, i, 0)),
                  kv_spec, kv_spec,
                  pl.BlockSpec((None, tq, LANES), lambda b_, g, i: (b_, i, g))],
        out_specs=pl.BlockSpec((None, tq, 2 * LANES), lambda b_, g, i: (b_, i, g)),
        out_shape=jax.ShapeDtypeStruct((b, s, NSA_KV_HEADS * 2 * LANES), F32),
        scratch_shapes=[pltpu.VMEM((NSA_GQA * tq, LANES), F32), pltpu.VMEM((NSA_GQA * tq, LANES), F32)],
        compiler_params=_cparams(3), name="sel_attention",
    )(qa, bias, ks, vs, gates)


def _win_kernel(q_ref, k_ref, v_ref, g_ref, o_ref, *, tq, nk):
    i = pl.program_id(2)
    rows = NSA_GQA * tq
    q4 = _stack_q(q_ref)
    k0 = pl.multiple_of(jnp.maximum(i * tq + tq - nk, 0), tq)
    s = _dot_nt(q4, k_ref[pl.ds(k0, nk), :])
    t_col = i * tq + (lax.broadcasted_iota(jnp.int32, (rows, 1), 0) & (tq - 1))
    dist = t_col - (k0 + lax.broadcasted_iota(jnp.int32, (1, nk), 1))
    s = jnp.where((dist >= 0) & (dist < WINDOW), s, NEG)
    m = jnp.max(s, axis=-1, keepdims=True)
    acc = _dot(jnp.exp2(s - m).astype(BF16), v_ref[pl.ds(k0, nk), :])
    xs = [_finish_slot(acc[r * tq:(r + 1) * tq, :]) * _gate_col(g_ref, 2, r, tq) for r in range(NSA_GQA)]
    _store_group(o_ref, xs, tq)


def _win_attention(qa, kw, vw, gates):
    b, s, _ = qa.shape
    tq = WIN_Q_TILE
    nk = min(WINDOW + tq, s)
    gw = NSA_GQA * LANES
    kv_spec = pl.BlockSpec((None, s, LANES), lambda b_, g, i: (b_, 0, g))
    return pl.pallas_call(
        functools.partial(_win_kernel, tq=tq, nk=nk),
        grid=(b, NSA_KV_HEADS, s // tq),
        in_specs=[pl.BlockSpec((None, tq, gw), lambda b_, g, i: (b_, i, g)), kv_spec, kv_spec,
                  pl.BlockSpec((None, tq, LANES), lambda b_, g, i: (b_, i, g))],
        out_specs=pl.BlockSpec((None, tq, 2 * LANES), lambda b_, g, i: (b_, i, g)),
        out_shape=jax.ShapeDtypeStruct((b, s, NSA_KV_HEADS * 2 * LANES), F32),
        compiler_params=_cparams(3), name="win_attention",
    )(qa, kw, vw, gates)


def _mem_kv_kernel(mem_ref, gain_ref, w_ref, gt_ref, k_ref, v_ref):
    nm = mem_ref.shape[0]
    lo = _lane_lo(nm)
    hn = _rms_rows(mem_ref[...], gain_ref[...]).astype(BF16)
    y = _dot(hn, w_ref[...])
    for h in range(MEM_HEADS):
        z = _slot_norm(y[:, h * LANES:(h + 1) * LANES], gt_ref[...], None)
        k_ref[:, h * LANES:(h + 1) * LANES] = jnp.where(lo, z, 0.0).astype(BF16)
        yv = y[:, (MEM_HEADS + h) * LANES:(MEM_HEADS + h + 1) * LANES]
        v_ref[:, h * LANES:(h + 1) * LANES] = jnp.where(lo, yv, 1.0).astype(BF16)


def _mem_kv(mem, gains, w, gt):
    b, nm, _ = mem.shape
    nl = gains.shape[0]
    width = MEM_HEADS * LANES
    out_spec = pl.BlockSpec((None, None, nm, width), lambda l, b_: (l, b_, 0, 0))
    return pl.pallas_call(
        _mem_kv_kernel,
        grid=(nl, b),
        in_specs=[pl.BlockSpec((None, nm, D_MODEL), lambda l, b_: (b_, 0, 0)),
                  pl.BlockSpec((None, 1, D_MODEL), lambda l, b_: (l, 0, 0)),
                  pl.BlockSpec((None, D_MODEL, 2 * width), lambda l, b_: (l, 0, 0)),
                  pl.BlockSpec((None, 1, LANES), lambda l, b_: (l, 0, 0))],
        out_specs=[out_spec, out_spec],
        out_shape=[jax.ShapeDtypeStruct((nl, b, nm, width), BF16)] * 2,
        compiler_params=_cparams(2), name="mem_kv",
    )(mem, gains, w, gt)


def _mem_attn_kernel(q_ref, k_ref, v_ref, o_ref, *, tq):
    lo = _lane_lo(tq)
    xs = []
    for h in range(MEM_HEADS):
        sl = slice(h * LANES, (h + 1) * LANES)
        s = _dot_nt(q_ref[:, sl], k_ref[:, sl])
        m = jnp.max(s, axis=-1, keepdims=True)
        xs.append(_finish_slot(_dot(jnp.exp2(s - m).astype(BF16), v_ref[:, sl])))
    for pair in range(MEM_HEADS // 2):
        o_ref[:, pair * LANES:(pair + 1) * LANES] = _pack_pair(xs[2 * pair], xs[2 * pair + 1], lo)


def _mem_attention(qm, mk, mv, layer):
    b, s, width = qm.shape
    nm = mk.shape[2]
    tq = min(MEM_Q_TILE, s)
    kv_spec = pl.BlockSpec((None, None, nm, width), lambda b_, i: (layer, b_, 0, 0))
    return pl.pallas_call(
        functools.partial(_mem_attn_kernel, tq=tq),
        grid=(b, s // tq),
        in_specs=[pl.BlockSpec((None, tq, width), lambda b_, i: (b_, i, 0)), kv_spec, kv_spec],
        out_specs=pl.BlockSpec((None, tq, MEM_W), lambda b_, i: (b_, i, 0)),
        out_shape=jax.ShapeDtypeStruct((b, s, MEM_W), F32),
        compiler_params=_cparams(2), name="mem_attention",
    )(qm, mk, mv)


def _mlp_tail(x1, gain_ref, wup_ref, wdn_ref, out_ref):
    hn = _rms_rows(x1, gain_ref[...]).astype(BF16)
    out_ref[...] = x1
    for c in range(D_FF // FF_CHUNK):
        u = _dot(hn, wup_ref[:, c * FF_CHUNK:(c + 1) * FF_CHUNK])
        u = jnp.square(jnp.maximum(u, 0.0)).astype(BF16)
        out_ref[...] += _dot(u, wdn_ref[c * FF_CHUNK:(c + 1) * FF_CHUNK, :])


def _a_out_kernel(oc_ref, os_ref, ow_ref, om_ref, x_ref, wo_ref, gain_ref, wup_ref, wdn_ref, out_ref):
    o_main = (oc_ref[...] + os_ref[...] + ow_ref[...]).astype(BF16)
    x1 = x_ref[...] + _dot(o_main, wo_ref[:A_Q, :]) + _dot(om_ref[...].astype(BF16), wo_ref[A_Q:, :])
    _mlp_tail(x1, gain_ref, wup_ref, wdn_ref, out_ref)


def _b_out_kernel(od_ref, om_ref, x_ref, wo_ref, gain_ref, wup_ref, wdn_ref, out_ref):
    kd = DIL_SLOTS * HEAD_DIM
    x1 = (x_ref[...] + _dot(od_ref[...].astype(BF16), wo_ref[:kd, :])
          + _dot(om_ref[...].astype(BF16), wo_ref[kd:, :]))
    _mlp_tail(x1, gain_ref, wup_ref, wdn_ref, out_ref)


def _out_mlp(kern, acts, x2d, wo, gain, wup, wdn, name):
    n = x2d.shape[0]
    tm = ROW_TILE
    row_spec = lambda width: pl.BlockSpec((tm, width), lambda i: (i, 0))
    return pl.pallas_call(
        kern,
        grid=(n // tm,),
        in_specs=[row_spec(a.shape[1]) for a in acts] + [row_spec(D_MODEL), _const_spec(wo.shape),
                  _const_spec((1, D_MODEL)), _const_spec(wup.shape), _const_spec(wdn.shape)],
        out_specs=row_spec(D_MODEL),
        out_shape=jax.ShapeDtypeStruct((n, D_MODEL), F32),
        compiler_params=_cparams(1), name=name,
    )(*acts, x2d, wo, gain, wup, wdn)


B_QCOLS = N_DIL_GROUPS * DIL_SLOTS * LANES
B_COLS = B_QCOLS + MEM_HEADS * LANES
KV_COLS = 2 * DIL_SLOTS * LANES


def _b_proj_kernel(x_ref, gains_ref, wq_ref, wkv_ref, tab_ref, gt_ref,
                   q0_ref, q1_ref, q2_ref, qm_ref, k_ref, v_ref, *, tm):
    x = x_ref[...]
    xn = x * lax.rsqrt(jnp.mean(x * x, axis=-1, keepdims=True) + EPS)
    tab = tab_ref[...]
    lo = _lane_lo(tm)
    hq = (xn * gains_ref[0:1, :]).astype(BF16)
    width = DIL_SLOTS * LANES
    for gi, ref in enumerate((q0_ref, q1_ref, q2_ref)):
        y = _dot(hq, wq_ref[:, gi * width:(gi + 1) * width])
        for h in range(DIL_SLOTS):
            z = _slot_norm(y[:, h * LANES:(h + 1) * LANES], gt_ref[gi:gi + 1, :], tab)
            ref[:, h * LANES:(h + 1) * LANES] = jnp.where(lo, z * Q_SCALE, 0.0).astype(BF16)
    y = _dot(hq, wq_ref[:, B_QCOLS:B_COLS])
    for h in range(MEM_HEADS):
        z = _slot_norm(y[:, h * LANES:(h + 1) * LANES], gt_ref[3:4, :], None)
        qm_ref[:, h * LANES:(h + 1) * LANES] = jnp.where(lo, z * Q_SCALE, 0.0).astype(BF16)
    hk = (xn * gains_ref[1:2, :]).astype(BF16)
    y = _dot(hk, wkv_ref[...])
    for h in range(DIL_SLOTS):
        z = _slot_norm(y[:, h * LANES:(h + 1) * LANES], gt_ref[4:5, :], tab)
        k_ref[:, h * LANES:(h + 1) * LANES] = jnp.where(lo, z, 0.0).astype(BF16)
        yv = y[:, (DIL_SLOTS + h) * LANES:(DIL_SLOTS + h + 1) * LANES]
        v_ref[:, h * LANES:(h + 1) * LANES] = jnp.where(lo, yv, 1.0).astype(BF16)


def _b_proj(x2d, gains, wq, wkv, tab, gt, seq):
    n = x2d.shape[0]
    tm = ROW_TILE
    nseq = seq // tm
    row_spec = lambda width: pl.BlockSpec((tm, width), lambda i: (i, 0))
    width = DIL_SLOTS * LANES
    widths = (width, width, width, MEM_HEADS * LANES, width, width)
    return pl.pallas_call(
        functools.partial(_b_proj_kernel, tm=tm),
        grid=(n // tm,),
        in_specs=[row_spec(D_MODEL), _const_spec(gains.shape), _const_spec(wq.shape),
                  _const_spec(wkv.shape), pl.BlockSpec((tm, LANES), lambda i: (i % nseq, 0)),
                  _const_spec(gt.shape)],
        out_specs=[row_spec(wd) for wd in widths],
        out_shape=[jax.ShapeDtypeStruct((n, wd), BF16) for wd in widths],
        compiler_params=_cparams(1), name="b_proj",
    )(x2d, gains, wq, wkv, tab, gt)


def _dil_kernel(q0_ref, q1_ref, q2_ref, k_ref, v_ref, o_ref, qf_ref, kf_ref, vf_ref, lse_ref, *, seq):
    tq = DIL_Q_TILE
    heads = (slice(0, LANES), slice(LANES, 2 * LANES))
    for hd, sl in enumerate(heads):
        kf_ref[hd] = k_ref[:, sl].astype(F32)
        vf_ref[hd] = v_ref[:, sl].astype(F32)

    def attend(qs, ks, vs, qpos0, kpos0):
        nq, nk = qs[0].shape[0], ks[0].shape[0]
        lo = _lane_lo(nq)
        dist = (qpos0 + lax.broadcasted_iota(jnp.int32, (nq, 1), 0)
                - (kpos0 + lax.broadcasted_iota(jnp.int32, (1, nk), 1)))
        mask = (dist >= 0) & (dist <= DIL_BAND)
        xs, ls = [], []
        for q, k, v in zip(qs, ks, vs):
            s = jnp.where(mask, _dot_nt(q, k), NEG)
            m = jnp.max(s, axis=-1, keepdims=True)
            acc = _dot(jnp.exp2(s - m).astype(BF16), v)
            xs.append(_finish_slot(acc))
            ls.append(m + jnp.log2(jnp.where(lo, 1.0, acc)))
        return _pack_pair(xs[0], xs[1], lo), jnp.where(lo, _swap_halves(ls[0]), ls[1])

    def merge(o_old, l_old, o_new, l_new):
        mx = jnp.maximum(l_old, l_new)
        a = jnp.exp2(l_old - mx)
        b = jnp.exp2(l_new - mx)
        den = a + b
        return (a * o_old + b * o_new) / den, mx + jnp.log2(den)

    nk = min(tq + DIL_BAND, seq)

    def body0(i, c):
        q0 = pl.multiple_of(i * tq, tq)
        k0 = pl.multiple_of(jnp.clip(i * tq - DIL_BAND, 0, seq - nk), DIL_BAND)
        o, l = attend([q0_ref[pl.ds(q0, tq), sl] for sl in heads],
                      [k_ref[pl.ds(k0, nk), sl] for sl in heads],
                      [v_ref[pl.ds(k0, nk), sl] for sl in heads], q0, k0)
        o_ref[pl.ds(q0, tq), :] = o
        lse_ref[pl.ds(q0, tq), :] = l
        return c

    lax.fori_loop(0, seq // tq, body0, 0, unroll=DIL_UNROLL)

    for q_ref, (_, dil) in zip((q1_ref, q2_ref), DIL_PATTERNS[1:]):
        for hd, sl in enumerate(heads):
            qf_ref[hd] = q_ref[:, sl].astype(F32)
        length = seq // dil
        tqd = min(tq, length)
        nkd = min(tqd + DIL_BAND, length)
        ntile = length // tqd

        def body(it, c, dil=dil, length=length, tqd=tqd, nkd=nkd, ntile=ntile):
            r = it // ntile
            i = it - r * ntile
            qp = i * tqd
            kp = jnp.clip(qp - DIL_BAND, 0, length - nkd)
            qrows = pl.ds(r + dil * qp, tqd, stride=dil)
            krows = pl.ds(r + dil * kp, nkd, stride=dil)
            o, l = attend([qf_ref[hd, qrows, :].astype(BF16) for hd in range(2)],
                          [kf_ref[hd, krows, :].astype(BF16) for hd in range(2)],
                          [vf_ref[hd, krows, :].astype(BF16) for hd in range(2)], qp, kp)
            o, l = merge(o_ref[qrows, :], lse_ref[qrows, :], o, l)
            o_ref[qrows, :] = o
            lse_ref[qrows, :] = l
            return c

        lax.fori_loop(0, dil * ntile, body, 0, unroll=DIL_UNROLL)


def _dil_attention(q0, q1, q2, k, v):
    b, s, _ = k.shape
    npair = DIL_SLOTS // 2
    pw = 2 * LANES
    in_spec = pl.BlockSpec((None, s, pw), lambda b_, hp: (b_, 0, hp))
    return pl.pallas_call(
        functools.partial(_dil_kernel, seq=s),
        grid=(b, npair),
        in_specs=[in_spec] * 5,
        out_specs=pl.BlockSpec((None, s, LANES), lambda b_, hp: (b_, 0, hp)),
        out_shape=jax.ShapeDtypeStruct((b, s, DIL_SLOTS * HEAD_DIM), F32),
        scratch_shapes=[pltpu.VMEM((2, s, LANES), F32)] * 3 + [pltpu.VMEM((s, LANES), F32)],
        compiler_params=_cparams(2), name="dil_attention",
    )(q0, q1, q2, k, v)


def kernel(x, mem, attn_norm, mlp_norm, w_up, w_down, mem_norm, w_mem_kv, mem_q_norm, mem_k_norm,
           a_w_in, a_w_out, a_q_norm, a_k_norm, a_cmp_pos, a_cmp_w1, a_cmp_b1, a_cmp_w2, a_cmp_b2,
           kv_norm, w_kv_shared, kv_k_norm, b_w_in, b_w_out, b_q_norm):
    b, s, _ = x.shape
    n = b * s
    assert s % ROW_TILE == 0 and s % SEL_KEY_TILE == 0 and s // SEL_BLOCK <= HEAD_DIM
    assert s >= WINDOW + WIN_Q_TILE and (s // DIL_PATTERNS[-1][1]) % DIL_BAND == 0
    x2d = x.reshape(n, D_MODEL)
    tab = _rope_table(jnp.arange(s))

    wk, wv = w_mem_kv[..., :MEM_W], w_mem_kv[..., MEM_W:]
    w_mem = jnp.concatenate([jax.vmap(lambda w: _slots(w, MEM_HEADS, "dup"))(wk),
                             jax.vmap(lambda w: _slots(w, MEM_HEADS, "dup"))(wv)], axis=-1).astype(BF16)
    mk, mv = _mem_kv(mem, mem_norm[:, None, :], w_mem, _gain_slot(mem_k_norm, "dup")[:, None, :])

    gt_a = jnp.zeros((SUBLANES, LANES), F32)
    gt_a = gt_a.at[0].set(_gain_slot(a_q_norm[0], "rot")).at[1].set(_gain_slot(a_k_norm[0, 1], "rot"))
    gt_a = gt_a.at[2].set(_gain_slot(a_k_norm[0, 2], "rot")).at[3].set(_gain_slot(mem_q_norm[0], "dup"))
    qa, ks, kw, vs, vw, kcvc, qm, gates = _a_proj(
        x2d, attn_norm[0:1], _prep_a_w_in(a_w_in[0]), tab, gt_a, s)

    n_cmp = (s - CMP_BLOCK) // CMP_STRIDE + 1
    nc = s // CMP_STRIDE

    def half_blocks(t):
        t = t.reshape(b, s, NSA_KV_HEADS, HEAD_DIM).transpose(0, 2, 1, 3)
        return t.reshape(b * NSA_KV_HEADS, nc, CMP_STRIDE * HEAD_DIM)

    w2 = jnp.stack([_slots(a_cmp_w2[0, 0], 1, "rot"), _slots(a_cmp_w2[0, 1], 1, "dup")]).astype(BF16)
    b2 = jnp.stack([_slots(a_cmp_b2[0, 0][None], 1, "rot")[0], _slots(a_cmp_b2[0, 1][None], 1, "dup")[0]])
    tab_c = _rope_table(jnp.arange(nc) * CMP_STRIDE + (CMP_BLOCK - 1))
    kc, vc = _compress(
        half_blocks(kcvc[:, :A_KV]), half_blocks(kcvc[:, A_KV:2 * A_KV]),
        a_cmp_pos[0].reshape(2, 2, CMP_STRIDE * HEAD_DIM),
        a_cmp_w1[0].reshape(2, 2, CMP_STRIDE * HEAD_DIM, CMP_HIDDEN).astype(BF16), a_cmp_b1[0],
        w2, b2, tab_c, _gain_slot(a_k_norm[0, 0], "rot")[None])
    kc = kc.reshape(b, NSA_KV_HEADS, nc, LANES)
    vc = vc.reshape(b, NSA_KV_HEADS, nc, LANES)

    n_blk = s // SEL_BLOCK
    c_start = np.arange(nc)[None, :] * CMP_STRIDE
    b_start = np.arange(n_blk)[:, None] * SEL_BLOCK
    overlap_t = ((c_start < b_start + SEL_BLOCK) & (c_start + CMP_BLOCK > b_start)
                 & (np.arange(nc)[None, :] < n_cmp))
    ov = jnp.asarray(overlap_t, BF16)

    qa3 = qa.reshape(b, s, -1)
    gates3 = gates.reshape(b, s, -1)
    o_c, bias = _cmp_select(qa3, kc, vc, gates3, ov, n_cmp, min(SEL_TOPK, n_blk))
    o_s = _sel_attention(qa3, bias, ks.reshape(b, s, -1), vs.reshape(b, s, -1), gates3)
    o_w = _win_attention(qa3, kw.reshape(b, s, -1), vw.reshape(b, s, -1), gates3)
    o_m = _mem_attention(qm.reshape(b, s, -1), mk, mv, 0)
    x2d = _out_mlp(_a_out_kernel,
                   [o_c.reshape(n, -1), o_s.reshape(n, -1), o_w.reshape(n, -1), o_m.reshape(n, -1)],
                   x2d, a_w_out[0].astype(BF16), mlp_norm[0:1], w_up[0].astype(BF16),
                   w_down[0].astype(BF16), "a_out_mlp")

    wq = jnp.concatenate([_slots(b_w_in[0][:, :B_Q], N_DIL_GROUPS * DIL_SLOTS, "rot"),
                          _slots(b_w_in[0][:, B_Q:], MEM_HEADS, "dup")], axis=1).astype(BF16)
    kd = DIL_SLOTS * HEAD_DIM
    wkv = jnp.concatenate([_slots(w_kv_shared[:, :kd], DIL_SLOTS, "rot"),
                           _slots(w_kv_shared[:, kd:], DIL_SLOTS, "dup")], axis=1).astype(BF16)
    gt_b = jnp.zeros((SUBLANES, LANES), F32)
    for gi in range(N_DIL_GROUPS):
        gt_b = gt_b.at[gi].set(_gain_slot(b_q_norm[0, gi], "rot"))
    gt_b = gt_b.at[3].set(_gain_slot(mem_q_norm[1], "dup")).at[4].set(_gain_slot(kv_k_norm, "rot"))
    q0, q1, q2, qm1, kb, vb = _b_proj(x2d, jnp.stack([attn_norm[1], kv_norm]), wq, wkv, tab, gt_b, s)
    o_d = _dil_attention(*(t.reshape(b, s, -1) for t in (q0, q1, q2, kb, vb)))
    o_m = _mem_attention(qm1.reshape(b, s, -1), mk, mv, 1)
    x2d = _out_mlp(_b_out_kernel, [o_d.reshape(n, -1), o_m.reshape(n, -1)], x2d, b_w_out[0].astype(BF16),
                   mlp_norm[1:2], w_up[1].astype(BF16), w_down[1].astype(BF16), "b_out_mlp")
    return x2d.reshape(b, s, D_MODEL)
```

```python
import functools

import numpy as np
import jax
import jax.numpy as jnp
from jax import lax
from jax.experimental import pallas as pl
from jax.experimental.pallas import tpu as pltpu

D_MODEL = 1024
HEAD_DIM = 64
HALF = HEAD_DIM // 2
ROPE_THETA = 10000.0
EPS = 1e-6
NEG = -1e30
D_FF = 4 * D_MODEL
MEM_HEADS = 4
NSA_HEADS = 12
NSA_KV_HEADS = 3
NSA_GQA = NSA_HEADS // NSA_KV_HEADS
CMP_BLOCK = 32
CMP_STRIDE = 16
CMP_HIDDEN = 256
SEL_BLOCK = 64
SEL_TOPK = 16
WINDOW = 512
SEL_FORCE = 1e9
DIL_PATTERNS = ((128, 1), (512, 4), (2048, 16))
N_DIL_GROUPS = 3
DIL_SLOTS = 8
MEM_W = MEM_HEADS * HEAD_DIM
A_Q = NSA_HEADS * HEAD_DIM
A_KV = NSA_KV_HEADS * HEAD_DIM
B_Q = N_DIL_GROUPS * DIL_SLOTS * HEAD_DIM
LOG2E = 1.4426950408889634
Q_SCALE = HEAD_DIM ** -0.5 * LOG2E

LANES = 128
SUBLANES = 8
VMEM_LIMIT = 56 * 1024 * 1024
ROW_TILE = 512
Q_TILE = 256
CMP_Q_TILE = 256
DIL_UNROLL = 4
SEL_KEY_TILE = 512
MEM_Q_TILE = 512
DIL_Q_TILE = 128
DIL_BAND = 128
FF_CHUNK = 1024

BF16 = jnp.bfloat16
F32 = jnp.float32


def _cparams(n_grid):
    return pltpu.CompilerParams(dimension_semantics=("arbitrary",) * n_grid,
                                vmem_limit_bytes=VMEM_LIMIT)


def _const_spec(shape):
    nd = len(shape)
    return pl.BlockSpec(shape, lambda *_: (0,) * nd, pipeline_mode=pl.Buffered(1))


def _dot(a, b):
    return jnp.dot(a, b, preferred_element_type=F32)


def _dot_nt(a, b):
    return lax.dot_general(a, b, (((1,), (1,)), ((), ())), preferred_element_type=F32)


def _rms_rows(x, gain):
    return x * lax.rsqrt(jnp.mean(x * x, axis=-1, keepdims=True) + EPS) * gain


def _slot_norm(y, gt, tab):
    z = y * lax.rsqrt(jnp.mean(y * y, axis=-1, keepdims=True) + EPS) * gt
    if tab is not None:
        z = z * tab
        z = z + pltpu.roll(z, HEAD_DIM, 1)
    return z


def _lane_lo(rows):
    return lax.broadcasted_iota(jnp.int32, (rows, LANES), 1) < HEAD_DIM


def _swap_halves(x):
    return pltpu.roll(x, HEAD_DIM, 1)


def _value_slot(y, lo, head):
    return jnp.where(lo, y, 1.0) if head % 2 == 0 else jnp.where(lo, 1.0, y)


def _pair_num_den(acc_even, acc_odd, lo):
    return jnp.where(lo, acc_even, acc_odd), _swap_halves(jnp.where(lo, acc_odd, acc_even))


def _rot_half_cols(w):
    return jnp.concatenate([-w[..., HALF:], w[..., :HALF]], axis=-1)


def _swap_half(g):
    return jnp.concatenate([g[..., HALF:], g[..., :HALF]], axis=-1)


def _slots(w, n, kind):
    k = w.shape[0]
    w = w.reshape(k, n, HEAD_DIM)
    other = _rot_half_cols(w) if kind == "rot" else w
    return jnp.concatenate([w, other], axis=-1).reshape(k, n * LANES)


def _gain_slot(g, kind):
    other = _swap_half(g) if kind == "rot" else g
    return jnp.concatenate([g, other], axis=-1)


def _rope_table(pos):
    inv_freq = ROPE_THETA ** (-jnp.arange(HALF, dtype=F32) / HALF)
    ang = jnp.asarray(pos, F32)[:, None] * inv_freq[None, :]
    cos = jnp.cos(ang)
    sin = jnp.sin(ang)
    return jnp.concatenate([cos, cos, sin, sin], axis=-1)


def _pad_cols(w, width):
    return jnp.pad(w, ((0, 0), (0, width - w.shape[1])))


A_COL_Q = 0
A_COL_KS = A_COL_Q + NSA_HEADS * LANES
A_COL_KW = A_COL_KS + NSA_KV_HEADS * LANES
A_COL_VS = A_COL_KW + NSA_KV_HEADS * LANES
A_COL_VW = A_COL_VS + NSA_KV_HEADS * LANES
A_COL_KCVC = A_COL_VW + NSA_KV_HEADS * LANES
A_COL_QM = A_COL_KCVC + 3 * LANES
A_COL_G = A_COL_QM + MEM_HEADS * LANES
A_COLS = A_COL_G + NSA_KV_HEADS * LANES


def _prep_a_w_in(w):
    o = 0
    q = w[:, o:o + A_Q]; o += A_Q
    kc = w[:, o:o + A_KV]; o += A_KV
    vc = w[:, o:o + A_KV]; o += A_KV
    ks = w[:, o:o + A_KV]; o += A_KV
    vs = w[:, o:o + A_KV]; o += A_KV
    kw = w[:, o:o + A_KV]; o += A_KV
    vw = w[:, o:o + A_KV]; o += A_KV
    qm = w[:, o:o + MEM_W]; o += MEM_W
    gl = w[:, o:]
    gl = gl.reshape(D_MODEL, NSA_KV_HEADS, NSA_GQA, 3).transpose(0, 1, 3, 2)
    gl = gl.reshape(D_MODEL, NSA_KV_HEADS, 3 * NSA_GQA)
    gl = jnp.pad(gl, ((0, 0), (0, 0), (0, LANES - 3 * NSA_GQA))).reshape(D_MODEL, NSA_KV_HEADS * LANES)
    cols = [_slots(q, NSA_HEADS, "rot"), _slots(ks, NSA_KV_HEADS, "rot"), _slots(kw, NSA_KV_HEADS, "rot"),
            _slots(vs, NSA_KV_HEADS, "dup"), _slots(vw, NSA_KV_HEADS, "dup"),
            _pad_cols(jnp.concatenate([kc, vc], axis=1), 3 * LANES),
            _slots(qm, MEM_HEADS, "dup"), gl]
    return jnp.concatenate(cols, axis=1).astype(BF16)


def _a_proj_kernel(x_ref, gain_ref, w_ref, tab_ref, gt_ref,
                   qa_ref, ks_ref, kw_ref, vs_ref, vw_ref, kcvc_ref, qm_ref, g_ref, *, seq, tm):
    x = x_ref[...]
    hn = _rms_rows(x, gain_ref[...]).astype(BF16)
    tab = tab_ref[...]
    lo = _lane_lo(tm)
    lane = lax.broadcasted_iota(jnp.int32, (tm, LANES), 1)
    row = lax.broadcasted_iota(jnp.int32, (tm, LANES), 0)
    tok = (pl.program_id(0) % (seq // tm)) * tm + row
    blk_ind = jnp.where(lane - HEAD_DIM == tok // SEL_BLOCK, 1.0, 0.0)

    yq = _dot(hn, w_ref[:, A_COL_Q:A_COL_KS])
    for h in range(NSA_HEADS):
        z = _slot_norm(yq[:, h * LANES:(h + 1) * LANES], gt_ref[0:1, :], tab)
        qa_ref[:, h * LANES:(h + 1) * LANES] = jnp.where(lo, z * Q_SCALE, 0.0).astype(BF16)

    yk = _dot(hn, w_ref[:, A_COL_KS:A_COL_VS])
    for g in range(NSA_KV_HEADS):
        z = _slot_norm(yk[:, g * LANES:(g + 1) * LANES], gt_ref[1:2, :], tab)
        ks_ref[:, g * LANES:(g + 1) * LANES] = jnp.where(lo, z, blk_ind).astype(BF16)
        c = (NSA_KV_HEADS + g) * LANES
        z = _slot_norm(yk[:, c:c + LANES], gt_ref[2:3, :], tab)
        kw_ref[:, g * LANES:(g + 1) * LANES] = jnp.where(lo, z, 0.0).astype(BF16)

    yv = _dot(hn, w_ref[:, A_COL_VS:A_COL_KCVC])
    for j, ref in enumerate((vs_ref, vw_ref)):
        for g in range(NSA_KV_HEADS):
            c = (j * NSA_KV_HEADS + g) * LANES
            y = yv[:, c:c + LANES]
            ref[:, 2 * g * LANES:(2 * g + 1) * LANES] = jnp.where(lo, y, 1.0).astype(BF16)
            ref[:, (2 * g + 1) * LANES:(2 * g + 2) * LANES] = jnp.where(lo, 1.0, y).astype(BF16)

    yr = _dot(hn, w_ref[:, A_COL_KCVC:A_COLS])
    kcvc_ref[...] = yr[:, :3 * LANES]
    for h in range(MEM_HEADS):
        c = 3 * LANES + h * LANES
        z = _slot_norm(yr[:, c:c + LANES], gt_ref[3:4, :], None)
        qm_ref[:, h * LANES:(h + 1) * LANES] = jnp.where(lo, z * Q_SCALE, 0.0).astype(BF16)
    c = (3 + MEM_HEADS) * LANES
    g_ref[...] = jax.nn.sigmoid(yr[:, c:c + NSA_KV_HEADS * LANES])


def _a_proj(x2d, gain, w, tab, gt, seq):
    n = x2d.shape[0]
    tm = ROW_TILE
    nseq = seq // tm
    row_spec = lambda width: pl.BlockSpec((tm, width), lambda i: (i, 0))
    kvw = NSA_KV_HEADS * LANES
    widths = (NSA_HEADS * LANES, kvw, kvw, 2 * kvw, 2 * kvw, 3 * LANES, MEM_HEADS * LANES, kvw)
    dtypes = (BF16, BF16, BF16, BF16, BF16, F32, BF16, F32)
    return pl.pallas_call(
        functools.partial(_a_proj_kernel, seq=seq, tm=tm),
        grid=(n // tm,),
        in_specs=[row_spec(D_MODEL), _const_spec((1, D_MODEL)), _const_spec((D_MODEL, A_COLS)),
                  pl.BlockSpec((tm, LANES), lambda i: (i % nseq, 0)), _const_spec((SUBLANES, LANES))],
        out_specs=[row_spec(wd) for wd in widths],
        out_shape=[jax.ShapeDtypeStruct((n, wd), dt) for wd, dt in zip(widths, dtypes)],
        compiler_params=_cparams(1), name="a_proj",
    )(x2d, gain, w, tab, gt)


def _compress_kernel(hk_ref, hv_ref, pos_ref, w1_ref, b1_ref, w2_ref, b2_ref, tab_ref, gt_ref,
                     kc_ref, vc_ref):
    nc = hk_ref.shape[0]
    lo = _lane_lo(nc)
    for i, h_ref in enumerate((hk_ref, hv_ref)):
        h = h_ref[...]
        top = _dot((h + pos_ref[i, 0:1, :]).astype(BF16), w1_ref[i, 0])
        bot = _dot((h + pos_ref[i, 1:2, :]).astype(BF16), w1_ref[i, 1])
        pre = top + pltpu.roll(bot, nc - 1, 0) + b1_ref[i:i + 1, :]
        hid = jax.nn.gelu(pre).astype(BF16)
        y = _dot(hid, w2_ref[i]) + b2_ref[i:i + 1, :]
        if i == 0:
            z = _slot_norm(y, gt_ref[0:1, :], tab_ref[...])
            kc_ref[...] = jnp.where(lo, z, 0.0).astype(BF16)
        else:
            vc_ref[:, :LANES] = jnp.where(lo, y, 1.0).astype(BF16)
            vc_ref[:, LANES:] = jnp.where(lo, 1.0, y).astype(BF16)


def _compress(hk, hv, pos, w1, b1, w2, b2, tab_c, gt):
    bg, nc, _ = hk.shape
    hspec = pl.BlockSpec((None, nc, CMP_STRIDE * HEAD_DIM), lambda i: (i, 0, 0))
    ospec = lambda width: pl.BlockSpec((None, nc, width), lambda i: (i, 0, 0))
    return pl.pallas_call(
        _compress_kernel,
        grid=(bg,),
        in_specs=[hspec, hspec, _const_spec(pos.shape), _const_spec(w1.shape), _const_spec(b1.shape),
                  _const_spec(w2.shape), _const_spec(b2.shape), _const_spec(tab_c.shape),
                  _const_spec(gt.shape)],
        out_specs=[ospec(LANES), ospec(2 * LANES)],
        out_shape=[jax.ShapeDtypeStruct((bg, nc, LANES), BF16),
                   jax.ShapeDtypeStruct((bg, nc, 2 * LANES), BF16)],
        compiler_params=_cparams(1), name="compress",
    )(hk, hv, pos, w1, b1, w2, b2, tab_c, gt)


STACK_ORDER = (0, 2, 1, 3)


def _stack_q(q_ref, extra=None):
    parts = []
    for r in STACK_ORDER:
        q = q_ref[:, r * LANES:(r + 1) * LANES]
        parts.append(q if extra is None else q + extra)
    return jnp.concatenate(parts, axis=0)


def _gate_pairs(g_ref, branch, tq):
    lo = _lane_lo(tq)
    col = lambda r: jnp.broadcast_to(g_ref[:, branch * NSA_GQA + r:branch * NSA_GQA + r + 1], (tq, LANES))
    return jnp.concatenate([jnp.where(lo, col(0), col(1)), jnp.where(lo, col(2), col(3))], axis=0)


def _store_pairs(o_ref, out, tq):
    o_ref[:, :LANES] = out[:tq]
    o_ref[:, LANES:] = out[tq:]


def _add_tile_mask(s, mask, tq):
    nk = s.shape[1]
    return (s.reshape(NSA_GQA, tq, nk) + mask).reshape(NSA_GQA * tq, nk)


def _cmpwin_kernel(q_ref, kc_ref, vc_ref, kw_ref, vw_ref, g_ref, ov_ref, wm_ref, o_ref, bias_ref,
                   *, tq, nk, n_cmp, n_blk, n_sel):
    i = pl.program_id(2)
    lo = _lane_lo(tq)
    lo2 = _lane_lo(2 * tq)
    k0 = pl.multiple_of(jnp.maximum(i * tq + tq - nk, 0), tq)
    s = _add_tile_mask(_dot_nt(_stack_q(q_ref), kw_ref[pl.ds(k0, nk), :]), wm_ref[...], tq)
    m = jnp.max(s, axis=-1, keepdims=True)
    p = jnp.exp2(s - m).astype(BF16)
    num, den = _pair_num_den(_dot(p[:2 * tq], vw_ref[pl.ds(k0, nk), :LANES]),
                             _dot(p[2 * tq:], vw_ref[pl.ds(k0, nk), LANES:]), lo2)
    out_w = num / den * _gate_pairs(g_ref, 2, tq)

    kc = kc_ref[...]
    nc = kc.shape[0]
    t_row = i * tq + lax.broadcasted_iota(jnp.int32, (1, tq), 1)
    c_col = lax.broadcasted_iota(jnp.int32, (nc, 1), 0)
    cend_col = jnp.where(c_col < n_cmp, c_col * CMP_STRIDE + (CMP_BLOCK - 1), jnp.int32(2 ** 30))
    mask_t = cend_col <= t_row
    psum = jnp.zeros((nc, tq), F32)
    xs = []
    for r in range(NSA_GQA):
        st = jnp.where(mask_t, _dot_nt(kc, q_ref[:, r * LANES:(r + 1) * LANES]), NEG)
        mt = jnp.max(st, axis=0, keepdims=True)
        pt = jnp.where(mask_t, jnp.exp2(st - mt), 0.0)
        lt = jnp.sum(pt, axis=0, keepdims=True)
        pt = pt * jnp.where(lt > 0.0, 1.0 / lt, 0.0)
        psum = psum + pt
        vsl = slice(0, LANES) if r % 2 == 0 else slice(LANES, 2 * LANES)
        xs.append(_dot(pt.T.astype(BF16), vc_ref[:, vsl]))
    out_c = jnp.concatenate([jnp.where(lo, xs[0], xs[1]), jnp.where(lo, xs[2], xs[3])], axis=0)
    _store_pairs(o_ref, out_c * _gate_pairs(g_ref, 0, tq) + out_w, tq)

    ps_hi = psum.astype(BF16)
    ps_lo = (psum - ps_hi.astype(F32)).astype(BF16)
    ov = ov_ref[...]
    imp = _dot(ov, ps_hi) + _dot(ov, ps_lo)

    cur = t_row // SEL_BLOCK
    ngrp = n_blk // SUBLANES
    jsub = lax.broadcasted_iota(jnp.int32, (SUBLANES, tq), 0)
    vals = []
    for a in range(ngrp):
        j = jsub + a * SUBLANES
        forced = (j == 0) | (j == cur) | (j == cur - 1)
        v = jnp.where(j <= cur, imp[a * SUBLANES:(a + 1) * SUBLANES], NEG)
        vals.append(jnp.where(forced, SEL_FORCE, v))
    ranks = [jnp.zeros((SUBLANES, tq), F32) for _ in range(ngrp)]
    for jj in range(n_blk):
        a0, s0 = divmod(jj, SUBLANES)
        rowb = jnp.broadcast_to(vals[a0][s0:s0 + 1, :], (SUBLANES, tq))
        for a in range(ngrp):
            if a > a0:
                beats = jnp.where(rowb >= vals[a], 1.0, 0.0)
            elif a < a0:
                beats = jnp.where(rowb > vals[a], 1.0, 0.0)
            else:
                beats = jnp.where(jsub > s0, jnp.where(rowb >= vals[a], 1.0, 0.0),
                                  jnp.where(rowb > vals[a], 1.0, 0.0))
            ranks[a] = ranks[a] + beats
    parts = [jnp.zeros((HEAD_DIM, tq), F32)]
    for a in range(ngrp):
        live = jnp.where(vals[a] > NEG / 2, 0.0, NEG)
        parts.append(jnp.where(ranks[a] < n_sel, live, NEG))
    if n_blk < HEAD_DIM:
        parts.append(jnp.zeros((HEAD_DIM - n_blk, tq), F32))
    bias_t = jnp.concatenate(parts, axis=0)
    bias_ref[...] = bias_t.T.astype(BF16)


def _band_masks(n_var, tq, nk, k0_of, lo_dist, hi_dist):
    r = np.arange(tq)[None, :, None]
    c = np.arange(nk)[None, None, :]
    v = np.arange(n_var)[:, None, None]
    dist = v * tq + r - (np.asarray([k0_of(x) for x in range(n_var)])[:, None, None] + c)
    return jnp.asarray(np.where((dist >= lo_dist) & (dist <= hi_dist), 0.0, NEG), F32)


def _cmpwin(qa, kc, vc, kw, vw, gates, ov, n_cmp, n_sel):
    b, s, _ = qa.shape
    tq = CMP_Q_TILE
    nk = min(WINDOW + tq, s)
    nc = kc.shape[2]
    n_blk = s // SEL_BLOCK
    gw = NSA_GQA * LANES
    n_early = (nk - tq) // tq
    wmask = _band_masks(n_early + 1, tq, nk, lambda v: max(v * tq + tq - nk, 0), 0, WINDOW - 1)
    kern = functools.partial(_cmpwin_kernel, tq=tq, nk=nk, n_cmp=n_cmp, n_blk=n_blk, n_sel=n_sel)
    kc_spec = pl.BlockSpec((None, None, nc, LANES), lambda b_, g, i: (b_, g, 0, 0))
    vc_spec = pl.BlockSpec((None, None, nc, 2 * LANES), lambda b_, g, i: (b_, g, 0, 0))
    return pl.pallas_call(
        kern,
        grid=(b, NSA_KV_HEADS, s // tq),
        in_specs=[pl.BlockSpec((None, tq, gw), lambda b_, g, i: (b_, i, g)), kc_spec, vc_spec,
                  pl.BlockSpec((None, s, LANES), lambda b_, g, i: (b_, 0, g)),
                  pl.BlockSpec((None, s, 2 * LANES), lambda b_, g, i: (b_, 0, g)),
                  pl.BlockSpec((None, tq, LANES), lambda b_, g, i: (b_, i, g)),
                  _const_spec(ov.shape),
                  pl.BlockSpec((None, tq, nk), lambda b_, g, i: (jnp.minimum(i, n_early), 0, 0))],
        out_specs=[pl.BlockSpec((None, tq, 2 * LANES), lambda b_, g, i: (b_, i, g)),
                   pl.BlockSpec((None, None, tq, LANES), lambda b_, g, i: (b_, g, i, 0))],
        out_shape=[jax.ShapeDtypeStruct((b, s, NSA_KV_HEADS * 2 * LANES), F32),
                   jax.ShapeDtypeStruct((b, NSA_KV_HEADS, s, LANES), BF16)],
        compiler_params=_cparams(3), name="cmp_win_select",
    )(qa, kc, vc, kw, vw, gates, ov, wmask)


def _sel_kernel(q_ref, bias_ref, k_ref, v_ref, g_ref, dm_ref, o_ref, m_ref, acc_ref, *, tq, tk):
    i = pl.program_id(2)
    rows = NSA_GQA * tq
    half = 2 * tq
    q4 = _stack_q(q_ref, bias_ref[...])
    m_ref[...] = jnp.full((rows, LANES), NEG, F32)
    acc_ref[...] = jnp.zeros((rows, LANES), F32)
    rep = tk // LANES

    def step(kt, masked):
        k0 = pl.multiple_of(kt * tk, tk)
        s = _dot_nt(q4, k_ref[pl.ds(k0, tk), :])
        if masked:
            s = _add_tile_mask(s, dm_ref[...], tq)
        m_prev = m_ref[...]
        m_next = jnp.maximum(m_prev, jnp.max(s, axis=-1, keepdims=True))
        p = jnp.exp2(s - jnp.tile(m_next, (1, rep))).astype(BF16)
        alpha = jnp.exp2(m_prev - m_next)
        m_ref[...] = m_next
        acc_ref[:half] = acc_ref[:half] * alpha[:half] + _dot(p[:half], v_ref[pl.ds(k0, tk), :LANES])
        acc_ref[half:] = acc_ref[half:] * alpha[half:] + _dot(p[half:], v_ref[pl.ds(k0, tk), LANES:])

    kt_diag = (i * tq) // tk
    lax.fori_loop(0, kt_diag, lambda kt, c: (step(kt, False), c)[1], 0)
    step(kt_diag, True)
    num, den = _pair_num_den(acc_ref[:half], acc_ref[half:], _lane_lo(half))
    _store_pairs(o_ref, num / den * _gate_pairs(g_ref, 1, tq), tq)


def _sel_attention(qa, bias, ks, vs, gates):
    b, s, _ = qa.shape
    tq, tk = Q_TILE, min(SEL_KEY_TILE, s)
    gw = NSA_GQA * LANES
    n_var = tk // tq
    dmask = _band_masks(n_var, tq, tk, lambda v: 0, 0, tk)
    return pl.pallas_call(
        functools.partial(_sel_kernel, tq=tq, tk=tk),
        grid=(b, NSA_KV_HEADS, s // tq),
        in_specs=[pl.BlockSpec((None, tq, gw), lambda b_, g, i: (b_, i, g)),
                  pl.BlockSpec((None, None, tq, LANES), lambda b_, g, i: (b_, g, i, 0)),
                  pl.BlockSpec((None, s, LANES), lambda b_, g, i: (b_, 0, g)),
                  pl.BlockSpec((None, s, 2 * LANES), lambda b_, g, i: (b_, 0, g)),
                  pl.BlockSpec((None, tq, LANES), lambda b_, g, i: (b_, i, g)),
                  pl.BlockSpec((None, tq, tk), lambda b_, g, i: (i % n_var, 0, 0))],
        out_specs=pl.BlockSpec((None, tq, 2 * LANES), lambda b_, g, i: (b_, i, g)),
        out_shape=jax.ShapeDtypeStruct((b, s, NSA_KV_HEADS * 2 * LANES), F32),
        scratch_shapes=[pltpu.VMEM((NSA_GQA * tq, LANES), F32), pltpu.VMEM((NSA_GQA * tq, LANES), F32)],
        compiler_params=_cparams(3), name="sel_attention",
    )(qa, bias, ks, vs, gates, dmask)


def _mem_kv_kernel(mem_ref, gain_ref, w_ref, gt_ref, k_ref, v_ref):
    nm = mem_ref.shape[0]
    lo = _lane_lo(nm)
    hn = _rms_rows(mem_ref[...], gain_ref[...]).astype(BF16)
    y = _dot(hn, w_ref[...])
    for h in range(MEM_HEADS):
        z = _slot_norm(y[:, h * LANES:(h + 1) * LANES], gt_ref[...], None)
        k_ref[:, h * LANES:(h + 1) * LANES] = jnp.where(lo, z, 0.0).astype(BF16)
        yv = y[:, (MEM_HEADS + h) * LANES:(MEM_HEADS + h + 1) * LANES]
        v_ref[:, h * LANES:(h + 1) * LANES] = _value_slot(yv, lo, h).astype(BF16)


def _mem_kv(mem, gains, w, gt):
    b, nm, _ = mem.shape
    nl = gains.shape[0]
    width = MEM_HEADS * LANES
    out_spec = pl.BlockSpec((None, None, nm, width), lambda l, b_: (l, b_, 0, 0))
    return pl.pallas_call(
        _mem_kv_kernel,
        grid=(nl, b),
        in_specs=[pl.BlockSpec((None, nm, D_MODEL), lambda l, b_: (b_, 0, 0)),
                  pl.BlockSpec((None, 1, D_MODEL), lambda l, b_: (l, 0, 0)),
                  pl.BlockSpec((None, D_MODEL, 2 * width), lambda l, b_: (l, 0, 0)),
                  pl.BlockSpec((None, 1, LANES), lambda l, b_: (l, 0, 0))],
        out_specs=[out_spec, out_spec],
        out_shape=[jax.ShapeDtypeStruct((nl, b, nm, width), BF16)] * 2,
        compiler_params=_cparams(2), name="mem_kv",
    )(mem, gains, w, gt)


def _mem_attn_kernel(q_ref, k_ref, v_ref, o_ref, *, tq):
    lo = _lane_lo(tq)
    accs = []
    for h in range(MEM_HEADS):
        sl = slice(h * LANES, (h + 1) * LANES)
        s = _dot_nt(q_ref[:, sl], k_ref[:, sl])
        m = jnp.max(s, axis=-1, keepdims=True)
        accs.append(_dot(jnp.exp2(s - m).astype(BF16), v_ref[:, sl]))
    for pair in range(MEM_HEADS // 2):
        num, den = _pair_num_den(accs[2 * pair], accs[2 * pair + 1], lo)
        o_ref[:, pair * LANES:(pair + 1) * LANES] = num / den


def _mem_attention(qm, mk, mv, layer):
    b, s, width = qm.shape
    nm = mk.shape[2]
    tq = min(MEM_Q_TILE, s)
    kv_spec = pl.BlockSpec((None, None, nm, width), lambda b_, i: (layer, b_, 0, 0))
    return pl.pallas_call(
        functools.partial(_mem_attn_kernel, tq=tq),
        grid=(b, s // tq),
        in_specs=[pl.BlockSpec((None, tq, width), lambda b_, i: (b_, i, 0)), kv_spec, kv_spec],
        out_specs=pl.BlockSpec((None, tq, MEM_W), lambda b_, i: (b_, i, 0)),
        out_shape=jax.ShapeDtypeStruct((b, s, MEM_W), F32),
        compiler_params=_cparams(2), name="mem_attention",
    )(qm, mk, mv)


def _mlp_tail(x1, gain_ref, wup_ref, wdn_ref, out_ref):
    hn = _rms_rows(x1, gain_ref[...]).astype(BF16)
    out_ref[...] = x1
    for c in range(D_FF // FF_CHUNK):
        u = _dot(hn, wup_ref[:, c * FF_CHUNK:(c + 1) * FF_CHUNK])
        u = jnp.square(jnp.maximum(u, 0.0)).astype(BF16)
        out_ref[...] += _dot(u, wdn_ref[c * FF_CHUNK:(c + 1) * FF_CHUNK, :])


def _a_out_kernel(ocw_ref, os_ref, om_ref, x_ref, wo_ref, gain_ref, wup_ref, wdn_ref, out_ref):
    o_main = (ocw_ref[...] + os_ref[...]).astype(BF16)
    x1 = x_ref[...] + _dot(o_main, wo_ref[:A_Q, :]) + _dot(om_ref[...].astype(BF16), wo_ref[A_Q:, :])
    _mlp_tail(x1, gain_ref, wup_ref, wdn_ref, out_ref)


def _b_out_kernel(od_ref, om_ref, x_ref, wo_ref, gain_ref, wup_ref, wdn_ref, out_ref):
    kd = DIL_SLOTS * HEAD_DIM
    x1 = (x_ref[...] + _dot(od_ref[...].astype(BF16), wo_ref[:kd, :])
          + _dot(om_ref[...].astype(BF16), wo_ref[kd:, :]))
    _mlp_tail(x1, gain_ref, wup_ref, wdn_ref, out_ref)


def _out_mlp(kern, acts, x2d, wo, gain, wup, wdn, name):
    n = x2d.shape[0]
    tm = ROW_TILE
    row_spec = lambda width: pl.BlockSpec((tm, width), lambda i: (i, 0))
    return pl.pallas_call(
        kern,
        grid=(n // tm,),
        in_specs=[row_spec(a.shape[1]) for a in acts] + [row_spec(D_MODEL), _const_spec(wo.shape),
                  _const_spec((1, D_MODEL)), _const_spec(wup.shape), _const_spec(wdn.shape)],
        out_specs=row_spec(D_MODEL),
        out_shape=jax.ShapeDtypeStruct((n, D_MODEL), F32),
        compiler_params=_cparams(1), name=name,
    )(*acts, x2d, wo, gain, wup, wdn)


B_QCOLS = N_DIL_GROUPS * DIL_SLOTS * LANES
B_COLS = B_QCOLS + MEM_HEADS * LANES
KV_COLS = 2 * DIL_SLOTS * LANES


def _b_proj_kernel(x_ref, gains_ref, wq_ref, wkv_ref, tab_ref, gt_ref,
                   q0_ref, q1_ref, q2_ref, qm_ref, k_ref, v_ref, *, tm):
    x = x_ref[...]
    xn = x * lax.rsqrt(jnp.mean(x * x, axis=-1, keepdims=True) + EPS)
    tab = tab_ref[...]
    lo = _lane_lo(tm)
    hq = (xn * gains_ref[0:1, :]).astype(BF16)
    width = DIL_SLOTS * LANES
    for gi, ref in enumerate((q0_ref, q1_ref, q2_ref)):
        y = _dot(hq, wq_ref[:, gi * width:(gi + 1) * width])
        for h in range(DIL_SLOTS):
            z = _slot_norm(y[:, h * LANES:(h + 1) * LANES], gt_ref[gi:gi + 1, :], tab)
            ref[:, h * LANES:(h + 1) * LANES] = jnp.where(lo, z * Q_SCALE, 0.0).astype(BF16)
    y = _dot(hq, wq_ref[:, B_QCOLS:B_COLS])
    for h in range(MEM_HEADS):
        z = _slot_norm(y[:, h * LANES:(h + 1) * LANES], gt_ref[3:4, :], None)
        qm_ref[:, h * LANES:(h + 1) * LANES] = jnp.where(lo, z * Q_SCALE, 0.0).astype(BF16)
    hk = (xn * gains_ref[1:2, :]).astype(BF16)
    y = _dot(hk, wkv_ref[...])
    for h in range(DIL_SLOTS):
        z = _slot_norm(y[:, h * LANES:(h + 1) * LANES], gt_ref[4:5, :], tab)
        k_ref[:, h * LANES:(h + 1) * LANES] = jnp.where(lo, z, 0.0).astype(BF16)
        yv = y[:, (DIL_SLOTS + h) * LANES:(DIL_SLOTS + h + 1) * LANES]
        v_ref[:, h * LANES:(h + 1) * LANES] = _value_slot(yv, lo, h).astype(BF16)


def _b_proj(x2d, gains, wq, wkv, tab, gt, seq):
    n = x2d.shape[0]
    tm = ROW_TILE
    nseq = seq // tm
    row_spec = lambda width: pl.BlockSpec((tm, width), lambda i: (i, 0))
    width = DIL_SLOTS * LANES
    widths = (width, width, width, MEM_HEADS * LANES, width, width)
    return pl.pallas_call(
        functools.partial(_b_proj_kernel, tm=tm),
        grid=(n // tm,),
        in_specs=[row_spec(D_MODEL), _const_spec(gains.shape), _const_spec(wq.shape),
                  _const_spec(wkv.shape), pl.BlockSpec((tm, LANES), lambda i: (i % nseq, 0)),
                  _const_spec(gt.shape)],
        out_specs=[row_spec(wd) for wd in widths],
        out_shape=[jax.ShapeDtypeStruct((n, wd), BF16) for wd in widths],
        compiler_params=_cparams(1), name="b_proj",
    )(x2d, gains, wq, wkv, tab, gt)


def _dil_kernel(q0_ref, q1_ref, q2_ref, k_ref, v_ref, o_ref, qf_ref, kf_ref, vf_ref, lse_ref, *, seq):
    tq = DIL_Q_TILE
    heads = (slice(0, LANES), slice(LANES, 2 * LANES))
    for hd, sl in enumerate(heads):
        kf_ref[hd] = k_ref[:, sl].astype(F32)
        vf_ref[hd] = v_ref[:, sl].astype(F32)

    def attend(qs, ks, vs, qpos0, kpos0):
        nq, nk = qs[0].shape[0], ks[0].shape[0]
        lo = _lane_lo(nq)
        dist = (qpos0 + lax.broadcasted_iota(jnp.int32, (nq, 1), 0)
                - (kpos0 + lax.broadcasted_iota(jnp.int32, (1, nk), 1)))
        mask = (dist >= 0) & (dist <= DIL_BAND)
        accs, ms = [], []
        for q, k, v in zip(qs, ks, vs):
            s = jnp.where(mask, _dot_nt(q, k), NEG)
            m = jnp.max(s, axis=-1, keepdims=True)
            accs.append(_dot(jnp.exp2(s - m).astype(BF16), v))
            ms.append(jnp.broadcast_to(m, (nq, LANES)))
        num, den = _pair_num_den(accs[0], accs[1], lo)
        return num / den, jnp.where(lo, ms[0], ms[1]) + jnp.log2(den)

    def merge(o_old, l_old, o_new, l_new):
        mx = jnp.maximum(l_old, l_new)
        a = jnp.exp2(l_old - mx)
        b = jnp.exp2(l_new - mx)
        den = a + b
        return (a * o_old + b * o_new) / den, mx + jnp.log2(den)

    nk = min(tq + DIL_BAND, seq)

    def body0(i, c):
        q0 = pl.multiple_of(i * tq, tq)
        k0 = pl.multiple_of(jnp.clip(i * tq - DIL_BAND, 0, seq - nk), DIL_BAND)
        o, l = attend([q0_ref[pl.ds(q0, tq), sl] for sl in heads],
                      [k_ref[pl.ds(k0, nk), sl] for sl in heads],
                      [v_ref[pl.ds(k0, nk), sl] for sl in heads], q0, k0)
        o_ref[pl.ds(q0, tq), :] = o
        lse_ref[pl.ds(q0, tq), :] = l
        return c

    lax.fori_loop(0, seq // tq, body0, 0, unroll=DIL_UNROLL)

    for q_ref, (_, dil) in zip((q1_ref, q2_ref), DIL_PATTERNS[1:]):
        for hd, sl in enumerate(heads):
            qf_ref[hd] = q_ref[:, sl].astype(F32)
        length = seq // dil
        tqd = min(tq, length)
        nkd = min(tqd + DIL_BAND, length)
        ntile = length // tqd

        def body(it, c, dil=dil, length=length, tqd=tqd, nkd=nkd, ntile=ntile):
            r = it // ntile
            i = it - r * ntile
            qp = i * tqd
            kp = jnp.clip(qp - DIL_BAND, 0, length - nkd)
            qrows = pl.ds(r + dil * qp, tqd, stride=dil)
            krows = pl.ds(r + dil * kp, nkd, stride=dil)
            o, l = attend([qf_ref[hd, qrows, :].astype(BF16) for hd in range(2)],
                          [kf_ref[hd, krows, :].astype(BF16) for hd in range(2)],
                          [vf_ref[hd, krows, :].astype(BF16) for hd in range(2)], qp, kp)
            o, l = merge(o_ref[qrows, :], lse_ref[qrows, :], o, l)
            o_ref[qrows, :] = o
            lse_ref[qrows, :] = l
            return c

        lax.fori_loop(0, dil * ntile, body, 0, unroll=DIL_UNROLL)


def _dil_attention(q0, q1, q2, k, v):
    b, s, _ = k.shape
    npair = DIL_SLOTS // 2
    pw = 2 * LANES
    in_spec = pl.BlockSpec((None, s, pw), lambda b_, hp: (b_, 0, hp))
    return pl.pallas_call(
        functools.partial(_dil_kernel, seq=s),
        grid=(b, npair),
        in_specs=[in_spec] * 5,
        out_specs=pl.BlockSpec((None, s, LANES), lambda b_, hp: (b_, 0, hp)),
        out_shape=jax.ShapeDtypeStruct((b, s, DIL_SLOTS * HEAD_DIM), F32),
        scratch_shapes=[pltpu.VMEM((2, s, LANES), F32)] * 3 + [pltpu.VMEM((s, LANES), F32)],
        compiler_params=_cparams(2), name="dil_attention",
    )(q0, q1, q2, k, v)


def kernel(x, mem, attn_norm, mlp_norm, w_up, w_down, mem_norm, w_mem_kv, mem_q_norm, mem_k_norm,
           a_w_in, a_w_out, a_q_norm, a_k_norm, a_cmp_pos, a_cmp_w1, a_cmp_b1, a_cmp_w2, a_cmp_b2,
           kv_norm, w_kv_shared, kv_k_norm, b_w_in, b_w_out, b_q_norm):
    b, s, _ = x.shape
    n = b * s
    assert s % ROW_TILE == 0 and s % SEL_KEY_TILE == 0 and s // SEL_BLOCK <= HEAD_DIM
    assert s >= WINDOW + CMP_Q_TILE and(s // DIL_PATTERNS[-1][1]) % DIL_BAND == 0
    x2d = x.reshape(n, D_MODEL)
    tab = _rope_table(jnp.arange(s))

    wk, wv = w_mem_kv[..., :MEM_W], w_mem_kv[..., MEM_W:]
    w_mem = jnp.concatenate([jax.vmap(lambda w: _slots(w, MEM_HEADS, "dup"))(wk),
                             jax.vmap(lambda w: _slots(w, MEM_HEADS, "dup"))(wv)], axis=-1).astype(BF16)
    mk, mv = _mem_kv(mem, mem_norm[:, None, :], w_mem, _gain_slot(mem_k_norm, "dup")[:, None, :])

    gt_a = jnp.zeros((SUBLANES, LANES), F32)
    gt_a = gt_a.at[0].set(_gain_slot(a_q_norm[0], "rot")).at[1].set(_gain_slot(a_k_norm[0, 1], "rot"))
    gt_a = gt_a.at[2].set(_gain_slot(a_k_norm[0, 2], "rot")).at[3].set(_gain_slot(mem_q_norm[0], "dup"))
    qa, ks, kw, vs, vw, kcvc, qm, gates = _a_proj(
        x2d, attn_norm[0:1], _prep_a_w_in(a_w_in[0]), tab, gt_a, s)

    n_cmp = (s - CMP_BLOCK) // CMP_STRIDE + 1
    nc = s // CMP_STRIDE

    def half_blocks(t):
        t = t.reshape(b, s, NSA_KV_HEADS, HEAD_DIM).transpose(0, 2, 1, 3)
        return t.reshape(b * NSA_KV_HEADS, nc, CMP_STRIDE * HEAD_DIM)

    w2 = jnp.stack([_slots(a_cmp_w2[0, 0], 1, "rot"), _slots(a_cmp_w2[0, 1], 1, "dup")]).astype(BF16)
    b2 = jnp.stack([_slots(a_cmp_b2[0, 0][None], 1, "rot")[0], _slots(a_cmp_b2[0, 1][None], 1, "dup")[0]])
    tab_c = _rope_table(jnp.arange(nc) * CMP_STRIDE + (CMP_BLOCK - 1))
    kc, vc = _compress(
        half_blocks(kcvc[:, :A_KV]), half_blocks(kcvc[:, A_KV:2 * A_KV]),
        a_cmp_pos[0].reshape(2, 2, CMP_STRIDE * HEAD_DIM),
        a_cmp_w1[0].reshape(2, 2, CMP_STRIDE * HEAD_DIM, CMP_HIDDEN).astype(BF16), a_cmp_b1[0],
        w2, b2, tab_c, _gain_slot(a_k_norm[0, 0], "rot")[None])
    kc = kc.reshape(b, NSA_KV_HEADS, nc, LANES)
    vc = vc.reshape(b, NSA_KV_HEADS, nc, 2 * LANES)

    n_blk = s // SEL_BLOCK
    c_start = np.arange(nc)[None, :] * CMP_STRIDE
    b_start = np.arange(n_blk)[:, None] * SEL_BLOCK
    overlap_t = ((c_start < b_start + SEL_BLOCK) & (c_start + CMP_BLOCK > b_start)
                 & (np.arange(nc)[None, :] < n_cmp))
    ov = jnp.asarray(overlap_t, BF16)

    qa3 = qa.reshape(b, s, -1)
    gates3 = gates.reshape(b, s, -1)
    o_cw, bias = _cmpwin(qa3, kc, vc, kw.reshape(b, s, -1), vw.reshape(b, s, -1), gates3, ov,
                         n_cmp, min(SEL_TOPK, n_blk))
    o_s = _sel_attention(qa3, bias, ks.reshape(b, s, -1), vs.reshape(b, s, -1), gates3)
    o_m = _mem_attention(qm.reshape(b, s, -1), mk, mv, 0)
    x2d = _out_mlp(_a_out_kernel,
                   [o_cw.reshape(n, -1), o_s.reshape(n, -1), o_m.reshape(n, -1)],
                   x2d, a_w_out[0].astype(BF16), mlp_norm[0:1], w_up[0].astype(BF16),
                   w_down[0].astype(BF16), "a_out_mlp")

    wq = jnp.concatenate([_slots(b_w_in[0][:, :B_Q], N_DIL_GROUPS * DIL_SLOTS, "rot"),
                          _slots(b_w_in[0][:, B_Q:], MEM_HEADS, "dup")], axis=1).astype(BF16)
    kd = DIL_SLOTS * HEAD_DIM
    wkv = jnp.concatenate([_slots(w_kv_shared[:, :kd], DIL_SLOTS, "rot"),
                           _slots(w_kv_shared[:, kd:], DIL_SLOTS, "dup")], axis=1).astype(BF16)
    gt_b = jnp.zeros((SUBLANES, LANES), F32)
    for gi in range(N_DIL_GROUPS):
        gt_b = gt_b.at[gi].set(_gain_slot(b_q_norm[0, gi], "rot"))
    gt_b = gt_b.at[3].set(_gain_slot(mem_q_norm[1], "dup")).at[4].set(_gain_slot(kv_k_norm, "rot"))
    q0, q1, q2, qm1, kb, vb = _b_proj(x2d, jnp.stack([attn_norm[1], kv_norm]), wq, wkv, tab, gt_b, s)
    o_d = _dil_attention(*(t.reshape(b, s, -1) for t in (q0, q1, q2, kb, vb)))
    o_m = _mem_attention(qm1.reshape(b, s, -1), mk, mv, 1)
    x2d = _out_mlp(_b_out_kernel, [o_d.reshape(n, -1), o_m.reshape(n, -1)], x2d, b_w_out[0].astype(BF16),
                   mlp_norm[1:2], w_up[1].astype(BF16), w_down[1].astype(BF16), "b_out_mlp")
    return x2d.reshape(b, s, D_MODEL)
```

```python
import functools

import numpy as np
import jax
import jax.numpy as jnp
from jax import lax
from jax.experimental import pallas as pl
from jax.experimental.pallas import tpu as pltpu

D_MODEL = 1024
HEAD_DIM = 64
HALF = HEAD_DIM // 2
ROPE_THETA = 10000.0
EPS = 1e-6
NEG = -1e30
D_FF = 4 * D_MODEL
MEM_HEADS = 4
NSA_HEADS = 12
NSA_KV_HEADS = 3
NSA_GQA = NSA_HEADS // NSA_KV_HEADS
CMP_BLOCK = 32
CMP_STRIDE = 16
CMP_HIDDEN = 256
SEL_BLOCK = 64
SEL_TOPK = 16
WINDOW = 512
SEL_FORCE = 1e9
DIL_PATTERNS = ((128, 1), (512, 4), (2048, 16))
N_DIL_GROUPS = 3
DIL_SLOTS = 8
MEM_W = MEM_HEADS * HEAD_DIM
A_Q = NSA_HEADS * HEAD_DIM
A_KV = NSA_KV_HEADS * HEAD_DIM
B_Q = N_DIL_GROUPS * DIL_SLOTS * HEAD_DIM
LOG2E = 1.4426950408889634
Q_SCALE = HEAD_DIM ** -0.5 * LOG2E

LANES = 128
SUBLANES = 8
VMEM_LIMIT = 56 * 1024 * 1024
ROW_TILE = 512
Q_TILE = 256
CMP_Q_TILE = 256
DIL_UNROLL = 4
SEL_KEY_TILE = 512
MEM_Q_TILE = 512
DIL_Q_TILE = 128
DIL_BAND = 128
FF_CHUNK = 1024

BF16 = jnp.bfloat16
F32 = jnp.float32


def _cparams(n_grid):
    return pltpu.CompilerParams(dimension_semantics=("arbitrary",) * n_grid,
                                vmem_limit_bytes=VMEM_LIMIT)


def _const_spec(shape):
    nd = len(shape)
    return pl.BlockSpec(shape, lambda *_: (0,) * nd, pipeline_mode=pl.Buffered(1))


def _dot(a, b):
    return jnp.dot(a, b, preferred_element_type=F32)


def _dot_nt(a, b):
    return lax.dot_general(a, b, (((1,), (1,)), ((), ())), preferred_element_type=F32)


def _rms_rows(x, gain):
    return x * lax.rsqrt(jnp.mean(x * x, axis=-1, keepdims=True) + EPS) * gain


def _slot_norm(y, gt, tab):
    z = y * lax.rsqrt(jnp.mean(y * y, axis=-1, keepdims=True) + EPS) * gt
    if tab is not None:
        z = z * tab
        z = z + pltpu.roll(z, HEAD_DIM, 1)
    return z


def _head_mean_sq(y, ones_ref):
    sq = (y * y).astype(BF16)
    width = y.shape[1]
    step = ones_ref.shape[0]
    parts = []
    for c in range(0, width, step):
        cw = min(step, width - c)
        parts.append(_dot(sq[:, c:c + cw], ones_ref[:cw, :cw]))
    return jnp.concatenate(parts, axis=1) * (1.0 / HEAD_DIM)


def _pair_norm(y, ms, gain, rope):
    z = y * lax.rsqrt(ms + EPS) * gain
    if rope is not None:
        cos, sin_signed, first = rope
        partner = jnp.where(first, pltpu.roll(z, LANES - HALF, 1), pltpu.roll(z, HALF, 1))
        z = z * cos + partner * sin_signed
    return z


def _lane_lo(rows):
    return lax.broadcasted_iota(jnp.int32, (rows, LANES), 1) < HEAD_DIM


def _swap_halves(x):
    return pltpu.roll(x, HEAD_DIM, 1)


def _pair_num_den(acc_even, acc_odd, lo):
    return jnp.where(lo, acc_even, acc_odd), _swap_halves(jnp.where(lo, acc_odd, acc_even))


def _rot_half_cols(w):
    return jnp.concatenate([-w[..., HALF:], w[..., :HALF]], axis=-1)


def _swap_half(g):
    return jnp.concatenate([g[..., HALF:], g[..., :HALF]], axis=-1)


def _slots(w, n, kind):
    k = w.shape[0]
    w = w.reshape(k, n, HEAD_DIM)
    other = _rot_half_cols(w) if kind == "rot" else w
    return jnp.concatenate([w, other], axis=-1).reshape(k, n * LANES)


def _gain_slot(g, kind):
    other = _swap_half(g) if kind == "rot" else g
    return jnp.concatenate([g, other], axis=-1)


def _gain_pair(g):
    return jnp.concatenate([g, g], axis=-1)


def _rope_cos_sin(pos):
    inv_freq = ROPE_THETA ** (-jnp.arange(HALF, dtype=F32) / HALF)
    ang = jnp.asarray(pos, F32)[:, None] * inv_freq[None, :]
    return jnp.cos(ang), jnp.sin(ang)


def _rope_table(pos):
    cos, sin = _rope_cos_sin(pos)
    return jnp.concatenate([cos, cos, sin, sin], axis=-1)


def _rope_pair_tables(pos):
    cos, sin = _rope_cos_sin(pos)
    return (jnp.concatenate([cos, cos, cos, cos], axis=-1),
            jnp.concatenate([-sin, sin, -sin, sin], axis=-1))


def _pad_cols(w, width):
    return jnp.pad(w, ((0, 0), (0, width - w.shape[1])))


KV_PAIRS = (NSA_KV_HEADS + 1) // 2
A_COL_Q = 0
A_COL_KS = A_COL_Q + A_Q
A_COL_KW = A_COL_KS + KV_PAIRS * LANES
A_COL_VS = A_COL_KW + KV_PAIRS * LANES
A_COL_VW = A_COL_VS + KV_PAIRS * LANES
A_COL_KCVC = A_COL_VW + KV_PAIRS * LANES
A_COL_QM = A_COL_KCVC + NSA_KV_HEADS * LANES
A_COL_G = A_COL_QM + MEM_W
A_COLS = A_COL_G + NSA_KV_HEADS * LANES


def _prep_a_w_in(w):
    o = 0
    q = w[:, o:o + A_Q]; o += A_Q
    kc = w[:, o:o + A_KV]; o += A_KV
    vc = w[:, o:o + A_KV]; o += A_KV
    ks = w[:, o:o + A_KV]; o += A_KV
    vs = w[:, o:o + A_KV]; o += A_KV
    kw = w[:, o:o + A_KV]; o += A_KV
    vw = w[:, o:o + A_KV]; o += A_KV
    qm = w[:, o:o + MEM_W]; o += MEM_W
    gl = w[:, o:]
    gl = gl.reshape(D_MODEL, NSA_KV_HEADS, NSA_GQA, 3).transpose(0, 1, 3, 2)
    gl = gl.reshape(D_MODEL, NSA_KV_HEADS, 3 * NSA_GQA)
    gl = jnp.pad(gl, ((0, 0), (0, 0), (0, LANES - 3 * NSA_GQA))).reshape(D_MODEL, NSA_KV_HEADS * LANES)
    kcvc = jnp.concatenate([kc.reshape(D_MODEL, NSA_KV_HEADS, HEAD_DIM),
                            vc.reshape(D_MODEL, NSA_KV_HEADS, HEAD_DIM)], axis=-1)
    kvw = KV_PAIRS * LANES
    cols = [q, _pad_cols(ks, kvw), _pad_cols(kw, kvw), _pad_cols(vs, kvw), _pad_cols(vw, kvw),
            kcvc.reshape(D_MODEL, NSA_KV_HEADS * LANES), qm, gl]
    return jnp.concatenate(cols, axis=1).astype(BF16)


def _store_head_slots(ref, z, n_heads, pair, aux):
    lo = _lane_lo(z.shape[0])
    for half, data in enumerate((z, _swap_halves(z))):
        h = 2 * pair + half
        if h < n_heads:
            ref[:, h * LANES:(h + 1) * LANES] = jnp.where(lo, data, aux).astype(ref.dtype)


def _store_value_slots(ref, y, n_heads, pair, both):
    lo = _lane_lo(y.shape[0])
    ys = _swap_halves(y)
    for half in range(2):
        h = 2 * pair + half
        if h >= n_heads:
            continue
        even = jnp.where(lo, y if half == 0 else ys, 1.0)
        odd = jnp.where(lo, 1.0, ys if half == 0 else y)
        if both:
            ref[:, 2 * h * LANES:(2 * h + 1) * LANES] = even.astype(ref.dtype)
            ref[:, (2 * h + 1) * LANES:(2 * h + 2) * LANES] = odd.astype(ref.dtype)
        else:
            ref[:, h * LANES:(h + 1) * LANES] = (even if h % 2 == 0 else odd).astype(ref.dtype)


def _rope_args(cos_ref, sin_ref, rows):
    lane = lax.broadcasted_iota(jnp.int32, (rows, LANES), 1)
    return cos_ref[...], sin_ref[...], (lane & HALF) == 0


def _a_proj_kernel(x_ref, gain_ref, w_ref, cos_ref, sin_ref, gt_ref, ones_ref,
                   qa_ref, ks_ref, kw_ref, vs_ref, vw_ref, kcvc_ref, qm_ref, g_ref, *, seq, tm):
    hn = _rms_rows(x_ref[...], gain_ref[...]).astype(BF16)
    rope = _rope_args(cos_ref, sin_ref, tm)
    lane = lax.broadcasted_iota(jnp.int32, (tm, LANES), 1)
    row = lax.broadcasted_iota(jnp.int32, (tm, LANES), 0)
    tok = (pl.program_id(0) % (seq // tm)) * tm + row
    blk_ind = jnp.where(lane - HEAD_DIM == tok // SEL_BLOCK, 1.0, 0.0)

    yq = _dot(hn, w_ref[:, A_COL_Q:A_COL_KS])
    ms = _head_mean_sq(yq, ones_ref)
    for p in range(NSA_HEADS // 2):
        sl = slice(p * LANES, (p + 1) * LANES)
        z = _pair_norm(yq[:, sl], ms[:, sl], gt_ref[0:1, :], rope) * Q_SCALE
        _store_head_slots(qa_ref, z, NSA_HEADS, p, 0.0)

    yk = _dot(hn, w_ref[:, A_COL_KS:A_COL_VS])
    ms = _head_mean_sq(yk, ones_ref)
    for p in range(KV_PAIRS):
        sl = slice(p * LANES, (p + 1) * LANES)
        z = _pair_norm(yk[:, sl], ms[:, sl], gt_ref[1:2, :], rope)
        _store_head_slots(ks_ref, z, NSA_KV_HEADS, p, blk_ind)
        sl = slice((KV_PAIRS + p) * LANES, (KV_PAIRS + p + 1) * LANES)
        z = _pair_norm(yk[:, sl], ms[:, sl], gt_ref[2:3, :], rope)
        _store_head_slots(kw_ref, z, NSA_KV_HEADS, p, 0.0)

    yv = _dot(hn, w_ref[:, A_COL_VS:A_COL_KCVC])
    for j, ref in enumerate((vs_ref, vw_ref)):
        for p in range(KV_PAIRS):
            c = (j * KV_PAIRS + p) * LANES
            _store_value_slots(ref, yv[:, c:c + LANES], NSA_KV_HEADS, p, True)

    yr = _dot(hn, w_ref[:, A_COL_KCVC:A_COLS])
    kcvc_ref[...] = yr[:, :NSA_KV_HEADS * LANES]
    c = A_COL_QM - A_COL_KCVC
    ym = yr[:, c:c + MEM_W]
    ms = _head_mean_sq(ym, ones_ref)
    for p in range(MEM_HEADS // 2):
        sl = slice(p * LANES, (p + 1) * LANES)
        z = _pair_norm(ym[:, sl], ms[:, sl], gt_ref[3:4, :], None) * Q_SCALE
        _store_head_slots(qm_ref, z, MEM_HEADS, p, 0.0)
    c = A_COL_G - A_COL_KCVC
    g_ref[...] = jax.nn.sigmoid(yr[:, c:c + NSA_KV_HEADS * LANES])


def _head_ones():
    idx = np.arange(2 * LANES) // HEAD_DIM
    return jnp.asarray(idx[:, None] == idx[None, :], BF16)


def _a_proj(x2d, gain, w, cos, sin, gt, seq):
    n = x2d.shape[0]
    tm = ROW_TILE
    nseq = seq // tm
    row_spec = lambda width: pl.BlockSpec((tm, width), lambda i: (i, 0))
    tab_spec = pl.BlockSpec((tm, LANES), lambda i: (i % nseq, 0))
    kvw = NSA_KV_HEADS * LANES
    widths = (NSA_HEADS * LANES, kvw, kvw, 2 * kvw, 2 * kvw, kvw, MEM_HEADS * LANES, kvw)
    dtypes = (BF16, BF16, BF16, BF16, BF16, F32, BF16, F32)
    return pl.pallas_call(
        functools.partial(_a_proj_kernel, seq=seq, tm=tm),
        grid=(n // tm,),
        in_specs=[row_spec(D_MODEL), _const_spec((1, D_MODEL)), _const_spec((D_MODEL, A_COLS)),
                  tab_spec, tab_spec, _const_spec((SUBLANES, LANES)), _const_spec((2 * LANES, 2 * LANES))],
        out_specs=[row_spec(wd) for wd in widths],
        out_shape=[jax.ShapeDtypeStruct((n, wd), dt) for wd, dt in zip(widths, dtypes)],
        compiler_params=_cparams(1), name="a_proj",
    )(x2d, gain, w, cos, sin, gt, _head_ones())


def _compress_kernel(x_ref, pos_ref, w1_ref, b1_ref, w2_ref, b2_ref, tab_ref, gt_ref, kc_ref, vc_ref):
    nc = kc_ref.shape[0]
    lo = _lane_lo(nc)
    h = jnp.concatenate([x_ref[pl.ds(j, nc, stride=CMP_STRIDE), :] for j in range(CMP_STRIDE)], axis=1)
    top = _dot((h + pos_ref[0:1, :]).astype(BF16), w1_ref[0])
    bot = _dot((h + pos_ref[1:2, :]).astype(BF16), w1_ref[1])
    pre = top + pltpu.roll(bot, nc - 1, 0) + b1_ref[...]
    hid = jax.nn.gelu(pre).astype(BF16)
    yk = _dot(hid[:, :CMP_HIDDEN], w2_ref[0]) + b2_ref[0:1, :]
    yv = _dot(hid[:, CMP_HIDDEN:], w2_ref[1]) + b2_ref[1:2, :]
    kc_ref[...] = jnp.where(lo, _slot_norm(yk, gt_ref[...], tab_ref[...]), 0.0).astype(BF16)
    vc_ref[:, :LANES] = jnp.where(lo, yv, 1.0).astype(BF16)
    vc_ref[:, LANES:] = jnp.where(lo, 1.0, yv).astype(BF16)


def _compress(kcvc, pos, w1, b1, w2, b2, tab_c, gt):
    b, s, _ = kcvc.shape
    nc = s // CMP_STRIDE
    ospec = lambda width: pl.BlockSpec((None, None, nc, width), lambda b_, g: (b_, g, 0, 0))
    return pl.pallas_call(
        _compress_kernel,
        grid=(b, NSA_KV_HEADS),
        in_specs=[pl.BlockSpec((None, s, LANES), lambda b_, g: (b_, 0, g)),
                  _const_spec(pos.shape), _const_spec(w1.shape), _const_spec(b1.shape),
                  _const_spec(w2.shape), _const_spec(b2.shape), _const_spec(tab_c.shape),
                  _const_spec(gt.shape)],
        out_specs=[ospec(LANES), ospec(2 * LANES)],
        out_shape=[jax.ShapeDtypeStruct((b, NSA_KV_HEADS, nc, LANES), BF16),
                   jax.ShapeDtypeStruct((b, NSA_KV_HEADS, nc, 2 * LANES), BF16)],
        compiler_params=_cparams(2), name="compress",
    )(kcvc, pos, w1, b1, w2, b2, tab_c, gt)


def _prep_compress(cmp_pos, cmp_w1, cmp_b1, cmp_w2, cmp_b2):
    p = cmp_pos.reshape(2, 2, CMP_STRIDE, HEAD_DIM).transpose(1, 2, 0, 3).reshape(2, CMP_STRIDE * LANES)
    w = cmp_w1.reshape(2, 2, CMP_STRIDE, HEAD_DIM, CMP_HIDDEN)
    z = jnp.zeros_like(w[0])
    wk = jnp.concatenate([w[0], z], axis=-1)
    wv = jnp.concatenate([z, w[1]], axis=-1)
    w1 = jnp.concatenate([wk, wv], axis=2).reshape(2, CMP_STRIDE * LANES, 2 * CMP_HIDDEN)
    b1 = jnp.concatenate([cmp_b1[0], cmp_b1[1]])[None]
    w2 = jnp.stack([_slots(cmp_w2[0], 1, "rot"), _slots(cmp_w2[1], 1, "dup")])
    b2 = jnp.stack([_slots(cmp_b2[0][None], 1, "rot")[0], _slots(cmp_b2[1][None], 1, "dup")[0]])
    return p, w1.astype(BF16), b1, w2.astype(BF16), b2


STACK_ORDER = (0, 2, 1, 3)


def _stack_q(q_ref, extra=None):
    parts = []
    for r in STACK_ORDER:
        q = q_ref[:, r * LANES:(r + 1) * LANES]
        parts.append(q if extra is None else q + extra)
    return jnp.concatenate(parts, axis=0)


def _gate_pairs(g_ref, branch, tq):
    lo = _lane_lo(tq)
    col = lambda r: jnp.broadcast_to(g_ref[:, branch * NSA_GQA + r:branch * NSA_GQA + r + 1], (tq, LANES))
    return jnp.concatenate([jnp.where(lo, col(0), col(1)), jnp.where(lo, col(2), col(3))], axis=0)


def _store_pairs(o_ref, out, tq):
    o_ref[:, :LANES] = out[:tq]
    o_ref[:, LANES:] = out[tq:]


def _add_tile_mask(s, mask, tq):
    nk = s.shape[1]
    return (s.reshape(NSA_GQA, tq, nk) + mask).reshape(NSA_GQA * tq, nk)


def _cmpwin_kernel(q_ref, kc_ref, vc_ref, kw_ref, vw_ref, g_ref, ov_ref, wm_ref, o_ref, bias_ref,
                   *, tq, nk, n_cmp, n_blk, n_sel):
    i = pl.program_id(2)
    lo = _lane_lo(tq)
    lo2 = _lane_lo(2 * tq)
    k0 = pl.multiple_of(jnp.maximum(i * tq + tq - nk, 0), tq)
    s = _add_tile_mask(_dot_nt(_stack_q(q_ref), kw_ref[pl.ds(k0, nk), :]), wm_ref[...], tq)
    m = jnp.max(s, axis=-1, keepdims=True)
    p = jnp.exp2(s - m).astype(BF16)
    num, den = _pair_num_den(_dot(p[:2 * tq], vw_ref[pl.ds(k0, nk), :LANES]),
                             _dot(p[2 * tq:], vw_ref[pl.ds(k0, nk), LANES:]), lo2)
    out_w = num / den * _gate_pairs(g_ref, 2, tq)

    kc = kc_ref[...]
    nc = kc.shape[0]
    t_row = i * tq + lax.broadcasted_iota(jnp.int32, (1, tq), 1)
    c_col = lax.broadcasted_iota(jnp.int32, (nc, 1), 0)
    cend_col = jnp.where(c_col < n_cmp, c_col * CMP_STRIDE + (CMP_BLOCK - 1), jnp.int32(2 ** 30))
    mask_t = cend_col <= t_row
    psum = jnp.zeros((nc, tq), F32)
    xs = []
    for r in range(NSA_GQA):
        st = jnp.where(mask_t, _dot_nt(kc, q_ref[:, r * LANES:(r + 1) * LANES]), NEG)
        mt = jnp.max(st, axis=0, keepdims=True)
        pt = jnp.where(mask_t, jnp.exp2(st - mt), 0.0)
        lt = jnp.sum(pt, axis=0, keepdims=True)
        pt = pt * jnp.where(lt > 0.0, 1.0 / lt, 0.0)
        psum = psum + pt
        vsl = slice(0, LANES) if r % 2 == 0 else slice(LANES, 2 * LANES)
        xs.append(_dot(pt.T.astype(BF16), vc_ref[:, vsl]))
    out_c = jnp.concatenate([jnp.where(lo, xs[0], xs[1]), jnp.where(lo, xs[2], xs[3])], axis=0)
    _store_pairs(o_ref, out_c * _gate_pairs(g_ref, 0, tq) + out_w, tq)

    ps_hi = psum.astype(BF16)
    ps_lo = (psum - ps_hi.astype(F32)).astype(BF16)
    ov = ov_ref[...]
    imp = _dot(ov, ps_hi) + _dot(ov, ps_lo)

    cur = t_row // SEL_BLOCK
    ngrp = n_blk // SUBLANES
    jsub = lax.broadcasted_iota(jnp.int32, (SUBLANES, tq), 0)
    vals = []
    for a in range(ngrp):
        j = jsub + a * SUBLANES
        forced = (j == 0) | (j == cur) | (j == cur - 1)
        v = jnp.where(j <= cur, imp[a * SUBLANES:(a + 1) * SUBLANES], NEG)
        vals.append(jnp.where(forced, SEL_FORCE, v))
    ranks = [jnp.zeros((SUBLANES, tq), F32) for _ in range(ngrp)]
    for jj in range(n_blk):
        a0, s0 = divmod(jj, SUBLANES)
        rowb = jnp.broadcast_to(vals[a0][s0:s0 + 1, :], (SUBLANES, tq))
        for a in range(ngrp):
            if a > a0:
                beats = jnp.where(rowb >= vals[a], 1.0, 0.0)
            elif a < a0:
                beats = jnp.where(rowb > vals[a], 1.0, 0.0)
            else:
                beats = jnp.where(jsub > s0, jnp.where(rowb >= vals[a], 1.0, 0.0),
                                  jnp.where(rowb > vals[a], 1.0, 0.0))
            ranks[a] = ranks[a] + beats
    parts = [jnp.zeros((HEAD_DIM, tq), F32)]
    for a in range(ngrp):
        live = jnp.where(vals[a] > NEG / 2, 0.0, NEG)
        parts.append(jnp.where(ranks[a] < n_sel, live, NEG))
    if n_blk < HEAD_DIM:
        parts.append(jnp.zeros((HEAD_DIM - n_blk, tq), F32))
    bias_t = jnp.concatenate(parts, axis=0)
    bias_ref[...] = bias_t.T.astype(BF16)


def _band_masks(n_var, tq, nk, k0_of, lo_dist, hi_dist):
    r = np.arange(tq)[None, :, None]
    c = np.arange(nk)[None, None, :]
    v = np.arange(n_var)[:, None, None]
    dist = v * tq + r - (np.asarray([k0_of(x) for x in range(n_var)])[:, None, None] + c)
    return jnp.asarray(np.where((dist >= lo_dist) & (dist <= hi_dist), 0.0, NEG), F32)


def _cmpwin(qa, kc, vc, kw, vw, gates, ov, n_cmp, n_sel):
    b, s, _ = qa.shape
    tq = CMP_Q_TILE
    nk = min(WINDOW + tq, s)
    nc = kc.shape[2]
    n_blk = s // SEL_BLOCK
    gw = NSA_GQA * LANES
    n_early = (nk - tq) // tq
    wmask = _band_masks(n_early + 1, tq, nk, lambda v: max(v * tq + tq - nk, 0), 0, WINDOW - 1)
    kern = functools.partial(_cmpwin_kernel, tq=tq, nk=nk, n_cmp=n_cmp, n_blk=n_blk, n_sel=n_sel)
    kc_spec = pl.BlockSpec((None, None, nc, LANES), lambda b_, g, i: (b_, g, 0, 0))
    vc_spec = pl.BlockSpec((None, None, nc, 2 * LANES), lambda b_, g, i: (b_, g, 0, 0))
    return pl.pallas_call(
        kern,
        grid=(b, NSA_KV_HEADS, s // tq),
        in_specs=[pl.BlockSpec((None, tq, gw), lambda b_, g, i: (b_, i, g)), kc_spec, vc_spec,
                  pl.BlockSpec((None, s, LANES), lambda b_, g, i: (b_, 0, g)),
                  pl.BlockSpec((None, s, 2 * LANES), lambda b_, g, i: (b_, 0, g)),
                  pl.BlockSpec((None, tq, LANES), lambda b_, g, i: (b_, i, g)),
                  _const_spec(ov.shape),
                  pl.BlockSpec((None, tq, nk), lambda b_, g, i: (jnp.minimum(i, n_early), 0, 0))],
        out_specs=[pl.BlockSpec((None, tq, 2 * LANES), lambda b_, g, i: (b_, i, g)),
                   pl.BlockSpec((None, None, tq, LANES), lambda b_, g, i: (b_, g, i, 0))],
        out_shape=[jax.ShapeDtypeStruct((b, s, NSA_KV_HEADS * 2 * LANES), F32),
                   jax.ShapeDtypeStruct((b, NSA_KV_HEADS, s, LANES), BF16)],
        compiler_params=_cparams(3), name="cmp_win_select",
    )(qa, kc, vc, kw, vw, gates, ov, wmask)


def _sel_kernel(q_ref, bias_ref, k_ref, v_ref, g_ref, dm_ref, o_ref, m_ref, acc_ref, *, tq, tk):
    i = pl.program_id(2)
    rows = NSA_GQA * tq
    half = 2 * tq
    q4 = _stack_q(q_ref, bias_ref[...])
    m_ref[...] = jnp.full((rows, LANES), NEG, F32)
    acc_ref[...] = jnp.zeros((rows, LANES), F32)
    rep = tk // LANES

    def step(kt, masked):
        k0 = pl.multiple_of(kt * tk, tk)
        s = _dot_nt(q4, k_ref[pl.ds(k0, tk), :])
        if masked:
            s = _add_tile_mask(s, dm_ref[...], tq)
        m_prev = m_ref[...]
        m_next = jnp.maximum(m_prev, jnp.max(s, axis=-1, keepdims=True))
        p = jnp.exp2(s - jnp.tile(m_next, (1, rep))).astype(BF16)
        alpha = jnp.exp2(m_prev - m_next)
        m_ref[...] = m_next
        acc_ref[:half] = acc_ref[:half] * alpha[:half] + _dot(p[:half], v_ref[pl.ds(k0, tk), :LANES])
        acc_ref[half:] = acc_ref[half:] * alpha[half:] + _dot(p[half:], v_ref[pl.ds(k0, tk), LANES:])

    kt_diag = (i * tq) // tk
    lax.fori_loop(0, kt_diag, lambda kt, c: (step(kt, False), c)[1], 0)
    step(kt_diag, True)
    num, den = _pair_num_den(acc_ref[:half], acc_ref[half:], _lane_lo(half))
    _store_pairs(o_ref, num / den * _gate_pairs(g_ref, 1, tq), tq)


def _sel_attention(qa, bias, ks, vs, gates):
    b, s, _ = qa.shape
    tq, tk = Q_TILE, min(SEL_KEY_TILE, s)
    gw = NSA_GQA * LANES
    n_var = tk // tq
    dmask = _band_masks(n_var, tq, tk, lambda v: 0, 0, tk)
    return pl.pallas_call(
        functools.partial(_sel_kernel, tq=tq, tk=tk),
        grid=(b, NSA_KV_HEADS, s // tq),
        in_specs=[pl.BlockSpec((None, tq, gw), lambda b_, g, i: (b_, i, g)),
                  pl.BlockSpec((None, None, tq, LANES), lambda b_, g, i: (b_, g, i, 0)),
                  pl.BlockSpec((None, s, LANES), lambda b_, g, i: (b_, 0, g)),
                  pl.BlockSpec((None, s, 2 * LANES), lambda b_, g, i: (b_, 0, g)),
                  pl.BlockSpec((None, tq, LANES), lambda b_, g, i: (b_, i, g)),
                  pl.BlockSpec((None, tq, tk), lambda b_, g, i: (i % n_var, 0, 0))],
        out_specs=pl.BlockSpec((None, tq, 2 * LANES), lambda b_, g, i: (b_, i, g)),
        out_shape=jax.ShapeDtypeStruct((b, s, NSA_KV_HEADS * 2 * LANES), F32),
        scratch_shapes=[pltpu.VMEM((NSA_GQA * tq, LANES), F32), pltpu.VMEM((NSA_GQA * tq, LANES), F32)],
        compiler_params=_cparams(3), name="sel_attention",
    )(qa, bias, ks, vs, gates, dmask)


def _mem_kv_kernel(mem_ref, gain_ref, w_ref, gt_ref, ones_ref, k_ref, v_ref):
    hn = _rms_rows(mem_ref[...], gain_ref[...]).astype(BF16)
    y = _dot(hn, w_ref[...])
    ms = _head_mean_sq(y[:, :MEM_W], ones_ref)
    for p in range(MEM_HEADS // 2):
        sl = slice(p * LANES, (p + 1) * LANES)
        z = _pair_norm(y[:, sl], ms[:, sl], gt_ref[...], None)
        _store_head_slots(k_ref, z, MEM_HEADS, p, 0.0)
        c = MEM_W + p * LANES
        _store_value_slots(v_ref, y[:, c:c + LANES], MEM_HEADS, p, False)


def _mem_kv(mem, gains, w, gt):
    b, nm, _ = mem.shape
    nl = gains.shape[0]
    width = MEM_HEADS * LANES
    out_spec = pl.BlockSpec((None, None, nm, width), lambda l, b_: (l, b_, 0, 0))
    return pl.pallas_call(
        _mem_kv_kernel,
        grid=(nl, b),
        in_specs=[pl.BlockSpec((None, nm, D_MODEL), lambda l, b_: (b_, 0, 0)),
                  pl.BlockSpec((None, 1, D_MODEL), lambda l, b_: (l, 0, 0)),
                  pl.BlockSpec((None, D_MODEL, 2 * MEM_W), lambda l, b_: (l, 0, 0)),
                  pl.BlockSpec((None, 1, LANES), lambda l, b_: (l, 0, 0)),
                  _const_spec((2 * LANES, 2 * LANES))],
        out_specs=[out_spec, out_spec],
        out_shape=[jax.ShapeDtypeStruct((nl, b, nm, width), BF16)] * 2,
        compiler_params=_cparams(2), name="mem_kv",
    )(mem, gains, w, gt, _head_ones())


def _mem_attn_kernel(q_ref, k_ref, v_ref, o_ref, *, tq):
    lo = _lane_lo(tq)
    accs = []
    for h in range(MEM_HEADS):
        sl = slice(h * LANES, (h + 1) * LANES)
        s = _dot_nt(q_ref[:, sl], k_ref[:, sl])
        m = jnp.max(s, axis=-1, keepdims=True)
        accs.append(_dot(jnp.exp2(s - m).astype(BF16), v_ref[:, sl]))
    for pair in range(MEM_HEADS // 2):
        num, den = _pair_num_den(accs[2 * pair], accs[2 * pair + 1], lo)
        o_ref[:, pair * LANES:(pair + 1) * LANES] = num / den


def _mem_attention(qm, mk, mv, layer):
    b, s, width = qm.shape
    nm = mk.shape[2]
    tq = min(MEM_Q_TILE, s)
    kv_spec = pl.BlockSpec((None, None, nm, width), lambda b_, i: (layer, b_, 0, 0))
    return pl.pallas_call(
        functools.partial(_mem_attn_kernel, tq=tq),
        grid=(b, s // tq),
        in_specs=[pl.BlockSpec((None, tq, width), lambda b_, i: (b_, i, 0)), kv_spec, kv_spec],
        out_specs=pl.BlockSpec((None, tq, MEM_W), lambda b_, i: (b_, i, 0)),
        out_shape=jax.ShapeDtypeStruct((b, s, MEM_W), F32),
        compiler_params=_cparams(2), name="mem_attention",
    )(qm, mk, mv)


def _mlp_tail(x1, gain_ref, wup_ref, wdn_ref, out_ref):
    hn = _rms_rows(x1, gain_ref[...]).astype(BF16)
    out_ref[...] = x1
    for c in range(D_FF // FF_CHUNK):
        u = _dot(hn, wup_ref[:, c * FF_CHUNK:(c + 1) * FF_CHUNK])
        u = jnp.square(jnp.maximum(u, 0.0)).astype(BF16)
        out_ref[...] += _dot(u, wdn_ref[c * FF_CHUNK:(c + 1) * FF_CHUNK, :])


def _a_out_kernel(ocw_ref, os_ref, om_ref, x_ref, wo_ref, gain_ref, wup_ref, wdn_ref, out_ref):
    o_main = (ocw_ref[...] + os_ref[...]).astype(BF16)
    x1 = x_ref[...] + _dot(o_main, wo_ref[:A_Q, :]) + _dot(om_ref[...].astype(BF16), wo_ref[A_Q:, :])
    _mlp_tail(x1, gain_ref, wup_ref, wdn_ref, out_ref)


def _b_out_kernel(od_ref, om_ref, x_ref, wo_ref, gain_ref, wup_ref, wdn_ref, out_ref):
    kd = DIL_SLOTS * HEAD_DIM
    x1 = (x_ref[...] + _dot(od_ref[...].astype(BF16), wo_ref[:kd, :])
          + _dot(om_ref[...].astype(BF16), wo_ref[kd:, :]))
    _mlp_tail(x1, gain_ref, wup_ref, wdn_ref, out_ref)


def _out_mlp(kern, acts, x2d, wo, gain, wup, wdn, name):
    n = x2d.shape[0]
    tm = ROW_TILE
    row_spec = lambda width: pl.BlockSpec((tm, width), lambda i: (i, 0))
    return pl.pallas_call(
        kern,
        grid=(n // tm,),
        in_specs=[row_spec(a.shape[1]) for a in acts] + [row_spec(D_MODEL), _const_spec(wo.shape),
                  _const_spec((1, D_MODEL)), _const_spec(wup.shape), _const_spec(wdn.shape)],
        out_specs=row_spec(D_MODEL),
        out_shape=jax.ShapeDtypeStruct((n, D_MODEL), F32),
        compiler_params=_cparams(1), name=name,
    )(*acts, x2d, wo, gain, wup, wdn)


def _b_proj_kernel(x_ref, gains_ref, wq_ref, wkv_ref, cos_ref, sin_ref, gt_ref, ones_ref,
                   q0_ref, q1_ref, q2_ref, qm_ref, k_ref, v_ref, *, tm):
    x = x_ref[...]
    xn = x * lax.rsqrt(jnp.mean(x * x, axis=-1, keepdims=True) + EPS)
    rope = _rope_args(cos_ref, sin_ref, tm)
    hq = (xn * gains_ref[0:1, :]).astype(BF16)
    width = DIL_SLOTS * HEAD_DIM
    for gi, ref in enumerate((q0_ref, q1_ref, q2_ref)):
        y = _dot(hq, wq_ref[:, gi * width:(gi + 1) * width])
        ms = _head_mean_sq(y, ones_ref)
        for p in range(DIL_SLOTS // 2):
            sl = slice(p * LANES, (p + 1) * LANES)
            z = _pair_norm(y[:, sl], ms[:, sl], gt_ref[gi:gi + 1, :], rope) * Q_SCALE
            _store_head_slots(ref, z, DIL_SLOTS, p, 0.0)
    y = _dot(hq, wq_ref[:, B_Q:B_Q + MEM_W])
    ms = _head_mean_sq(y, ones_ref)
    for p in range(MEM_HEADS // 2):
        sl = slice(p * LANES, (p + 1) * LANES)
        z = _pair_norm(y[:, sl], ms[:, sl], gt_ref[3:4, :], None) * Q_SCALE
        _store_head_slots(qm_ref, z, MEM_HEADS, p, 0.0)
    hk = (xn * gains_ref[1:2, :]).astype(BF16)
    y = _dot(hk, wkv_ref[...])
    ms = _head_mean_sq(y[:, :width], ones_ref)
    for p in range(DIL_SLOTS // 2):
        sl = slice(p * LANES, (p + 1) * LANES)
        z = _pair_norm(y[:, sl], ms[:, sl], gt_ref[4:5, :], rope)
        _store_head_slots(k_ref, z, DIL_SLOTS, p, 0.0)
        c = width + p * LANES
        _store_value_slots(v_ref, y[:, c:c + LANES], DIL_SLOTS, p, False)


def _b_proj(x2d, gains, wq, wkv, cos, sin, gt, seq):
    n = x2d.shape[0]
    tm = ROW_TILE
    nseq = seq // tm
    row_spec = lambda width: pl.BlockSpec((tm, width), lambda i: (i, 0))
    tab_spec = pl.BlockSpec((tm, LANES), lambda i: (i % nseq, 0))
    width = DIL_SLOTS * LANES
    widths = (width, width, width, MEM_HEADS * LANES, width, width)
    return pl.pallas_call(
        functools.partial(_b_proj_kernel, tm=tm),
        grid=(n // tm,),
        in_specs=[row_spec(D_MODEL), _const_spec(gains.shape), _const_spec(wq.shape),
                  _const_spec(wkv.shape), tab_spec, tab_spec, _const_spec(gt.shape),
                  _const_spec((2 * LANES, 2 * LANES))],
        out_specs=[row_spec(wd) for wd in widths],
        out_shape=[jax.ShapeDtypeStruct((n, wd), BF16) for wd in widths],
        compiler_params=_cparams(1), name="b_proj",
    )(x2d, gains, wq, wkv, cos, sin, gt, _head_ones())


def _dil_kernel(q0_ref, q1_ref, q2_ref, k_ref, v_ref, o_ref, qf_ref, kf_ref, vf_ref, lse_ref, *, seq):
    tq = DIL_Q_TILE
    heads = (slice(0, LANES), slice(LANES, 2 * LANES))
    for hd, sl in enumerate(heads):
        kf_ref[hd] = k_ref[:, sl].astype(F32)
        vf_ref[hd] = v_ref[:, sl].astype(F32)

    def attend(qs, ks, vs, qpos0, kpos0):
        nq, nk = qs[0].shape[0], ks[0].shape[0]
        lo = _lane_lo(nq)
        dist = (qpos0 + lax.broadcasted_iota(jnp.int32, (nq, 1), 0)
                - (kpos0 + lax.broadcasted_iota(jnp.int32, (1, nk), 1)))
        mask = (dist >= 0) & (dist <= DIL_BAND)
        accs, ms = [], []
        for q, k, v in zip(qs, ks, vs):
            s = jnp.where(mask, _dot_nt(q, k), NEG)
            m = jnp.max(s, axis=-1, keepdims=True)
            accs.append(_dot(jnp.exp2(s - m).astype(BF16), v))
            ms.append(jnp.broadcast_to(m, (nq, LANES)))
        num, den = _pair_num_den(accs[0], accs[1], lo)
        return num / den, jnp.where(lo, ms[0], ms[1]) + jnp.log2(den)

    def merge(o_old, l_old, o_new, l_new):
        mx = jnp.maximum(l_old, l_new)
        a = jnp.exp2(l_old - mx)
        b = jnp.exp2(l_new - mx)
        den = a + b
        return (a * o_old + b * o_new) / den, mx + jnp.log2(den)

    nk = min(tq + DIL_BAND, seq)

    def body0(i, c):
        q0 = pl.multiple_of(i * tq, tq)
        k0 = pl.multiple_of(jnp.clip(i * tq - DIL_BAND, 0, seq - nk), DIL_BAND)
        o, l = attend([q0_ref[pl.ds(q0, tq), sl] for sl in heads],
                      [k_ref[pl.ds(k0, nk), sl] for sl in heads],
                      [v_ref[pl.ds(k0, nk), sl] for sl in heads], q0, k0)
        o_ref[pl.ds(q0, tq), :] = o
        lse_ref[pl.ds(q0, tq), :] = l
        return c

    lax.fori_loop(0, seq // tq, body0, 0, unroll=DIL_UNROLL)

    for q_ref, (_, dil) in zip((q1_ref, q2_ref), DIL_PATTERNS[1:]):
        for hd, sl in enumerate(heads):
            qf_ref[hd] = q_ref[:, sl].astype(F32)
        length = seq // dil
        tqd = min(tq, length)
        nkd = min(tqd + DIL_BAND, length)
        ntile = length // tqd

        def body(it, c, dil=dil, length=length, tqd=tqd, nkd=nkd, ntile=ntile):
            r = it // ntile
            i = it - r * ntile
            qp = i * tqd
            kp = jnp.clip(qp - DIL_BAND, 0, length - nkd)
            qrows = pl.ds(r + dil * qp, tqd, stride=dil)
            krows = pl.ds(r + dil * kp, nkd, stride=dil)
            o, l = attend([qf_ref[hd, qrows, :].astype(BF16) for hd in range(2)],
                          [kf_ref[hd, krows, :].astype(BF16) for hd in range(2)],
                          [vf_ref[hd, krows, :].astype(BF16) for hd in range(2)], qp, kp)
            o, l = merge(o_ref[qrows, :], lse_ref[qrows, :], o, l)
            o_ref[qrows, :] = o
            lse_ref[qrows, :] = l
            return c

        lax.fori_loop(0, dil * ntile, body, 0, unroll=DIL_UNROLL)


def _dil_attention(q0, q1, q2, k, v):
    b, s, _ = k.shape
    npair = DIL_SLOTS // 2
    pw = 2 * LANES
    in_spec = pl.BlockSpec((None, s, pw), lambda b_, hp: (b_, 0, hp))
    return pl.pallas_call(
        functools.partial(_dil_kernel, seq=s),
        grid=(b, npair),
        in_specs=[in_spec] * 5,
        out_specs=pl.BlockSpec((None, s, LANES), lambda b_, hp: (b_, 0, hp)),
        out_shape=jax.ShapeDtypeStruct((b, s, DIL_SLOTS * HEAD_DIM), F32),
        scratch_shapes=[pltpu.VMEM((2, s, LANES), F32)] * 3 + [pltpu.VMEM((s, LANES), F32)],
        compiler_params=_cparams(2), name="dil_attention",
    )(q0, q1, q2, k, v)


def kernel(x, mem, attn_norm, mlp_norm, w_up, w_down, mem_norm, w_mem_kv, mem_q_norm, mem_k_norm,
           a_w_in, a_w_out, a_q_norm, a_k_norm, a_cmp_pos, a_cmp_w1, a_cmp_b1, a_cmp_w2, a_cmp_b2,
           kv_norm, w_kv_shared, kv_k_norm, b_w_in, b_w_out, b_q_norm):
    b, s, _ = x.shape
    n = b * s
    assert s % ROW_TILE == 0 and s % SEL_KEY_TILE == 0 and s // SEL_BLOCK <= HEAD_DIM
    assert s >= WINDOW + CMP_Q_TILE and (s // DIL_PATTERNS[-1][1]) % DIL_BAND == 0
    x2d = x.reshape(n, D_MODEL)
    cos, sin = _rope_pair_tables(jnp.arange(s))

    mk, mv = _mem_kv(mem, mem_norm[:, None, :], w_mem_kv.astype(BF16), _gain_pair(mem_k_norm)[:, None, :])

    gt_a = jnp.zeros((SUBLANES, LANES), F32)
    gt_a = gt_a.at[0].set(_gain_pair(a_q_norm[0])).at[1].set(_gain_pair(a_k_norm[0, 1]))
    gt_a = gt_a.at[2].set(_gain_pair(a_k_norm[0, 2])).at[3].set(_gain_pair(mem_q_norm[0]))
    qa, ks, kw, vs, vw, kcvc, qm, gates = _a_proj(
        x2d, attn_norm[0:1], _prep_a_w_in(a_w_in[0]), cos, sin, gt_a, s)

    n_cmp = (s - CMP_BLOCK) // CMP_STRIDE + 1
    nc = s // CMP_STRIDE
    tab_c = _rope_table(jnp.arange(nc) * CMP_STRIDE + (CMP_BLOCK - 1))
    kc, vc = _compress(kcvc.reshape(b, s, -1),
                       *_prep_compress(a_cmp_pos[0], a_cmp_w1[0], a_cmp_b1[0], a_cmp_w2[0], a_cmp_b2[0]),
                       tab_c, _gain_slot(a_k_norm[0, 0], "rot")[None])

    n_blk = s // SEL_BLOCK
    c_start = np.arange(nc)[None, :] * CMP_STRIDE
    b_start = np.arange(n_blk)[:, None] * SEL_BLOCK
    overlap_t = ((c_start < b_start + SEL_BLOCK) & (c_start + CMP_BLOCK > b_start)
                 & (np.arange(nc)[None, :] < n_cmp))
    ov = jnp.asarray(overlap_t, BF16)

    qa3 = qa.reshape(b, s, -1)
    gates3 = gates.reshape(b, s, -1)
    o_cw, bias = _cmpwin(qa3, kc, vc, kw.reshape(b, s, -1), vw.reshape(b, s, -1), gates3, ov,
                         n_cmp, min(SEL_TOPK, n_blk))
    o_s = _sel_attention(qa3, bias, ks.reshape(b, s, -1), vs.reshape(b, s, -1), gates3)
    o_m = _mem_attention(qm.reshape(b, s, -1), mk, mv, 0)
    x2d = _out_mlp(_a_out_kernel,
                   [o_cw.reshape(n, -1), o_s.reshape(n, -1), o_m.reshape(n, -1)],
                   x2d, a_w_out[0].astype(BF16), mlp_norm[0:1], w_up[0].astype(BF16),
                   w_down[0].astype(BF16), "a_out_mlp")

    gt_b = jnp.zeros((SUBLANES, LANES), F32)
    for gi in range(N_DIL_GROUPS):
        gt_b = gt_b.at[gi].set(_gain_pair(b_q_norm[0, gi]))
    gt_b = gt_b.at[3].set(_gain_pair(mem_q_norm[1])).at[4].set(_gain_pair(kv_k_norm))
    q0, q1, q2, qm1, kb, vb = _b_proj(x2d, jnp.stack([attn_norm[1], kv_norm]), b_w_in[0].astype(BF16),
                                      w_kv_shared.astype(BF16), cos, sin, gt_b, s)
    o_d = _dil_attention(*(t.reshape(b, s, -1) for t in (q0, q1, q2, kb, vb)))
    o_m = _mem_attention(qm1.reshape(b, s, -1), mk, mv, 1)
    x2d = _out_mlp(_b_out_kernel, [o_d.reshape(n, -1), o_m.reshape(n, -1)], x2d, b_w_out[0].astype(BF16),
                   mlp_norm[1:2], w_up[1].astype(BF16), w_down[1].astype(BF16), "b_out_mlp")
    return x2d.reshape(b, s, D_MODEL)
```

```python
import functools

import numpy as np
import jax
import jax.numpy as jnp
from jax import lax
from jax.experimental import pallas as pl
from jax.experimental.pallas import tpu as pltpu

D_MODEL = 1024
HEAD_DIM = 64
HALF = HEAD_DIM // 2
ROPE_THETA = 10000.0
EPS = 1e-6
NEG = -1e30
D_FF = 4 * D_MODEL
MEM_HEADS = 4
NSA_HEADS = 12
NSA_KV_HEADS = 3
NSA_GQA = NSA_HEADS // NSA_KV_HEADS
CMP_BLOCK = 32
CMP_STRIDE = 16
CMP_HIDDEN = 256
SEL_BLOCK = 64
SEL_TOPK = 16
WINDOW = 512
SEL_FORCE = 1e9
DIL_PATTERNS = ((128, 1), (512, 4), (2048, 16))
N_DIL_GROUPS = 3
DIL_SLOTS = 8
MEM_W = MEM_HEADS * HEAD_DIM
A_Q = NSA_HEADS * HEAD_DIM
A_KV = NSA_KV_HEADS * HEAD_DIM
B_Q = N_DIL_GROUPS * DIL_SLOTS * HEAD_DIM
LOG2E = 1.4426950408889634
Q_SCALE = HEAD_DIM ** -0.5 * LOG2E

LANES = 128
SUBLANES = 8
VMEM_LIMIT = 56 * 1024 * 1024
ROW_TILE = 512
Q_TILE = 256
DIL_UNROLL = 4
SEL_KEY_TILE = 512
MEM_Q_TILE = 512
DIL_Q_TILE = 128
DIL_Q_TILE_WIDE = 256
DIL_BAND = 128
FF_CHUNK = 1024

BF16 = jnp.bfloat16
F32 = jnp.float32


def _cparams(n_grid):
    return pltpu.CompilerParams(dimension_semantics=("arbitrary",) * n_grid,
                                vmem_limit_bytes=VMEM_LIMIT)


def _const_spec(shape):
    nd = len(shape)
    return pl.BlockSpec(shape, lambda *_: (0,) * nd, pipeline_mode=pl.Buffered(1))


def _dot(a, b):
    return jnp.dot(a, b, preferred_element_type=F32)


def _dot_nt(a, b):
    return lax.dot_general(a, b, (((1,), (1,)), ((), ())), preferred_element_type=F32)


def _rms_rows(x, gain):
    return x * lax.rsqrt(jnp.mean(x * x, axis=-1, keepdims=True) + EPS) * gain


def _slot_norm(y, gt, tab):
    z = y * lax.rsqrt(jnp.mean(y * y, axis=-1, keepdims=True) + EPS) * gt
    if tab is not None:
        z = z * tab
        z = z + pltpu.roll(z, HEAD_DIM, 1)
    return z


def _head_mean_sq(y, ones_ref):
    sq = (y * y).astype(BF16)
    width = y.shape[1]
    step = ones_ref.shape[0]
    parts = []
    for c in range(0, width, step):
        cw = min(step, width - c)
        parts.append(_dot(sq[:, c:c + cw], ones_ref[:cw, :cw]))
    return jnp.concatenate(parts, axis=1) * (1.0 / HEAD_DIM)


def _pair_norm(y, ms, gain, rope):
    z = y * lax.rsqrt(ms + EPS) * gain
    if rope is not None:
        cos, sin_signed, first = rope
        partner = jnp.where(first, pltpu.roll(z, LANES - HALF, 1), pltpu.roll(z, HALF, 1))
        z = z * cos + partner * sin_signed
    return z


def _lane_lo(rows):
    return lax.broadcasted_iota(jnp.int32, (rows, LANES), 1) < HEAD_DIM


def _swap_halves(x):
    return pltpu.roll(x, HEAD_DIM, 1)


def _pair_num_den(acc_even, acc_odd, lo):
    return jnp.where(lo, acc_even, acc_odd), _swap_halves(jnp.where(lo, acc_odd, acc_even))


def _rot_half_cols(w):
    return jnp.concatenate([-w[..., HALF:], w[..., :HALF]], axis=-1)


def _swap_half(g):
    return jnp.concatenate([g[..., HALF:], g[..., :HALF]], axis=-1)


def _slots(w, n, kind):
    k = w.shape[0]
    w = w.reshape(k, n, HEAD_DIM)
    other = _rot_half_cols(w) if kind == "rot" else w
    return jnp.concatenate([w, other], axis=-1).reshape(k, n * LANES)


def _gain_slot(g, kind):
    other = _swap_half(g) if kind == "rot" else g
    return jnp.concatenate([g, other], axis=-1)


def _gain_pair(g):
    return jnp.concatenate([g, g], axis=-1)


def _rope_cos_sin(pos):
    inv_freq = ROPE_THETA ** (-jnp.arange(HALF, dtype=F32) / HALF)
    ang = jnp.asarray(pos, F32)[:, None] * inv_freq[None, :]
    return jnp.cos(ang), jnp.sin(ang)


def _rope_table(pos):
    cos, sin = _rope_cos_sin(pos)
    return jnp.concatenate([cos, cos, sin, sin], axis=-1)


def _rope_pair_tables(pos):
    cos, sin = _rope_cos_sin(pos)
    return (jnp.concatenate([cos, cos, cos, cos], axis=-1),
            jnp.concatenate([-sin, sin, -sin, sin], axis=-1))


def _pad_cols(w, width):
    return jnp.pad(w, ((0, 0), (0, width - w.shape[1])))


KV_PAIRS = (NSA_KV_HEADS + 1) // 2
A_COL_Q = 0
A_COL_KS = A_COL_Q + A_Q
A_COL_KW = A_COL_KS + KV_PAIRS * LANES
A_COL_VS = A_COL_KW + KV_PAIRS * LANES
A_COL_VW = A_COL_VS + KV_PAIRS * LANES
A_COL_KCVC = A_COL_VW + KV_PAIRS * LANES
A_COL_QM = A_COL_KCVC + NSA_KV_HEADS * LANES
A_COL_G = A_COL_QM + MEM_W
A_COLS = A_COL_G + NSA_KV_HEADS * LANES


def _prep_a_w_in(w):
    o = 0
    q = w[:, o:o + A_Q]; o += A_Q
    kc = w[:, o:o + A_KV]; o += A_KV
    vc = w[:, o:o + A_KV]; o += A_KV
    ks = w[:, o:o + A_KV]; o += A_KV
    vs = w[:, o:o + A_KV]; o += A_KV
    kw = w[:, o:o + A_KV]; o += A_KV
    vw = w[:, o:o + A_KV]; o += A_KV
    qm = w[:, o:o + MEM_W]; o += MEM_W
    gl = w[:, o:]
    gl = gl.reshape(D_MODEL, NSA_KV_HEADS, NSA_GQA, 3).transpose(0, 1, 3, 2)
    gl = gl.reshape(D_MODEL, NSA_KV_HEADS, 3 * NSA_GQA)
    gl = jnp.pad(gl, ((0, 0), (0, 0), (0, LANES - 3 * NSA_GQA))).reshape(D_MODEL, NSA_KV_HEADS * LANES)
    kcvc = jnp.concatenate([kc.reshape(D_MODEL, NSA_KV_HEADS, HEAD_DIM),
                            vc.reshape(D_MODEL, NSA_KV_HEADS, HEAD_DIM)], axis=-1)
    kvw = KV_PAIRS * LANES
    cols = [q, _pad_cols(ks, kvw), _pad_cols(kw, kvw), _pad_cols(vs, kvw), _pad_cols(vw, kvw),
            kcvc.reshape(D_MODEL, NSA_KV_HEADS * LANES), qm, gl]
    return jnp.concatenate(cols, axis=1).astype(BF16)


def _store_head_slots(ref, z, n_heads, pair, aux):
    lo = _lane_lo(z.shape[0])
    for half, data in enumerate((z, _swap_halves(z))):
        h = 2 * pair + half
        if h < n_heads:
            ref[:, h * LANES:(h + 1) * LANES] = jnp.where(lo, data, aux).astype(ref.dtype)


def _store_value_slots(ref, y, n_heads, pair, both):
    lo = _lane_lo(y.shape[0])
    ys = _swap_halves(y)
    for half in range(2):
        h = 2 * pair + half
        if h >= n_heads:
            continue
        even = jnp.where(lo, y if half == 0 else ys, 1.0)
        odd = jnp.where(lo, 1.0, ys if half == 0 else y)
        if both:
            ref[:, 2 * h * LANES:(2 * h + 1) * LANES] = even.astype(ref.dtype)
            ref[:, (2 * h + 1) * LANES:(2 * h + 2) * LANES] = odd.astype(ref.dtype)
        else:
            ref[:, h * LANES:(h + 1) * LANES] = (even if h % 2 == 0 else odd).astype(ref.dtype)


def _rope_args(cos_ref, sin_ref, rows):
    lane = lax.broadcasted_iota(jnp.int32, (rows, LANES), 1)
    return cos_ref[...], sin_ref[...], (lane & HALF) == 0


def _a_proj_kernel(x_ref, gain_ref, w_ref, cos_ref, sin_ref, gt_ref, ones_ref,
                   qa_ref, ks_ref, kw_ref, vs_ref, vw_ref, kcvc_ref, qm_ref, g_ref, *, seq, tm):
    hn = _rms_rows(x_ref[...], gain_ref[...]).astype(BF16)
    rope = _rope_args(cos_ref, sin_ref, tm)
    lane = lax.broadcasted_iota(jnp.int32, (tm, LANES), 1)
    row = lax.broadcasted_iota(jnp.int32, (tm, LANES), 0)
    tok = (pl.program_id(0) % (seq // tm)) * tm + row
    blk_ind = jnp.where(lane - HEAD_DIM == tok // SEL_BLOCK, 1.0, 0.0)

    yq = _dot(hn, w_ref[:, A_COL_Q:A_COL_KS])
    ms = _head_mean_sq(yq, ones_ref)
    for p in range(NSA_HEADS // 2):
        sl = slice(p * LANES, (p + 1) * LANES)
        z = _pair_norm(yq[:, sl], ms[:, sl], gt_ref[0:1, :], rope) * Q_SCALE
        _store_head_slots(qa_ref, z, NSA_HEADS, p, 0.0)

    yk = _dot(hn, w_ref[:, A_COL_KS:A_COL_VS])
    ms = _head_mean_sq(yk, ones_ref)
    for p in range(KV_PAIRS):
        sl = slice(p * LANES, (p + 1) * LANES)
        z = _pair_norm(yk[:, sl], ms[:, sl], gt_ref[1:2, :], rope)
        _store_head_slots(ks_ref, z, NSA_KV_HEADS, p, blk_ind)
        sl = slice((KV_PAIRS + p) * LANES, (KV_PAIRS + p + 1) * LANES)
        z = _pair_norm(yk[:, sl], ms[:, sl], gt_ref[2:3, :], rope)
        _store_head_slots(kw_ref, z, NSA_KV_HEADS, p, 0.0)

    yv = _dot(hn, w_ref[:, A_COL_VS:A_COL_KCVC])
    for j, ref in enumerate((vs_ref, vw_ref)):
        for p in range(KV_PAIRS):
            c = (j * KV_PAIRS + p) * LANES
            _store_value_slots(ref, yv[:, c:c + LANES], NSA_KV_HEADS, p, True)

    yr = _dot(hn, w_ref[:, A_COL_KCVC:A_COLS])
    kcvc_ref[...] = yr[:, :NSA_KV_HEADS * LANES]
    c = A_COL_QM - A_COL_KCVC
    ym = yr[:, c:c + MEM_W]
    ms = _head_mean_sq(ym, ones_ref)
    for p in range(MEM_HEADS // 2):
        sl = slice(p * LANES, (p + 1) * LANES)
        z = _pair_norm(ym[:, sl], ms[:, sl], gt_ref[3:4, :], None) * Q_SCALE
        _store_head_slots(qm_ref, z, MEM_HEADS, p, 0.0)
    c = A_COL_G - A_COL_KCVC
    g_ref[...] = jax.nn.sigmoid(yr[:, c:c + NSA_KV_HEADS * LANES])


def _head_ones():
    idx = np.arange(2 * LANES) // HEAD_DIM
    return jnp.asarray(idx[:, None] == idx[None, :], BF16)


def _a_proj(x2d, gain, w, cos, sin, gt, seq):
    n = x2d.shape[0]
    tm = ROW_TILE
    nseq = seq // tm
    row_spec = lambda width: pl.BlockSpec((tm, width), lambda i: (i, 0))
    tab_spec = pl.BlockSpec((tm, LANES), lambda i: (i % nseq, 0))
    kvw = NSA_KV_HEADS * LANES
    widths = (NSA_HEADS * LANES, kvw, kvw, 2 * kvw, 2 * kvw, kvw, MEM_HEADS * LANES, kvw)
    dtypes = (BF16, BF16, BF16, BF16, BF16, F32, BF16, F32)
    return pl.pallas_call(
        functools.partial(_a_proj_kernel, seq=seq, tm=tm),
        grid=(n // tm,),
        in_specs=[row_spec(D_MODEL), _const_spec((1, D_MODEL)), _const_spec((D_MODEL, A_COLS)),
                  tab_spec, tab_spec, _const_spec((SUBLANES, LANES)), _const_spec((2 * LANES, 2 * LANES))],
        out_specs=[row_spec(wd) for wd in widths],
        out_shape=[jax.ShapeDtypeStruct((n, wd), dt) for wd, dt in zip(widths, dtypes)],
        compiler_params=_cparams(1), name="a_proj",
    )(x2d, gain, w, cos, sin, gt, _head_ones())


def _compress_kernel(x_ref, pos_ref, w1_ref, b1_ref, w2_ref, b2_ref, tab_ref, gt_ref, kc_ref, vc_ref):
    nc = kc_ref.shape[0]
    lo = _lane_lo(nc)
    h = jnp.concatenate([x_ref[pl.ds(j, nc, stride=CMP_STRIDE), :] for j in range(CMP_STRIDE)], axis=1)
    top = _dot((h + pos_ref[0:1, :]).astype(BF16), w1_ref[0])
    bot = _dot((h + pos_ref[1:2, :]).astype(BF16), w1_ref[1])
    pre = top + pltpu.roll(bot, nc - 1, 0) + b1_ref[...]
    hid = jax.nn.gelu(pre).astype(BF16)
    yk = _dot(hid[:, :CMP_HIDDEN], w2_ref[0]) + b2_ref[0:1, :]
    yv = _dot(hid[:, CMP_HIDDEN:], w2_ref[1]) + b2_ref[1:2, :]
    kc_ref[...] = jnp.where(lo, _slot_norm(yk, gt_ref[...], tab_ref[...]), 0.0).astype(BF16)
    vc_ref[:, :LANES] = jnp.where(lo, yv, 1.0).astype(BF16)
    vc_ref[:, LANES:] = jnp.where(lo, 1.0, yv).astype(BF16)


def _compress(kcvc, pos, w1, b1, w2, b2, tab_c, gt):
    b, s, _ = kcvc.shape
    nc = s // CMP_STRIDE
    ospec = lambda width: pl.BlockSpec((None, None, nc, width), lambda b_, g: (b_, g, 0, 0))
    return pl.pallas_call(
        _compress_kernel,
        grid=(b, NSA_KV_HEADS),
        in_specs=[pl.BlockSpec((None, s, LANES), lambda b_, g: (b_, 0, g)),
                  _const_spec(pos.shape), _const_spec(w1.shape), _const_spec(b1.shape),
                  _const_spec(w2.shape), _const_spec(b2.shape), _const_spec(tab_c.shape),
                  _const_spec(gt.shape)],
        out_specs=[ospec(LANES), ospec(2 * LANES)],
        out_shape=[jax.ShapeDtypeStruct((b, NSA_KV_HEADS, nc, LANES), BF16),
                   jax.ShapeDtypeStruct((b, NSA_KV_HEADS, nc, 2 * LANES), BF16)],
        compiler_params=_cparams(2), name="compress",
    )(kcvc, pos, w1, b1, w2, b2, tab_c, gt)


def _prep_compress(cmp_pos, cmp_w1, cmp_b1, cmp_w2, cmp_b2):
    p = cmp_pos.reshape(2, 2, CMP_STRIDE, HEAD_DIM).transpose(1, 2, 0, 3).reshape(2, CMP_STRIDE * LANES)
    w = cmp_w1.reshape(2, 2, CMP_STRIDE, HEAD_DIM, CMP_HIDDEN)
    z = jnp.zeros_like(w[0])
    wk = jnp.concatenate([w[0], z], axis=-1)
    wv = jnp.concatenate([z, w[1]], axis=-1)
    w1 = jnp.concatenate([wk, wv], axis=2).reshape(2, CMP_STRIDE * LANES, 2 * CMP_HIDDEN)
    b1 = jnp.concatenate([cmp_b1[0], cmp_b1[1]])[None]
    w2 = jnp.stack([_slots(cmp_w2[0], 1, "rot"), _slots(cmp_w2[1], 1, "dup")])
    b2 = jnp.stack([_slots(cmp_b2[0][None], 1, "rot")[0], _slots(cmp_b2[1][None], 1, "dup")[0]])
    return p, w1.astype(BF16), b1, w2.astype(BF16), b2


STACK_ORDER = (0, 2, 1, 3)


def _stack_q(q_ref, extra=None):
    parts = []
    for r in STACK_ORDER:
        q = q_ref[:, r * LANES:(r + 1) * LANES]
        parts.append(q if extra is None else q + extra)
    return jnp.concatenate(parts, axis=0)


def _gate_pairs(g_ref, branch, tq):
    lo = _lane_lo(tq)
    col = lambda r: jnp.broadcast_to(g_ref[:, branch * NSA_GQA + r:branch * NSA_GQA + r + 1], (tq, LANES))
    return jnp.concatenate([jnp.where(lo, col(0), col(1)), jnp.where(lo, col(2), col(3))], axis=0)


def _store_pairs(o_ref, out, tq):
    o_ref[:, :LANES] = out[:tq]
    o_ref[:, LANES:] = out[tq:]


def _add_tile_mask(s, mask, tq):
    nk = s.shape[1]
    return (s.reshape(NSA_GQA, tq, nk) + mask).reshape(NSA_GQA * tq, nk)


def _window_branch(i, q_ref, kw_ref, vw_ref, g_ref, wm_ref, tq, nk):
    k0 = pl.multiple_of(jnp.maximum(i * tq + tq - nk, 0), tq)
    s = _add_tile_mask(_dot_nt(_stack_q(q_ref), kw_ref[pl.ds(k0, nk), :]), wm_ref[...], tq)
    m = jnp.max(s, axis=-1, keepdims=True)
    p = jnp.exp2(s - m).astype(BF16)
    num, den = _pair_num_den(_dot(p[:2 * tq], vw_ref[pl.ds(k0, nk), :LANES]),
                             _dot(p[2 * tq:], vw_ref[pl.ds(k0, nk), LANES:]), _lane_lo(2 * tq))
    return num / den * _gate_pairs(g_ref, 2, tq)


def _compressed_branch(i, q_ref, kc_ref, vc_ref, g_ref, ov_ref, tq, n_cmp, n_blk, n_sel):
    lo = _lane_lo(tq)
    kc = kc_ref[...]
    nc = kc.shape[0]
    t_row = i * tq + lax.broadcasted_iota(jnp.int32, (1, tq), 1)
    c_col = lax.broadcasted_iota(jnp.int32, (nc, 1), 0)
    cend_col = jnp.where(c_col < n_cmp, c_col * CMP_STRIDE + (CMP_BLOCK - 1), jnp.int32(2 ** 30))
    mask_t = cend_col <= t_row
    psum = jnp.zeros((nc, tq), F32)
    xs = []
    for r in range(NSA_GQA):
        st = jnp.where(mask_t, _dot_nt(kc, q_ref[:, r * LANES:(r + 1) * LANES]), NEG)
        mt = jnp.max(st, axis=0, keepdims=True)
        pt = jnp.where(mask_t, jnp.exp2(st - mt), 0.0)
        lt = jnp.sum(pt, axis=0, keepdims=True)
        pt = pt * jnp.where(lt > 0.0, 1.0 / lt, 0.0)
        psum = psum + pt
        vsl = slice(0, LANES) if r % 2 == 0 else slice(LANES, 2 * LANES)
        xs.append(_dot(pt.T.astype(BF16), vc_ref[:, vsl]))
    out_c = jnp.concatenate([jnp.where(lo, xs[0], xs[1]), jnp.where(lo, xs[2], xs[3])], axis=0)
    out_c = out_c * _gate_pairs(g_ref, 0, tq)

    ps_hi = psum.astype(BF16)
    ps_lo = (psum - ps_hi.astype(F32)).astype(BF16)
    ov = ov_ref[...]
    imp = _dot(ov, ps_hi) + _dot(ov, ps_lo)

    cur = t_row // SEL_BLOCK
    ngrp = n_blk // SUBLANES
    jsub = lax.broadcasted_iota(jnp.int32, (SUBLANES, tq), 0)
    vals = []
    for a in range(ngrp):
        j = jsub + a * SUBLANES
        forced = (j == 0) | (j == cur) | (j == cur - 1)
        v = jnp.where(j <= cur, imp[a * SUBLANES:(a + 1) * SUBLANES], NEG)
        vals.append(jnp.where(forced, SEL_FORCE, v))
    ranks = [jnp.zeros((SUBLANES, tq), F32) for _ in range(ngrp)]
    for jj in range(n_blk):
        a0, s0 = divmod(jj, SUBLANES)
        rowb = jnp.broadcast_to(vals[a0][s0:s0 + 1, :], (SUBLANES, tq))
        for a in range(ngrp):
            if a > a0:
                beats = jnp.where(rowb >= vals[a], 1.0, 0.0)
            elif a < a0:
                beats = jnp.where(rowb > vals[a], 1.0, 0.0)
            else:
                beats = jnp.where(jsub > s0, jnp.where(rowb >= vals[a], 1.0, 0.0),
                                  jnp.where(rowb > vals[a], 1.0, 0.0))
            ranks[a] = ranks[a] + beats
    parts = [jnp.zeros((HEAD_DIM, tq), F32)]
    for a in range(ngrp):
        live = jnp.where(vals[a] > NEG / 2, 0.0, NEG)
        parts.append(jnp.where(ranks[a] < n_sel, live, NEG))
    if n_blk < HEAD_DIM:
        parts.append(jnp.zeros((HEAD_DIM - n_blk, tq), F32))
    bias_t = jnp.concatenate(parts, axis=0)
    return out_c, bias_t.T.astype(BF16)


def _band_masks(n_var, tq, nk, k0_of, lo_dist, hi_dist):
    r = np.arange(tq)[None, :, None]
    c = np.arange(nk)[None, None, :]
    v = np.arange(n_var)[:, None, None]
    dist = v * tq + r - (np.asarray([k0_of(x) for x in range(n_var)])[:, None, None] + c)
    return jnp.asarray(np.where((dist >= lo_dist) & (dist <= hi_dist), 0.0, NEG), F32)


def _selected_branch(i, q_ref, bias, k_ref, v_ref, g_ref, dm_ref, m_ref, acc_ref, tq, tk):
    rows = NSA_GQA * tq
    half = 2 * tq
    q4 = _stack_q(q_ref, bias)
    m_ref[...] = jnp.full((rows, LANES), NEG, F32)
    acc_ref[...] = jnp.zeros((rows, LANES), F32)
    rep = tk // LANES

    def step(kt, masked):
        k0 = pl.multiple_of(kt * tk, tk)
        s = _dot_nt(q4, k_ref[pl.ds(k0, tk), :])
        if masked:
            s = _add_tile_mask(s, dm_ref[...], tq)
        m_prev = m_ref[...]
        m_next = jnp.maximum(m_prev, jnp.max(s, axis=-1, keepdims=True))
        p = jnp.exp2(s - jnp.tile(m_next, (1, rep))).astype(BF16)
        alpha = jnp.exp2(m_prev - m_next)
        m_ref[...] = m_next
        acc_ref[:half] = acc_ref[:half] * alpha[:half] + _dot(p[:half], v_ref[pl.ds(k0, tk), :LANES])
        acc_ref[half:] = acc_ref[half:] * alpha[half:] + _dot(p[half:], v_ref[pl.ds(k0, tk), LANES:])

    kt_diag = (i * tq) // tk
    lax.fori_loop(0, kt_diag, lambda kt, c: (step(kt, False), c)[1], 0)
    step(kt_diag, True)
    num, den = _pair_num_den(acc_ref[:half], acc_ref[half:], _lane_lo(half))
    return num / den * _gate_pairs(g_ref, 1, tq)


def _nsa_kernel(q_ref, kc_ref, vc_ref, kw_ref, vw_ref, ks_ref, vs_ref, g_ref, ov_ref, wm_ref, dm_ref,
                o_ref, m_ref, acc_ref, *, tq, tk, nk, n_cmp, n_blk, n_sel):
    i = pl.program_id(2)
    out_w = _window_branch(i, q_ref, kw_ref, vw_ref, g_ref, wm_ref, tq, nk)
    out_c, bias = _compressed_branch(i, q_ref, kc_ref, vc_ref, g_ref, ov_ref, tq, n_cmp, n_blk, n_sel)
    _store_pairs(o_ref, out_c + out_w, tq)
    out_s = _selected_branch(i, q_ref, bias, ks_ref, vs_ref, g_ref, dm_ref, m_ref, acc_ref, tq, tk)
    o_ref[:, :LANES] += out_s[:tq]
    o_ref[:, LANES:] += out_s[tq:]


def _nsa_attention(qa, kc, vc, kw, vw, ks, vs, gates, ov, n_cmp, n_sel):
    b, s, _ = qa.shape
    tq, tk = Q_TILE, min(SEL_KEY_TILE, s)
    nk = min(WINDOW + tq, s)
    nc = kc.shape[2]
    n_blk = s // SEL_BLOCK
    gw = NSA_GQA * LANES
    n_early = (nk - tq) // tq
    wmask = _band_masks(n_early + 1, tq, nk, lambda v: max(v * tq + tq - nk, 0), 0, WINDOW - 1)
    n_var = tk // tq
    dmask = _band_masks(n_var, tq, tk, lambda v: 0, 0, tk)
    kern = functools.partial(_nsa_kernel, tq=tq, tk=tk, nk=nk, n_cmp=n_cmp, n_blk=n_blk, n_sel=n_sel)
    k_spec = pl.BlockSpec((None, s, LANES), lambda b_, g, i: (b_, 0, g))
    v_spec = pl.BlockSpec((None, s, 2 * LANES), lambda b_, g, i: (b_, 0, g))
    return pl.pallas_call(
        kern,
        grid=(b, NSA_KV_HEADS, s // tq),
        in_specs=[pl.BlockSpec((None, tq, gw), lambda b_, g, i: (b_, i, g)),
                  pl.BlockSpec((None, None, nc, LANES), lambda b_, g, i: (b_, g, 0, 0)),
                  pl.BlockSpec((None, None, nc, 2 * LANES), lambda b_, g, i: (b_, g, 0, 0)),
                  k_spec, v_spec, k_spec, v_spec,
                  pl.BlockSpec((None, tq, LANES), lambda b_, g, i: (b_, i, g)),
                  _const_spec(ov.shape),
                  pl.BlockSpec((None, tq, nk), lambda b_, g, i: (jnp.minimum(i, n_early), 0, 0)),
                  pl.BlockSpec((None, tq, tk), lambda b_, g, i: (i % n_var, 0, 0))],
        out_specs=pl.BlockSpec((None, tq, 2 * LANES), lambda b_, g, i: (b_, i, g)),
        out_shape=jax.ShapeDtypeStruct((b, s, NSA_KV_HEADS * 2 * LANES), F32),
        scratch_shapes=[pltpu.VMEM((NSA_GQA * tq, LANES), F32), pltpu.VMEM((NSA_GQA * tq, LANES), F32)],
        compiler_params=_cparams(3), name="nsa_attention",
    )(qa, kc, vc, kw, vw, ks, vs, gates, ov, wmask, dmask)


def _mem_kv_kernel(mem_ref, gain_ref, w_ref, gt_ref, ones_ref, k_ref, v_ref):
    hn = _rms_rows(mem_ref[...], gain_ref[...]).astype(BF16)
    y = _dot(hn, w_ref[...])
    ms = _head_mean_sq(y[:, :MEM_W], ones_ref)
    for p in range(MEM_HEADS // 2):
        sl = slice(p * LANES, (p + 1) * LANES)
        z = _pair_norm(y[:, sl], ms[:, sl], gt_ref[...], None)
        _store_head_slots(k_ref, z, MEM_HEADS, p, 0.0)
        c = MEM_W + p * LANES
        _store_value_slots(v_ref, y[:, c:c + LANES], MEM_HEADS, p, False)


def _mem_kv(mem, gains, w, gt):
    b, nm, _ = mem.shape
    nl = gains.shape[0]
    width = MEM_HEADS * LANES
    out_spec = pl.BlockSpec((None, None, nm, width), lambda l, b_: (l, b_, 0, 0))
    return pl.pallas_call(
        _mem_kv_kernel,
        grid=(nl, b),
        in_specs=[pl.BlockSpec((None, nm, D_MODEL), lambda l, b_: (b_, 0, 0)),
                  pl.BlockSpec((None, 1, D_MODEL), lambda l, b_: (l, 0, 0)),
                  pl.BlockSpec((None, D_MODEL, 2 * MEM_W), lambda l, b_: (l, 0, 0)),
                  pl.BlockSpec((None, 1, LANES), lambda l, b_: (l, 0, 0)),
                  _const_spec((2 * LANES, 2 * LANES))],
        out_specs=[out_spec, out_spec],
        out_shape=[jax.ShapeDtypeStruct((nl, b, nm, width), BF16)] * 2,
        compiler_params=_cparams(2), name="mem_kv",
    )(mem, gains, w, gt, _head_ones())


def _mem_attn_kernel(q_ref, k_ref, v_ref, o_ref, *, tq):
    lo = _lane_lo(tq)
    accs = []
    for h in range(MEM_HEADS):
        sl = slice(h * LANES, (h + 1) * LANES)
        s = _dot_nt(q_ref[:, sl], k_ref[:, sl])
        m = jnp.max(s, axis=-1, keepdims=True)
        accs.append(_dot(jnp.exp2(s - m).astype(BF16), v_ref[:, sl]))
    for pair in range(MEM_HEADS // 2):
        num, den = _pair_num_den(accs[2 * pair], accs[2 * pair + 1], lo)
        o_ref[:, pair * LANES:(pair + 1) * LANES] = num / den


def _mem_attention(qm, mk, mv, layer):
    b, s, width = qm.shape
    nm = mk.shape[2]
    tq = min(MEM_Q_TILE, s)
    kv_spec = pl.BlockSpec((None, None, nm, width), lambda b_, i: (layer, b_, 0, 0))
    return pl.pallas_call(
        functools.partial(_mem_attn_kernel, tq=tq),
        grid=(b, s // tq),
        in_specs=[pl.BlockSpec((None, tq, width), lambda b_, i: (b_, i, 0)), kv_spec, kv_spec],
        out_specs=pl.BlockSpec((None, tq, MEM_W), lambda b_, i: (b_, i, 0)),
        out_shape=jax.ShapeDtypeStruct((b, s, MEM_W), F32),
        compiler_params=_cparams(2), name="mem_attention",
    )(qm, mk, mv)


def _mlp_tail(x1, gain_ref, wup_ref, wdn_ref, out_ref):
    hn = _rms_rows(x1, gain_ref[...]).astype(BF16)
    out_ref[...] = x1
    for c in range(D_FF // FF_CHUNK):
        u = _dot(hn, wup_ref[:, c * FF_CHUNK:(c + 1) * FF_CHUNK])
        u = jnp.square(jnp.maximum(u, 0.0)).astype(BF16)
        out_ref[...] += _dot(u, wdn_ref[c * FF_CHUNK:(c + 1) * FF_CHUNK, :])


def _a_out_kernel(oa_ref, om_ref, x_ref, wo_ref, gain_ref, wup_ref, wdn_ref, out_ref):
    x1 = (x_ref[...] + _dot(oa_ref[...].astype(BF16), wo_ref[:A_Q, :])
          + _dot(om_ref[...].astype(BF16), wo_ref[A_Q:, :]))
    _mlp_tail(x1, gain_ref, wup_ref, wdn_ref, out_ref)


def _b_out_kernel(od_ref, om_ref, x_ref, wo_ref, gain_ref, wup_ref, wdn_ref, out_ref):
    kd = DIL_SLOTS * HEAD_DIM
    x1 = (x_ref[...] + _dot(od_ref[...].astype(BF16), wo_ref[:kd, :])
          + _dot(om_ref[...].astype(BF16), wo_ref[kd:, :]))
    _mlp_tail(x1, gain_ref, wup_ref, wdn_ref, out_ref)


def _out_mlp(kern, acts, x2d, wo, gain, wup, wdn, name):
    n = x2d.shape[0]
    tm = ROW_TILE
    row_spec = lambda width: pl.BlockSpec((tm, width), lambda i: (i, 0))
    return pl.pallas_call(
        kern,
        grid=(n // tm,),
        in_specs=[row_spec(a.shape[1]) for a in acts] + [row_spec(D_MODEL), _const_spec(wo.shape),
                  _const_spec((1, D_MODEL)), _const_spec(wup.shape), _const_spec(wdn.shape)],
        out_specs=row_spec(D_MODEL),
        out_shape=jax.ShapeDtypeStruct((n, D_MODEL), F32),
        compiler_params=_cparams(1), name=name,
    )(*acts, x2d, wo, gain, wup, wdn)


def _b_proj_kernel(x_ref, gains_ref, wq_ref, wkv_ref, cos_ref, sin_ref, gt_ref, ones_ref,
                   q0_ref, q1_ref, q2_ref, qm_ref, k_ref, v_ref, *, tm):
    x = x_ref[...]
    xn = x * lax.rsqrt(jnp.mean(x * x, axis=-1, keepdims=True) + EPS)
    rope = _rope_args(cos_ref, sin_ref, tm)
    hq = (xn * gains_ref[0:1, :]).astype(BF16)
    width = DIL_SLOTS * HEAD_DIM
    for gi, ref in enumerate((q0_ref, q1_ref, q2_ref)):
        y = _dot(hq, wq_ref[:, gi * width:(gi + 1) * width])
        ms = _head_mean_sq(y, ones_ref)
        for p in range(DIL_SLOTS // 2):
            sl = slice(p * LANES, (p + 1) * LANES)
            z = _pair_norm(y[:, sl], ms[:, sl], gt_ref[gi:gi + 1, :], rope) * Q_SCALE
            _store_head_slots(ref, z, DIL_SLOTS, p, 0.0)
    y = _dot(hq, wq_ref[:, B_Q:B_Q + MEM_W])
    ms = _head_mean_sq(y, ones_ref)
    for p in range(MEM_HEADS // 2):
        sl = slice(p * LANES, (p + 1) * LANES)
        z = _pair_norm(y[:, sl], ms[:, sl], gt_ref[3:4, :], None) * Q_SCALE
        _store_head_slots(qm_ref, z, MEM_HEADS, p, 0.0)
    hk = (xn * gains_ref[1:2, :]).astype(BF16)
    y = _dot(hk, wkv_ref[...])
    ms = _head_mean_sq(y[:, :width], ones_ref)
    for p in range(DIL_SLOTS // 2):
        sl = slice(p * LANES, (p + 1) * LANES)
        z = _pair_norm(y[:, sl], ms[:, sl], gt_ref[4:5, :], rope)
        _store_head_slots(k_ref, z, DIL_SLOTS, p, 0.0)
        c = width + p * LANES
        _store_value_slots(v_ref, y[:, c:c + LANES], DIL_SLOTS, p, False)


def _b_proj(x2d, gains, wq, wkv, cos, sin, gt, seq):
    n = x2d.shape[0]
    tm = ROW_TILE
    nseq = seq // tm
    row_spec = lambda width: pl.BlockSpec((tm, width), lambda i: (i, 0))
    tab_spec = pl.BlockSpec((tm, LANES), lambda i: (i % nseq, 0))
    width = DIL_SLOTS * LANES
    widths = (width, width, width, MEM_HEADS * LANES, width, width)
    return pl.pallas_call(
        functools.partial(_b_proj_kernel, tm=tm),
        grid=(n // tm,),
        in_specs=[row_spec(D_MODEL), _const_spec(gains.shape), _const_spec(wq.shape),
                  _const_spec(wkv.shape), tab_spec, tab_spec, _const_spec(gt.shape),
                  _const_spec((2 * LANES, 2 * LANES))],
        out_specs=[row_spec(wd) for wd in widths],
        out_shape=[jax.ShapeDtypeStruct((n, wd), BF16) for wd in widths],
        compiler_params=_cparams(1), name="b_proj",
    )(x2d, gains, wq, wkv, cos, sin, gt, _head_ones())


def _dil_kernel(q0_ref, q1_ref, q2_ref, k_ref, v_ref, o_ref, qf_ref, kf_ref, vf_ref, lse_ref, *, seq):
    tq = DIL_Q_TILE
    heads = (slice(0, LANES), slice(LANES, 2 * LANES))
    for hd, sl in enumerate(heads):
        kf_ref[hd] = k_ref[:, sl].astype(F32)
        vf_ref[hd] = v_ref[:, sl].astype(F32)

    def attend(qs, ks, vs, qpos0, kpos0):
        nq, nk = qs[0].shape[0], ks[0].shape[0]
        lo = _lane_lo(nq)
        dist = (qpos0 + lax.broadcasted_iota(jnp.int32, (nq, 1), 0)
                - (kpos0 + lax.broadcasted_iota(jnp.int32, (1, nk), 1)))
        mask = (dist >= 0) & (dist <= DIL_BAND)
        accs, ms = [], []
        for q, k, v in zip(qs, ks, vs):
            s = jnp.where(mask, _dot_nt(q, k), NEG)
            m = jnp.max(s, axis=-1, keepdims=True)
            accs.append(_dot(jnp.exp2(s - m).astype(BF16), v))
            ms.append(jnp.broadcast_to(m, (nq, LANES)))
        num, den = _pair_num_den(accs[0], accs[1], lo)
        return num / den, jnp.where(lo, ms[0], ms[1]) + jnp.log2(den)

    def merge(o_old, l_old, o_new, l_new):
        mx = jnp.maximum(l_old, l_new)
        a = jnp.exp2(l_old - mx)
        b = jnp.exp2(l_new - mx)
        den = a + b
        return (a * o_old + b * o_new) / den, mx + jnp.log2(den)

    nk = min(tq + DIL_BAND, seq)

    def body0(i, c):
        q0 = pl.multiple_of(i * tq, tq)
        k0 = pl.multiple_of(jnp.clip(i * tq - DIL_BAND, 0, seq - nk), DIL_BAND)
        o, l = attend([q0_ref[pl.ds(q0, tq), sl] for sl in heads],
                      [k_ref[pl.ds(k0, nk), sl] for sl in heads],
                      [v_ref[pl.ds(k0, nk), sl] for sl in heads], q0, k0)
        o_ref[pl.ds(q0, tq), :] = o
        lse_ref[pl.ds(q0, tq), :] = l
        return c

    lax.fori_loop(0, seq // tq, body0, 0, unroll=DIL_UNROLL)

    for q_ref, (_, dil) in zip((q1_ref, q2_ref), DIL_PATTERNS[1:]):
        for hd, sl in enumerate(heads):
            qf_ref[hd] = q_ref[:, sl].astype(F32)
        length = seq // dil
        tqd = min(tq if dil < 2 * SUBLANES else DIL_Q_TILE_WIDE, length)
        nkd = min(tqd + DIL_BAND, length)
        ntile = length // tqd

        def body(it, c, dil=dil, length=length, tqd=tqd, nkd=nkd, ntile=ntile):
            r = it // ntile
            i = it - r * ntile
            qp = i * tqd
            kp = jnp.clip(qp - DIL_BAND, 0, length - nkd)
            qrows = pl.ds(r + dil * qp, tqd, stride=dil)
            krows = pl.ds(r + dil * kp, nkd, stride=dil)
            o, l = attend([qf_ref[hd, qrows, :].astype(BF16) for hd in range(2)],
                          [kf_ref[hd, krows, :].astype(BF16) for hd in range(2)],
                          [vf_ref[hd, krows, :].astype(BF16) for hd in range(2)], qp, kp)
            o, l = merge(o_ref[qrows, :], lse_ref[qrows, :], o, l)
            o_ref[qrows, :] = o
            lse_ref[qrows, :] = l
            return c

        lax.fori_loop(0, dil * ntile, body, 0, unroll=DIL_UNROLL)


def _dil_attention(q0, q1, q2, k, v):
    b, s, _ = k.shape
    npair = DIL_SLOTS // 2
    pw = 2 * LANES
    in_spec = pl.BlockSpec((None, s, pw), lambda b_, hp: (b_, 0, hp))
    return pl.pallas_call(
        functools.partial(_dil_kernel, seq=s),
        grid=(b, npair),
        in_specs=[in_spec] * 5,
        out_specs=pl.BlockSpec((None, s, LANES), lambda b_, hp: (b_, 0, hp)),
        out_shape=jax.ShapeDtypeStruct((b, s, DIL_SLOTS * HEAD_DIM), F32),
        scratch_shapes=[pltpu.VMEM((2, s, LANES), F32)] * 3 + [pltpu.VMEM((s, LANES), F32)],
        compiler_params=_cparams(2), name="dil_attention",
    )(q0, q1, q2, k, v)


def kernel(x, mem, attn_norm, mlp_norm, w_up, w_down, mem_norm, w_mem_kv, mem_q_norm, mem_k_norm,
           a_w_in, a_w_out, a_q_norm, a_k_norm, a_cmp_pos, a_cmp_w1, a_cmp_b1, a_cmp_w2, a_cmp_b2,
           kv_norm, w_kv_shared, kv_k_norm, b_w_in, b_w_out, b_q_norm):
    b, s, _ = x.shape
    n = b * s
    assert s % ROW_TILE == 0 and s % SEL_KEY_TILE == 0 and s // SEL_BLOCK <= HEAD_DIM
    assert s >= WINDOW + Q_TILE and (s // DIL_PATTERNS[-1][1]) % DIL_BAND == 0
    x2d = x.reshape(n, D_MODEL)
    cos, sin = _rope_pair_tables(jnp.arange(s))

    mk, mv = _mem_kv(mem, mem_norm[:, None, :], w_mem_kv.astype(BF16), _gain_pair(mem_k_norm)[:, None, :])

    gt_a = jnp.zeros((SUBLANES, LANES), F32)
    gt_a = gt_a.at[0].set(_gain_pair(a_q_norm[0])).at[1].set(_gain_pair(a_k_norm[0, 1]))
    gt_a = gt_a.at[2].set(_gain_pair(a_k_norm[0, 2])).at[3].set(_gain_pair(mem_q_norm[0]))
    qa, ks, kw, vs, vw, kcvc, qm, gates = _a_proj(
        x2d, attn_norm[0:1], _prep_a_w_in(a_w_in[0]), cos, sin, gt_a, s)

    n_cmp = (s - CMP_BLOCK) // CMP_STRIDE + 1
    nc = s // CMP_STRIDE
    tab_c = _rope_table(jnp.arange(nc) * CMP_STRIDE + (CMP_BLOCK - 1))
    kc, vc = _compress(kcvc.reshape(b, s, -1),
                       *_prep_compress(a_cmp_pos[0], a_cmp_w1[0], a_cmp_b1[0], a_cmp_w2[0], a_cmp_b2[0]),
                       tab_c, _gain_slot(a_k_norm[0, 0], "rot")[None])

    n_blk = s // SEL_BLOCK
    c_start = np.arange(nc)[None, :] * CMP_STRIDE
    b_start = np.arange(n_blk)[:, None] * SEL_BLOCK
    overlap_t = ((c_start < b_start + SEL_BLOCK) & (c_start + CMP_BLOCK > b_start)
                 & (np.arange(nc)[None, :] < n_cmp))
    ov = jnp.asarray(overlap_t, BF16)

    qa3 = qa.reshape(b, s, -1)
    gates3 = gates.reshape(b, s, -1)
    o_a = _nsa_attention(qa3, kc, vc, kw.reshape(b, s, -1), vw.reshape(b, s, -1), ks.reshape(b, s, -1),
                         vs.reshape(b, s, -1), gates3, ov, n_cmp, min(SEL_TOPK, n_blk))
    o_m = _mem_attention(qm.reshape(b, s, -1), mk, mv, 0)
    x2d = _out_mlp(_a_out_kernel, [o_a.reshape(n, -1), o_m.reshape(n, -1)],
                   x2d, a_w_out[0].astype(BF16), mlp_norm[0:1], w_up[0].astype(BF16),
                   w_down[0].astype(BF16), "a_out_mlp")

    gt_b = jnp.zeros((SUBLANES, LANES), F32)
    for gi in range(N_DIL_GROUPS):
        gt_b = gt_b.at[gi].set(_gain_pair(b_q_norm[0, gi]))
    gt_b = gt_b.at[3].set(_gain_pair(mem_q_norm[1])).at[4].set(_gain_pair(kv_k_norm))
    q0, q1, q2, qm1, kb, vb = _b_proj(x2d, jnp.stack([attn_norm[1], kv_norm]), b_w_in[0].astype(BF16),
                                      w_kv_shared.astype(BF16), cos, sin, gt_b, s)
    o_d = _dil_attention(*(t.reshape(b, s, -1) for t in (q0, q1, q2, kb, vb)))
    o_m = _mem_attention(qm1.reshape(b, s, -1), mk, mv, 1)
    x2d = _out_mlp(_b_out_kernel, [o_d.reshape(n, -1), o_m.reshape(n, -1)], x2d, b_w_out[0].astype(BF16),
                   mlp_norm[1:2], w_up[1].astype(BF16), w_down[1].astype(BF16), "b_out_mlp")
    return x2d.reshape(b, s, D_MODEL)
```

```python
import functools

import numpy as np
import jax
import jax.numpy as jnp
from jax import lax
from jax.experimental import pallas as pl
from jax.experimental.pallas import tpu as pltpu

D_MODEL = 1024
HEAD_DIM = 64
HALF = HEAD_DIM // 2
ROPE_THETA = 10000.0
EPS = 1e-6
NEG = -1e30
D_FF = 4 * D_MODEL
MEM_HEADS = 4
NSA_HEADS = 12
NSA_KV_HEADS = 3
NSA_GQA = NSA_HEADS // NSA_KV_HEADS
CMP_BLOCK = 32
CMP_STRIDE = 16
CMP_HIDDEN = 256
SEL_BLOCK = 64
SEL_TOPK = 16
WINDOW = 512
SEL_FORCE = 1e9
DIL_PATTERNS = ((128, 1), (512, 4), (2048, 16))
N_DIL_GROUPS = 3
DIL_SLOTS = 8
MEM_W = MEM_HEADS * HEAD_DIM
A_Q = NSA_HEADS * HEAD_DIM
A_KV = NSA_KV_HEADS * HEAD_DIM
B_Q = N_DIL_GROUPS * DIL_SLOTS * HEAD_DIM
LOG2E = 1.4426950408889634
Q_SCALE = HEAD_DIM ** -0.5 * LOG2E

LANES = 128
SUBLANES = 8
VMEM_LIMIT = 56 * 1024 * 1024
ROW_TILE = 512
PROJ_ROW_TILE = 1024
Q_TILE = 256
DIL_UNROLL = 4
SEL_KEY_TILE = 512
DIL_Q_TILE = 128
DIL_Q_TILE_WIDE = 256
DIL_BAND = 128
FF_CHUNK = 1024

BF16 = jnp.bfloat16
F32 = jnp.float32


def _cparams(n_grid):
    return pltpu.CompilerParams(dimension_semantics=("arbitrary",) * n_grid,
                                vmem_limit_bytes=VMEM_LIMIT)


def _const_spec(shape):
    nd = len(shape)
    return pl.BlockSpec(shape, lambda *_: (0,) * nd, pipeline_mode=pl.Buffered(1))


def _dot(a, b):
    return jnp.dot(a, b, preferred_element_type=F32)


def _dot_nt(a, b):
    return lax.dot_general(a, b, (((1,), (1,)), ((), ())), preferred_element_type=F32)


def _rms_rows(x, gain):
    return x * lax.rsqrt(jnp.mean(x * x, axis=-1, keepdims=True) + EPS) * gain


def _slot_norm(y, gt, tab):
    z = y * lax.rsqrt(jnp.mean(y * y, axis=-1, keepdims=True) + EPS) * gt
    if tab is not None:
        z = z * tab
        z = z + pltpu.roll(z, HEAD_DIM, 1)
    return z


def _head_mean_sq(y, ones_ref):
    sq = (y * y).astype(BF16)
    width = y.shape[1]
    step = ones_ref.shape[0]
    parts = []
    for c in range(0, width, step):
        cw = min(step, width - c)
        parts.append(_dot(sq[:, c:c + cw], ones_ref[:cw, :cw]))
    return jnp.concatenate(parts, axis=1) * (1.0 / HEAD_DIM)


def _pair_norm(y, ms, gain, rope):
    z = y * lax.rsqrt(ms + EPS) * gain
    if rope is not None:
        cos, sin_signed, first = rope
        partner = jnp.where(first, pltpu.roll(z, LANES - HALF, 1), pltpu.roll(z, HALF, 1))
        z = z * cos + partner * sin_signed
    return z


def _lane_lo(rows):
    return lax.broadcasted_iota(jnp.int32, (rows, LANES), 1) < HEAD_DIM


def _swap_halves(x):
    return pltpu.roll(x, HEAD_DIM, 1)


def _pair_num_den(acc_even, acc_odd, lo):
    return jnp.where(lo, acc_even, acc_odd), _swap_halves(jnp.where(lo, acc_odd, acc_even))


def _rot_half_cols(w):
    return jnp.concatenate([-w[..., HALF:], w[..., :HALF]], axis=-1)


def _swap_half(g):
    return jnp.concatenate([g[..., HALF:], g[..., :HALF]], axis=-1)


def _slots(w, n, kind):
    k = w.shape[0]
    w = w.reshape(k, n, HEAD_DIM)
    other = _rot_half_cols(w) if kind == "rot" else w
    return jnp.concatenate([w, other], axis=-1).reshape(k, n * LANES)


def _gain_slot(g, kind):
    other = _swap_half(g) if kind == "rot" else g
    return jnp.concatenate([g, other], axis=-1)


def _gain_pair(g):
    return jnp.concatenate([g, g], axis=-1)


def _rope_cos_sin(pos):
    inv_freq = ROPE_THETA ** (-jnp.arange(HALF, dtype=F32) / HALF)
    ang = jnp.asarray(pos, F32)[:, None] * inv_freq[None, :]
    return jnp.cos(ang), jnp.sin(ang)


def _rope_table(pos):
    cos, sin = _rope_cos_sin(pos)
    return jnp.concatenate([cos, cos, sin, sin], axis=-1)


def _rope_pair_tables(pos):
    cos, sin = _rope_cos_sin(pos)
    return (jnp.concatenate([cos, cos, cos, cos], axis=-1),
            jnp.concatenate([-sin, sin, -sin, sin], axis=-1))


def _pad_cols(w, width):
    return jnp.pad(w, ((0, 0), (0, width - w.shape[1])))


KV_PAIRS = (NSA_KV_HEADS + 1) // 2
A_COL_Q = 0
A_COL_KS = A_COL_Q + A_Q
A_COL_KW = A_COL_KS + KV_PAIRS * LANES
A_COL_VS = A_COL_KW + KV_PAIRS * LANES
A_COL_VW = A_COL_VS + KV_PAIRS * LANES
A_COL_KCVC = A_COL_VW + KV_PAIRS * LANES
A_COL_QM = A_COL_KCVC + NSA_KV_HEADS * LANES
A_COL_G = A_COL_QM + MEM_W
A_COLS = A_COL_G + NSA_KV_HEADS * LANES


def _prep_a_w_in(w):
    o = 0
    q = w[:, o:o + A_Q]; o += A_Q
    kc = w[:, o:o + A_KV]; o += A_KV
    vc = w[:, o:o + A_KV]; o += A_KV
    ks = w[:, o:o + A_KV]; o += A_KV
    vs = w[:, o:o + A_KV]; o += A_KV
    kw = w[:, o:o + A_KV]; o += A_KV
    vw = w[:, o:o + A_KV]; o += A_KV
    qm = w[:, o:o + MEM_W]; o += MEM_W
    gl = w[:, o:]
    gl = gl.reshape(D_MODEL, NSA_KV_HEADS, NSA_GQA, 3).transpose(0, 1, 3, 2)
    gl = gl.reshape(D_MODEL, NSA_KV_HEADS, 3 * NSA_GQA)
    gl = jnp.pad(gl, ((0, 0), (0, 0), (0, LANES - 3 * NSA_GQA))).reshape(D_MODEL, NSA_KV_HEADS * LANES)
    kcvc = jnp.concatenate([kc.reshape(D_MODEL, NSA_KV_HEADS, HEAD_DIM),
                            vc.reshape(D_MODEL, NSA_KV_HEADS, HEAD_DIM)], axis=-1)
    kvw = KV_PAIRS * LANES
    cols = [q, _pad_cols(ks, kvw), _pad_cols(kw, kvw), _pad_cols(vs, kvw), _pad_cols(vw, kvw),
            kcvc.reshape(D_MODEL, NSA_KV_HEADS * LANES), qm, gl]
    return jnp.concatenate(cols, axis=1).astype(BF16)


def _store_head_slots(ref, z, n_heads, pair, aux):
    lo = _lane_lo(z.shape[0])
    for half, data in enumerate((z, _swap_halves(z))):
        h = 2 * pair + half
        if h < n_heads:
            ref[:, h * LANES:(h + 1) * LANES] = jnp.where(lo, data, aux).astype(ref.dtype)


def _store_value_slots(ref, y, n_heads, pair, both):
    lo = _lane_lo(y.shape[0])
    ys = _swap_halves(y)
    for half in range(2):
        h = 2 * pair + half
        if h >= n_heads:
            continue
        even = jnp.where(lo, y if half == 0 else ys, 1.0)
        odd = jnp.where(lo, 1.0, ys if half == 0 else y)
        if both:
            ref[:, 2 * h * LANES:(2 * h + 1) * LANES] = even.astype(ref.dtype)
            ref[:, (2 * h + 1) * LANES:(2 * h + 2) * LANES] = odd.astype(ref.dtype)
        else:
            ref[:, h * LANES:(h + 1) * LANES] = (even if h % 2 == 0 else odd).astype(ref.dtype)


def _rope_args(cos_ref, sin_ref, rows):
    lane = lax.broadcasted_iota(jnp.int32, (rows, LANES), 1)
    return cos_ref[...], sin_ref[...], (lane & HALF) == 0


def _a_proj_kernel(x_ref, gain_ref, w_ref, cos_ref, sin_ref, gt_ref, ones_ref,
                   qa_ref, ks_ref, kw_ref, vs_ref, vw_ref, kcvc_ref, qm_ref, g_ref, *, seq, tm):
    hn = _rms_rows(x_ref[...], gain_ref[...]).astype(BF16)
    rope = _rope_args(cos_ref, sin_ref, tm)
    lane = lax.broadcasted_iota(jnp.int32, (tm, LANES), 1)
    row = lax.broadcasted_iota(jnp.int32, (tm, LANES), 0)
    tok = (pl.program_id(0) % (seq // tm)) * tm + row
    blk_ind = jnp.where(lane - HEAD_DIM == tok // SEL_BLOCK, 1.0, 0.0)

    yq = _dot(hn, w_ref[:, A_COL_Q:A_COL_KS])
    ms = _head_mean_sq(yq, ones_ref)
    for p in range(NSA_HEADS // 2):
        sl = slice(p * LANES, (p + 1) * LANES)
        z = _pair_norm(yq[:, sl], ms[:, sl], gt_ref[0:1, :], rope) * Q_SCALE
        _store_head_slots(qa_ref, z, NSA_HEADS, p, 0.0)

    yk = _dot(hn, w_ref[:, A_COL_KS:A_COL_VS])
    ms = _head_mean_sq(yk, ones_ref)
    for p in range(KV_PAIRS):
        sl = slice(p * LANES, (p + 1) * LANES)
        z = _pair_norm(yk[:, sl], ms[:, sl], gt_ref[1:2, :], rope)
        _store_head_slots(ks_ref, z, NSA_KV_HEADS, p, blk_ind)
        sl = slice((KV_PAIRS + p) * LANES, (KV_PAIRS + p + 1) * LANES)
        z = _pair_norm(yk[:, sl], ms[:, sl], gt_ref[2:3, :], rope)
        _store_head_slots(kw_ref, z, NSA_KV_HEADS, p, 0.0)

    yv = _dot(hn, w_ref[:, A_COL_VS:A_COL_KCVC])
    for j, ref in enumerate((vs_ref, vw_ref)):
        for p in range(KV_PAIRS):
            c = (j * KV_PAIRS + p) * LANES
            _store_value_slots(ref, yv[:, c:c + LANES], NSA_KV_HEADS, p, True)

    yr = _dot(hn, w_ref[:, A_COL_KCVC:A_COLS])
    kcvc_ref[...] = yr[:, :NSA_KV_HEADS * LANES]
    c = A_COL_QM - A_COL_KCVC
    ym = yr[:, c:c + MEM_W]
    ms = _head_mean_sq(ym, ones_ref)
    for p in range(MEM_HEADS // 2):
        sl = slice(p * LANES, (p + 1) * LANES)
        z = _pair_norm(ym[:, sl], ms[:, sl], gt_ref[3:4, :], None) * Q_SCALE
        _store_head_slots(qm_ref, z, MEM_HEADS, p, 0.0)
    c = A_COL_G - A_COL_KCVC
    g_ref[...] = jax.nn.sigmoid(yr[:, c:c + NSA_KV_HEADS * LANES])


def _head_ones():
    idx = np.arange(2 * LANES) // HEAD_DIM
    return jnp.asarray(idx[:, None] == idx[None, :], BF16)


def _a_proj(x2d, gain, w, cos, sin, gt, seq):
    n = x2d.shape[0]
    tm = PROJ_ROW_TILE
    nseq = seq // tm
    row_spec = lambda width: pl.BlockSpec((tm, width), lambda i: (i, 0))
    tab_spec = pl.BlockSpec((tm, LANES), lambda i: (i % nseq, 0))
    kvw = NSA_KV_HEADS * LANES
    widths = (NSA_HEADS * LANES, kvw, kvw, 2 * kvw, 2 * kvw, kvw, MEM_HEADS * LANES, kvw)
    dtypes = (BF16, BF16, BF16, BF16, BF16, F32, BF16, F32)
    return pl.pallas_call(
        functools.partial(_a_proj_kernel, seq=seq, tm=tm),
        grid=(n // tm,),
        in_specs=[row_spec(D_MODEL), _const_spec((1, D_MODEL)), _const_spec((D_MODEL, A_COLS)),
                  tab_spec, tab_spec, _const_spec((SUBLANES, LANES)), _const_spec((2 * LANES, 2 * LANES))],
        out_specs=[row_spec(wd) for wd in widths],
        out_shape=[jax.ShapeDtypeStruct((n, wd), dt) for wd, dt in zip(widths, dtypes)],
        compiler_params=_cparams(1), name="a_proj",
    )(x2d, gain, w, cos, sin, gt, _head_ones())


def _compress_kernel(x_ref, pos_ref, w1_ref, b1_ref, w2_ref, b2_ref, tab_ref, gt_ref, kc_ref, vc_ref):
    nc = kc_ref.shape[0]
    lo = _lane_lo(nc)
    h = jnp.concatenate([x_ref[pl.ds(j, nc, stride=CMP_STRIDE), :] for j in range(CMP_STRIDE)], axis=1)
    top = _dot((h + pos_ref[0:1, :]).astype(BF16), w1_ref[0])
    bot = _dot((h + pos_ref[1:2, :]).astype(BF16), w1_ref[1])
    pre = top + pltpu.roll(bot, nc - 1, 0) + b1_ref[...]
    hid = jax.nn.gelu(pre).astype(BF16)
    yk = _dot(hid[:, :CMP_HIDDEN], w2_ref[0]) + b2_ref[0:1, :]
    yv = _dot(hid[:, CMP_HIDDEN:], w2_ref[1]) + b2_ref[1:2, :]
    kc_ref[...] = jnp.where(lo, _slot_norm(yk, gt_ref[...], tab_ref[...]), 0.0).astype(BF16)
    vc_ref[:, :LANES] = jnp.where(lo, yv, 1.0).astype(BF16)
    vc_ref[:, LANES:] = jnp.where(lo, 1.0, yv).astype(BF16)


def _compress(kcvc, pos, w1, b1, w2, b2, tab_c, gt):
    b, s, _ = kcvc.shape
    nc = s // CMP_STRIDE
    ospec = lambda width: pl.BlockSpec((None, None, nc, width), lambda b_, g: (b_, g, 0, 0))
    return pl.pallas_call(
        _compress_kernel,
        grid=(b, NSA_KV_HEADS),
        in_specs=[pl.BlockSpec((None, s, LANES), lambda b_, g: (b_, 0, g)),
                  _const_spec(pos.shape), _const_spec(w1.shape), _const_spec(b1.shape),
                  _const_spec(w2.shape), _const_spec(b2.shape), _const_spec(tab_c.shape),
                  _const_spec(gt.shape)],
        out_specs=[ospec(LANES), ospec(2 * LANES)],
        out_shape=[jax.ShapeDtypeStruct((b, NSA_KV_HEADS, nc, LANES), BF16),
                   jax.ShapeDtypeStruct((b, NSA_KV_HEADS, nc, 2 * LANES), BF16)],
        compiler_params=_cparams(2), name="compress",
    )(kcvc, pos, w1, b1, w2, b2, tab_c, gt)


def _prep_compress(cmp_pos, cmp_w1, cmp_b1, cmp_w2, cmp_b2):
    p = cmp_pos.reshape(2, 2, CMP_STRIDE, HEAD_DIM).transpose(1, 2, 0, 3).reshape(2, CMP_STRIDE * LANES)
    w = cmp_w1.reshape(2, 2, CMP_STRIDE, HEAD_DIM, CMP_HIDDEN)
    z = jnp.zeros_like(w[0])
    wk = jnp.concatenate([w[0], z], axis=-1)
    wv = jnp.concatenate([z, w[1]], axis=-1)
    w1 = jnp.concatenate([wk, wv], axis=2).reshape(2, CMP_STRIDE * LANES, 2 * CMP_HIDDEN)
    b1 = jnp.concatenate([cmp_b1[0], cmp_b1[1]])[None]
    w2 = jnp.stack([_slots(cmp_w2[0], 1, "rot"), _slots(cmp_w2[1], 1, "dup")])
    b2 = jnp.stack([_slots(cmp_b2[0][None], 1, "rot")[0], _slots(cmp_b2[1][None], 1, "dup")[0]])
    return p, w1.astype(BF16), b1, w2.astype(BF16), b2


STACK_ORDER = (0, 2, 1, 3)


def _stack_q(q_ref, extra=None):
    parts = []
    for r in STACK_ORDER:
        q = q_ref[:, r * LANES:(r + 1) * LANES]
        parts.append(q if extra is None else q + extra)
    return jnp.concatenate(parts, axis=0)


def _gate_pairs(g_ref, branch, tq):
    lo = _lane_lo(tq)
    col = lambda r: jnp.broadcast_to(g_ref[:, branch * NSA_GQA + r:branch * NSA_GQA + r + 1], (tq, LANES))
    return jnp.concatenate([jnp.where(lo, col(0), col(1)), jnp.where(lo, col(2), col(3))], axis=0)


def _store_pairs(o_ref, out, tq):
    o_ref[:, :LANES] = out[:tq]
    o_ref[:, LANES:] = out[tq:]


def _add_tile_mask(s, mask, tq):
    nk = s.shape[1]
    return (s.reshape(NSA_GQA, tq, nk) + mask).reshape(NSA_GQA * tq, nk)


def _window_branch(i, q_ref, kw_ref, vw_ref, g_ref, wm_ref, tq, nk):
    k0 = pl.multiple_of(jnp.maximum(i * tq + tq - nk, 0), tq)
    s = _add_tile_mask(_dot_nt(_stack_q(q_ref), kw_ref[pl.ds(k0, nk), :]), wm_ref[...], tq)
    m = jnp.max(s, axis=-1, keepdims=True)
    p = jnp.exp2(s - m).astype(BF16)
    num, den = _pair_num_den(_dot(p[:2 * tq], vw_ref[pl.ds(k0, nk), :LANES]),
                             _dot(p[2 * tq:], vw_ref[pl.ds(k0, nk), LANES:]), _lane_lo(2 * tq))
    return num / den * _gate_pairs(g_ref, 2, tq)


def _compressed_branch(i, q_ref, kc_ref, vc_ref, g_ref, ov_ref, tq, n_cmp, n_blk, n_sel):
    lo = _lane_lo(tq)
    kc = kc_ref[...]
    nc = kc.shape[0]
    t_row = i * tq + lax.broadcasted_iota(jnp.int32, (1, tq), 1)
    c_col = lax.broadcasted_iota(jnp.int32, (nc, 1), 0)
    cend_col = jnp.where(c_col < n_cmp, c_col * CMP_STRIDE + (CMP_BLOCK - 1), jnp.int32(2 ** 30))
    mask_t = cend_col <= t_row
    psum = jnp.zeros((nc, tq), F32)
    xs = []
    for r in range(NSA_GQA):
        st = jnp.where(mask_t, _dot_nt(kc, q_ref[:, r * LANES:(r + 1) * LANES]), NEG)
        mt = jnp.max(st, axis=0, keepdims=True)
        pt = jnp.where(mask_t, jnp.exp2(st - mt), 0.0)
        lt = jnp.sum(pt, axis=0, keepdims=True)
        pt = pt * jnp.where(lt > 0.0, 1.0 / lt, 0.0)
        psum = psum + pt
        vsl = slice(0, LANES) if r % 2 == 0 else slice(LANES, 2 * LANES)
        xs.append(_dot(pt.T.astype(BF16), vc_ref[:, vsl]))
    out_c = jnp.concatenate([jnp.where(lo, xs[0], xs[1]), jnp.where(lo, xs[2], xs[3])], axis=0)
    out_c = out_c * _gate_pairs(g_ref, 0, tq)

    ps_hi = psum.astype(BF16)
    ps_lo = (psum - ps_hi.astype(F32)).astype(BF16)
    ov = ov_ref[...]
    imp = _dot(ov, ps_hi) + _dot(ov, ps_lo)

    cur = t_row // SEL_BLOCK
    ngrp = n_blk // SUBLANES
    jsub = lax.broadcasted_iota(jnp.int32, (SUBLANES, tq), 0)
    vals = []
    for a in range(ngrp):
        j = jsub + a * SUBLANES
        forced = (j == 0) | (j == cur) | (j == cur - 1)
        v = jnp.where(j <= cur, imp[a * SUBLANES:(a + 1) * SUBLANES], NEG)
        vals.append(jnp.where(forced, SEL_FORCE, v))
    ranks = [jnp.zeros((SUBLANES, tq), F32) for _ in range(ngrp)]
    for jj in range(n_blk):
        a0, s0 = divmod(jj, SUBLANES)
        rowb = jnp.broadcast_to(vals[a0][s0:s0 + 1, :], (SUBLANES, tq))
        for a in range(ngrp):
            if a > a0:
                beats = jnp.where(rowb >= vals[a], 1.0, 0.0)
            elif a < a0:
                beats = jnp.where(rowb > vals[a], 1.0, 0.0)
            else:
                beats = jnp.where(jsub > s0, jnp.where(rowb >= vals[a], 1.0, 0.0),
                                  jnp.where(rowb > vals[a], 1.0, 0.0))
            ranks[a] = ranks[a] + beats
    parts = [jnp.zeros((HEAD_DIM, tq), F32)]
    for a in range(ngrp):
        live = jnp.where(vals[a] > NEG / 2, 0.0, NEG)
        parts.append(jnp.where(ranks[a] < n_sel, live, NEG))
    if n_blk < HEAD_DIM:
        parts.append(jnp.zeros((HEAD_DIM - n_blk, tq), F32))
    bias_t = jnp.concatenate(parts, axis=0)
    return out_c, bias_t.T.astype(BF16)


def _band_masks(n_var, tq, nk, k0_of, lo_dist, hi_dist):
    r = np.arange(tq)[None, :, None]
    c = np.arange(nk)[None, None, :]
    v = np.arange(n_var)[:, None, None]
    dist = v * tq + r - (np.asarray([k0_of(x) for x in range(n_var)])[:, None, None] + c)
    return jnp.asarray(np.where((dist >= lo_dist) & (dist <= hi_dist), 0.0, NEG), F32)


def _selected_branch(i, q_ref, bias, k_ref, v_ref, g_ref, dm_ref, m_ref, acc_ref, tq, tk):
    rows = NSA_GQA * tq
    half = 2 * tq
    q4 = _stack_q(q_ref, bias)
    m_ref[...] = jnp.full((rows, LANES), NEG, F32)
    acc_ref[...] = jnp.zeros((rows, LANES), F32)
    rep = tk // LANES

    def step(kt, masked):
        k0 = pl.multiple_of(kt * tk, tk)
        s = _dot_nt(q4, k_ref[pl.ds(k0, tk), :])
        if masked:
            s = _add_tile_mask(s, dm_ref[...], tq)
        m_prev = m_ref[...]
        m_next = jnp.maximum(m_prev, jnp.max(s, axis=-1, keepdims=True))
        p = jnp.exp2(s - jnp.tile(m_next, (1, rep))).astype(BF16)
        alpha = jnp.exp2(m_prev - m_next)
        m_ref[...] = m_next
        acc_ref[:half] = acc_ref[:half] * alpha[:half] + _dot(p[:half], v_ref[pl.ds(k0, tk), :LANES])
        acc_ref[half:] = acc_ref[half:] * alpha[half:] + _dot(p[half:], v_ref[pl.ds(k0, tk), LANES:])

    kt_diag = (i * tq) // tk
    lax.fori_loop(0, kt_diag, lambda kt, c: (step(kt, False), c)[1], 0)
    step(kt_diag, True)
    num, den = _pair_num_den(acc_ref[:half], acc_ref[half:], _lane_lo(half))
    return num / den * _gate_pairs(g_ref, 1, tq)


def _nsa_kernel(q_ref, kc_ref, vc_ref, kw_ref, vw_ref, ks_ref, vs_ref, g_ref, ov_ref, wm_ref, dm_ref,
                o_ref, m_ref, acc_ref, *, tq, tk, nk, n_cmp, n_blk, n_sel):
    i = pl.program_id(2)
    out_w = _window_branch(i, q_ref, kw_ref, vw_ref, g_ref, wm_ref, tq, nk)
    out_c, bias = _compressed_branch(i, q_ref, kc_ref, vc_ref, g_ref, ov_ref, tq, n_cmp, n_blk, n_sel)
    _store_pairs(o_ref, out_c + out_w, tq)
    out_s = _selected_branch(i, q_ref, bias, ks_ref, vs_ref, g_ref, dm_ref, m_ref, acc_ref, tq, tk)
    o_ref[:, :LANES] += out_s[:tq]
    o_ref[:, LANES:] += out_s[tq:]


def _nsa_attention(qa, kc, vc, kw, vw, ks, vs, gates, ov, n_cmp, n_sel):
    b, s, _ = qa.shape
    tq, tk = Q_TILE, min(SEL_KEY_TILE, s)
    nk = min(WINDOW + tq, s)
    nc = kc.shape[2]
    n_blk = s // SEL_BLOCK
    gw = NSA_GQA * LANES
    n_early = (nk - tq) // tq
    wmask = _band_masks(n_early + 1, tq, nk, lambda v: max(v * tq + tq - nk, 0), 0, WINDOW - 1)
    n_var = tk // tq
    dmask = _band_masks(n_var, tq, tk, lambda v: 0, 0, tk)
    kern = functools.partial(_nsa_kernel, tq=tq, tk=tk, nk=nk, n_cmp=n_cmp, n_blk=n_blk, n_sel=n_sel)
    k_spec = pl.BlockSpec((None, s, LANES), lambda b_, g, i: (b_, 0, g))
    v_spec = pl.BlockSpec((None, s, 2 * LANES), lambda b_, g, i: (b_, 0, g))
    return pl.pallas_call(
        kern,
        grid=(b, NSA_KV_HEADS, s // tq),
        in_specs=[pl.BlockSpec((None, tq, gw), lambda b_, g, i: (b_, i, g)),
                  pl.BlockSpec((None, None, nc, LANES), lambda b_, g, i: (b_, g, 0, 0)),
                  pl.BlockSpec((None, None, nc, 2 * LANES), lambda b_, g, i: (b_, g, 0, 0)),
                  k_spec, v_spec, k_spec, v_spec,
                  pl.BlockSpec((None, tq, LANES), lambda b_, g, i: (b_, i, g)),
                  _const_spec(ov.shape),
                  pl.BlockSpec((None, tq, nk), lambda b_, g, i: (jnp.minimum(i, n_early), 0, 0)),
                  pl.BlockSpec((None, tq, tk), lambda b_, g, i: (i % n_var, 0, 0))],
        out_specs=pl.BlockSpec((None, tq, 2 * LANES), lambda b_, g, i: (b_, i, g)),
        out_shape=jax.ShapeDtypeStruct((b, s, NSA_KV_HEADS * 2 * LANES), F32),
        scratch_shapes=[pltpu.VMEM((NSA_GQA * tq, LANES), F32), pltpu.VMEM((NSA_GQA * tq, LANES), F32)],
        compiler_params=_cparams(3), name="nsa_attention",
    )(qa, kc, vc, kw, vw, ks, vs, gates, ov, wmask, dmask)


def _mem_kv_kernel(mem_ref, gain_ref, w_ref, gt_ref, ones_ref, k_ref, v_ref):
    hn = _rms_rows(mem_ref[...], gain_ref[...]).astype(BF16)
    y = _dot(hn, w_ref[...])
    ms = _head_mean_sq(y[:, :MEM_W], ones_ref)
    for p in range(MEM_HEADS // 2):
        sl = slice(p * LANES, (p + 1) * LANES)
        z = _pair_norm(y[:, sl], ms[:, sl], gt_ref[...], None)
        _store_head_slots(k_ref, z, MEM_HEADS, p, 0.0)
        c = MEM_W + p * LANES
        _store_value_slots(v_ref, y[:, c:c + LANES], MEM_HEADS, p, False)


def _mem_kv(mem, gains, w, gt):
    b, nm, _ = mem.shape
    nl = gains.shape[0]
    width = MEM_HEADS * LANES
    out_spec = pl.BlockSpec((None, None, nm, width), lambda l, b_: (l, b_, 0, 0))
    return pl.pallas_call(
        _mem_kv_kernel,
        grid=(nl, b),
        in_specs=[pl.BlockSpec((None, nm, D_MODEL), lambda l, b_: (b_, 0, 0)),
                  pl.BlockSpec((None, 1, D_MODEL), lambda l, b_: (l, 0, 0)),
                  pl.BlockSpec((None, D_MODEL, 2 * MEM_W), lambda l, b_: (l, 0, 0)),
                  pl.BlockSpec((None, 1, LANES), lambda l, b_: (l, 0, 0)),
                  _const_spec((2 * LANES, 2 * LANES))],
        out_specs=[out_spec, out_spec],
        out_shape=[jax.ShapeDtypeStruct((nl, b, nm, width), BF16)] * 2,
        compiler_params=_cparams(2), name="mem_kv",
    )(mem, gains, w, gt, _head_ones())


def _mem_attend(q_ref, k_ref, v_ref):
    lo = _lane_lo(q_ref.shape[0])
    accs = []
    for h in range(MEM_HEADS):
        sl = slice(h * LANES, (h + 1) * LANES)
        s = _dot_nt(q_ref[:, sl], k_ref[:, sl])
        m = jnp.max(s, axis=-1, keepdims=True)
        accs.append(_dot(jnp.exp2(s - m).astype(BF16), v_ref[:, sl]))
    outs = []
    for pair in range(MEM_HEADS // 2):
        num, den = _pair_num_den(accs[2 * pair], accs[2 * pair + 1], lo)
        outs.append(num / den)
    return jnp.concatenate(outs, axis=1)


def _out_mlp_kernel(o_ref, qm_ref, mk_ref, mv_ref, x_ref, wo_ref, gain_ref, wup_ref, wdn_ref, out_ref):
    km = o_ref.shape[1]
    o_mem = _mem_attend(qm_ref, mk_ref, mv_ref).astype(BF16)
    x1 = x_ref[...] + _dot(o_ref[...].astype(BF16), wo_ref[:km, :]) + _dot(o_mem, wo_ref[km:, :])
    hn = _rms_rows(x1, gain_ref[...]).astype(BF16)
    out_ref[...] = x1
    for c in range(D_FF // FF_CHUNK):
        u = _dot(hn, wup_ref[:, c * FF_CHUNK:(c + 1) * FF_CHUNK])
        u = jnp.square(jnp.maximum(u, 0.0)).astype(BF16)
        out_ref[...] += _dot(u, wdn_ref[c * FF_CHUNK:(c + 1) * FF_CHUNK, :])


def _out_mlp(o_main, qm, mk, mv, layer, seq, x2d, wo, gain, wup, wdn, name):
    n = x2d.shape[0]
    tm = ROW_TILE
    nseq = seq // tm
    nm, mw = mk.shape[2], mk.shape[3]
    row_spec = lambda width: pl.BlockSpec((tm, width), lambda i: (i, 0))
    kv_spec = pl.BlockSpec((None, None, nm, mw), lambda i: (layer, i // nseq, 0, 0))
    return pl.pallas_call(
        _out_mlp_kernel,
        grid=(n // tm,),
        in_specs=[row_spec(o_main.shape[1]), row_spec(qm.shape[1]), kv_spec, kv_spec, row_spec(D_MODEL),
                  _const_spec(wo.shape), _const_spec((1, D_MODEL)), _const_spec(wup.shape),
                  _const_spec(wdn.shape)],
        out_specs=row_spec(D_MODEL),
        out_shape=jax.ShapeDtypeStruct((n, D_MODEL), F32),
        compiler_params=_cparams(1), name=name,
    )(o_main, qm, mk, mv, x2d, wo, gain, wup, wdn)


def _b_proj_kernel(x_ref, gains_ref, wq_ref, wkv_ref, cos_ref, sin_ref, gt_ref, ones_ref,
                   q0_ref, q1_ref, q2_ref, qm_ref, k_ref, v_ref, *, tm):
    x = x_ref[...]
    xn = x * lax.rsqrt(jnp.mean(x * x, axis=-1, keepdims=True) + EPS)
    rope = _rope_args(cos_ref, sin_ref, tm)
    hq = (xn * gains_ref[0:1, :]).astype(BF16)
    width = DIL_SLOTS * HEAD_DIM
    for gi, ref in enumerate((q0_ref, q1_ref, q2_ref)):
        y = _dot(hq, wq_ref[:, gi * width:(gi + 1) * width])
        ms = _head_mean_sq(y, ones_ref)
        for p in range(DIL_SLOTS // 2):
            sl = slice(p * LANES, (p + 1) * LANES)
            z = _pair_norm(y[:, sl], ms[:, sl], gt_ref[gi:gi + 1, :], rope) * Q_SCALE
            _store_head_slots(ref, z, DIL_SLOTS, p, 0.0)
    y = _dot(hq, wq_ref[:, B_Q:B_Q + MEM_W])
    ms = _head_mean_sq(y, ones_ref)
    for p in range(MEM_HEADS // 2):
        sl = slice(p * LANES, (p + 1) * LANES)
        z = _pair_norm(y[:, sl], ms[:, sl], gt_ref[3:4, :], None) * Q_SCALE
        _store_head_slots(qm_ref, z, MEM_HEADS, p, 0.0)
    hk = (xn * gains_ref[1:2, :]).astype(BF16)
    y = _dot(hk, wkv_ref[...])
    ms = _head_mean_sq(y[:, :width], ones_ref)
    for p in range(DIL_SLOTS // 2):
        sl = slice(p * LANES, (p + 1) * LANES)
        z = _pair_norm(y[:, sl], ms[:, sl], gt_ref[4:5, :], rope)
        _store_head_slots(k_ref, z, DIL_SLOTS, p, 0.0)
        c = width + p * LANES
        _store_value_slots(v_ref, y[:, c:c + LANES], DIL_SLOTS, p, False)


def _b_proj(x2d, gains, wq, wkv, cos, sin, gt, seq):
    n = x2d.shape[0]
    tm = PROJ_ROW_TILE
    nseq = seq // tm
    row_spec = lambda width: pl.BlockSpec((tm, width), lambda i: (i, 0))
    tab_spec = pl.BlockSpec((tm, LANES), lambda i: (i % nseq, 0))
    width = DIL_SLOTS * LANES
    widths = (width, width, width, MEM_HEADS * LANES, width, width)
    return pl.pallas_call(
        functools.partial(_b_proj_kernel, tm=tm),
        grid=(n // tm,),
        in_specs=[row_spec(D_MODEL), _const_spec(gains.shape), _const_spec(wq.shape),
                  _const_spec(wkv.shape), tab_spec, tab_spec, _const_spec(gt.shape),
                  _const_spec((2 * LANES, 2 * LANES))],
        out_specs=[row_spec(wd) for wd in widths],
        out_shape=[jax.ShapeDtypeStruct((n, wd), BF16) for wd in widths],
        compiler_params=_cparams(1), name="b_proj",
    )(x2d, gains, wq, wkv, cos, sin, gt, _head_ones())


def _dil_kernel(q0_ref, q1_ref, q2_ref, k_ref, v_ref, o_ref, qf_ref, kf_ref, vf_ref, lse_ref, *, seq):
    tq = DIL_Q_TILE
    heads = (slice(0, LANES), slice(LANES, 2 * LANES))
    for hd, sl in enumerate(heads):
        kf_ref[hd] = k_ref[:, sl].astype(F32)
        vf_ref[hd] = v_ref[:, sl].astype(F32)

    def attend(qs, ks, vs, qpos0, kpos0):
        nq, nk = qs[0].shape[0], ks[0].shape[0]
        lo = _lane_lo(nq)
        dist = (qpos0 + lax.broadcasted_iota(jnp.int32, (nq, 1), 0)
                - (kpos0 + lax.broadcasted_iota(jnp.int32, (1, nk), 1)))
        mask = (dist >= 0) & (dist <= DIL_BAND)
        accs, ms = [], []
        for q, k, v in zip(qs, ks, vs):
            s = jnp.where(mask, _dot_nt(q, k), NEG)
            m = jnp.max(s, axis=-1, keepdims=True)
            accs.append(_dot(jnp.exp2(s - m).astype(BF16), v))
            ms.append(jnp.broadcast_to(m, (nq, LANES)))
        num, den = _pair_num_den(accs[0], accs[1], lo)
        return num / den, jnp.where(lo, ms[0], ms[1]) + jnp.log2(den)

    def merge(o_old, l_old, o_new, l_new):
        mx = jnp.maximum(l_old, l_new)
        a = jnp.exp2(l_old - mx)
        b = jnp.exp2(l_new - mx)
        den = a + b
        return (a * o_old + b * o_new) / den, mx + jnp.log2(den)

    nk = min(tq + DIL_BAND, seq)

    def body0(i, c):
        q0 = pl.multiple_of(i * tq, tq)
        k0 = pl.multiple_of(jnp.clip(i * tq - DIL_BAND, 0, seq - nk), DIL_BAND)
        o, l = attend([q0_ref[pl.ds(q0, tq), sl] for sl in heads],
                      [k_ref[pl.ds(k0, nk), sl] for sl in heads],
                      [v_ref[pl.ds(k0, nk), sl] for sl in heads], q0, k0)
        o_ref[pl.ds(q0, tq), :] = o
        lse_ref[pl.ds(q0, tq), :] = l
        return c

    lax.fori_loop(0, seq // tq, body0, 0, unroll=DIL_UNROLL)

    for q_ref, (_, dil) in zip((q1_ref, q2_ref), DIL_PATTERNS[1:]):
        for hd, sl in enumerate(heads):
            qf_ref[hd] = q_ref[:, sl].astype(F32)
        length = seq // dil
        tqd = min(tq if dil < 2 * SUBLANES else DIL_Q_TILE_WIDE, length)
        nkd = min(tqd + DIL_BAND, length)
        ntile = length // tqd

        def body(it, c, dil=dil, length=length, tqd=tqd, nkd=nkd, ntile=ntile):
            r = it // ntile
            i = it - r * ntile
            qp = i * tqd
            kp = jnp.clip(qp - DIL_BAND, 0, length - nkd)
            qrows = pl.ds(r + dil * qp, tqd, stride=dil)
            krows = pl.ds(r + dil * kp, nkd, stride=dil)
            o, l = attend([qf_ref[hd, qrows, :].astype(BF16) for hd in range(2)],
                          [kf_ref[hd, krows, :].astype(BF16) for hd in range(2)],
                          [vf_ref[hd, krows, :].astype(BF16) for hd in range(2)], qp, kp)
            o, l = merge(o_ref[qrows, :], lse_ref[qrows, :], o, l)
            o_ref[qrows, :] = o
            lse_ref[qrows, :] = l
            return c

        lax.fori_loop(0, dil * ntile, body, 0, unroll=DIL_UNROLL)


def _dil_attention(q0, q1, q2, k, v):
    b, s, _ = k.shape
    npair = DIL_SLOTS // 2
    pw = 2 * LANES
    in_spec = pl.BlockSpec((None, s, pw), lambda b_, hp: (b_, 0, hp))
    return pl.pallas_call(
        functools.partial(_dil_kernel, seq=s),
        grid=(b, npair),
        in_specs=[in_spec] * 5,
        out_specs=pl.BlockSpec((None, s, LANES), lambda b_, hp: (b_, 0, hp)),
        out_shape=jax.ShapeDtypeStruct((b, s, DIL_SLOTS * HEAD_DIM), F32),
        scratch_shapes=[pltpu.VMEM((2, s, LANES), F32)] * 3 + [pltpu.VMEM((s, LANES), F32)],
        compiler_params=_cparams(2), name="dil_attention",
    )(q0, q1, q2, k, v)


def kernel(x, mem, attn_norm, mlp_norm, w_up, w_down, mem_norm, w_mem_kv, mem_q_norm, mem_k_norm,
           a_w_in, a_w_out, a_q_norm, a_k_norm, a_cmp_pos, a_cmp_w1, a_cmp_b1, a_cmp_w2, a_cmp_b2,
           kv_norm, w_kv_shared, kv_k_norm, b_w_in, b_w_out, b_q_norm):
    b, s, _ = x.shape
    n = b * s
    assert s % PROJ_ROW_TILE == 0 and s % ROW_TILE == 0 and s % SEL_KEY_TILE == 0
    assert s // SEL_BLOCK <= HEAD_DIM
    assert s >= WINDOW + Q_TILE and (s // DIL_PATTERNS[-1][1]) % DIL_BAND == 0
    x2d = x.reshape(n, D_MODEL)
    cos, sin = _rope_pair_tables(jnp.arange(s))

    mk, mv = _mem_kv(mem, mem_norm[:, None, :], w_mem_kv.astype(BF16), _gain_pair(mem_k_norm)[:, None, :])

    gt_a = jnp.zeros((SUBLANES, LANES), F32)
    gt_a = gt_a.at[0].set(_gain_pair(a_q_norm[0])).at[1].set(_gain_pair(a_k_norm[0, 1]))
    gt_a = gt_a.at[2].set(_gain_pair(a_k_norm[0, 2])).at[3].set(_gain_pair(mem_q_norm[0]))
    qa, ks, kw, vs, vw, kcvc, qm, gates = _a_proj(
        x2d, attn_norm[0:1], _prep_a_w_in(a_w_in[0]), cos, sin, gt_a, s)

    n_cmp = (s - CMP_BLOCK) // CMP_STRIDE + 1
    nc = s // CMP_STRIDE
    tab_c = _rope_table(jnp.arange(nc) * CMP_STRIDE + (CMP_BLOCK - 1))
    kc, vc = _compress(kcvc.reshape(b, s, -1),
                       *_prep_compress(a_cmp_pos[0], a_cmp_w1[0], a_cmp_b1[0], a_cmp_w2[0], a_cmp_b2[0]),
                       tab_c, _gain_slot(a_k_norm[0, 0], "rot")[None])

    n_blk = s // SEL_BLOCK
    c_start = np.arange(nc)[None, :] * CMP_STRIDE
    b_start = np.arange(n_blk)[:, None] * SEL_BLOCK
    overlap_t = ((c_start < b_start + SEL_BLOCK) & (c_start + CMP_BLOCK > b_start)
                 & (np.arange(nc)[None, :] < n_cmp))
    ov = jnp.asarray(overlap_t, BF16)

    qa3 = qa.reshape(b, s, -1)
    gates3 = gates.reshape(b, s, -1)
    o_a = _nsa_attention(qa3, kc, vc, kw.reshape(b, s, -1), vw.reshape(b, s, -1), ks.reshape(b, s, -1),
                         vs.reshape(b, s, -1), gates3, ov, n_cmp, min(SEL_TOPK, n_blk))
    x2d = _out_mlp(o_a.reshape(n, -1), qm, mk, mv, 0, s, x2d, a_w_out[0].astype(BF16), mlp_norm[0:1],
                   w_up[0].astype(BF16), w_down[0].astype(BF16), "a_out_mlp")

    gt_b = jnp.zeros((SUBLANES, LANES), F32)
    for gi in range(N_DIL_GROUPS):
        gt_b = gt_b.at[gi].set(_gain_pair(b_q_norm[0, gi]))
    gt_b = gt_b.at[3].set(_gain_pair(mem_q_norm[1])).at[4].set(_gain_pair(kv_k_norm))
    q0, q1, q2, qm1, kb, vb = _b_proj(x2d, jnp.stack([attn_norm[1], kv_norm]), b_w_in[0].astype(BF16),
                                      w_kv_shared.astype(BF16), cos, sin, gt_b, s)
    o_d = _dil_attention(*(t.reshape(b, s, -1) for t in (q0, q1, q2, kb, vb)))
    x2d = _out_mlp(o_d.reshape(n, -1), qm1, mk, mv, 1, s, x2d, b_w_out[0].astype(BF16), mlp_norm[1:2],
                   w_up[1].astype(BF16), w_down[1].astype(BF16), "b_out_mlp")
    return x2d.reshape(b, s, D_MODEL)
```

```python
import functools

import numpy as np
import jax
import jax.numpy as jnp
from jax import lax
from jax.experimental import pallas as pl
from jax.experimental.pallas import tpu as pltpu

D_MODEL = 1024
HEAD_DIM = 64
HALF = HEAD_DIM // 2
ROPE_THETA = 10000.0
EPS = 1e-6
NEG = -1e30
D_FF = 4 * D_MODEL
MEM_HEADS = 4
NSA_HEADS = 12
NSA_KV_HEADS = 3
NSA_GQA = NSA_HEADS // NSA_KV_HEADS
CMP_BLOCK = 32
CMP_STRIDE = 16
CMP_HIDDEN = 256
SEL_BLOCK = 64
SEL_TOPK = 16
WINDOW = 512
SEL_FORCE = 1e9
DIL_PATTERNS = ((128, 1), (512, 4), (2048, 16))
N_DIL_GROUPS = 3
DIL_SLOTS = 8
MEM_W = MEM_HEADS * HEAD_DIM
A_Q = NSA_HEADS * HEAD_DIM
A_KV = NSA_KV_HEADS * HEAD_DIM
B_Q = N_DIL_GROUPS * DIL_SLOTS * HEAD_DIM
LOG2E = 1.4426950408889634
Q_SCALE = HEAD_DIM ** -0.5 * LOG2E

LANES = 128
SUBLANES = 8
VMEM_LIMIT = 56 * 1024 * 1024
ROW_TILE = 512
PROJ_ROW_TILE = 1024
Q_TILE = 512
DIL_UNROLL = 4
SEL_KEY_TILE = 512
DIL_Q_TILE = 128
DIL_Q_TILE_WIDE = 256
DIL_BAND = 128
FF_CHUNK = 1024

BF16 = jnp.bfloat16
F32 = jnp.float32


def _cparams(n_grid):
    return pltpu.CompilerParams(dimension_semantics=("arbitrary",) * n_grid,
                                vmem_limit_bytes=VMEM_LIMIT)


def _const_spec(shape):
    nd = len(shape)
    return pl.BlockSpec(shape, lambda *_: (0,) * nd, pipeline_mode=pl.Buffered(1))


def _dot(a, b):
    return jnp.dot(a, b, preferred_element_type=F32)


def _dot_nt(a, b):
    return lax.dot_general(a, b, (((1,), (1,)), ((), ())), preferred_element_type=F32)


def _rms_rows(x, gain):
    return x * lax.rsqrt(jnp.mean(x * x, axis=-1, keepdims=True) + EPS) * gain


def _slot_norm(y, gt, tab):
    z = y * lax.rsqrt(jnp.mean(y * y, axis=-1, keepdims=True) + EPS) * gt
    if tab is not None:
        z = z * tab
        z = z + pltpu.roll(z, HEAD_DIM, 1)
    return z


def _head_mean_sq(y, ones_ref):
    sq = (y * y).astype(BF16)
    width = y.shape[1]
    step = ones_ref.shape[0]
    parts = []
    for c in range(0, width, step):
        cw = min(step, width - c)
        parts.append(_dot(sq[:, c:c + cw], ones_ref[:cw, :cw]))
    return jnp.concatenate(parts, axis=1) * (1.0 / HEAD_DIM)


def _pair_norm(y, ms, gain, rope):
    z = y * lax.rsqrt(ms + EPS) * gain
    if rope is not None:
        cos, sin_signed, first = rope
        partner = jnp.where(first, pltpu.roll(z, LANES - HALF, 1), pltpu.roll(z, HALF, 1))
        z = z * cos + partner * sin_signed
    return z


def _lane_lo(rows):
    return lax.broadcasted_iota(jnp.int32, (rows, LANES), 1) < HEAD_DIM


def _swap_halves(x):
    return pltpu.roll(x, HEAD_DIM, 1)


def _pair_num_den(acc_even, acc_odd, lo):
    return jnp.where(lo, acc_even, acc_odd), _swap_halves(jnp.where(lo, acc_odd, acc_even))


def _rot_half_cols(w):
    return jnp.concatenate([-w[..., HALF:], w[..., :HALF]], axis=-1)


def _swap_half(g):
    return jnp.concatenate([g[..., HALF:], g[..., :HALF]], axis=-1)


def _slots(w, n, kind):
    k = w.shape[0]
    w = w.reshape(k, n, HEAD_DIM)
    other = _rot_half_cols(w) if kind == "rot" else w
    return jnp.concatenate([w, other], axis=-1).reshape(k, n * LANES)


def _gain_slot(g, kind):
    other = _swap_half(g) if kind == "rot" else g
    return jnp.concatenate([g, other], axis=-1)


def _gain_pair(g):
    return jnp.concatenate([g, g], axis=-1)


def _rope_cos_sin(pos):
    inv_freq = ROPE_THETA ** (-jnp.arange(HALF, dtype=F32) / HALF)
    ang = jnp.asarray(pos, F32)[:, None] * inv_freq[None, :]
    return jnp.cos(ang), jnp.sin(ang)


def _rope_table(pos):
    cos, sin = _rope_cos_sin(pos)
    return jnp.concatenate([cos, cos, sin, sin], axis=-1)


def _rope_pair_tables(pos):
    cos, sin = _rope_cos_sin(pos)
    return (jnp.concatenate([cos, cos, cos, cos], axis=-1),
            jnp.concatenate([-sin, sin, -sin, sin], axis=-1))


def _pad_cols(w, width):
    return jnp.pad(w, ((0, 0), (0, width - w.shape[1])))


KV_PAIRS = (NSA_KV_HEADS + 1) // 2
A_COL_Q = 0
A_COL_KS = A_COL_Q + A_Q
A_COL_KW = A_COL_KS + KV_PAIRS * LANES
A_COL_VS = A_COL_KW + KV_PAIRS * LANES
A_COL_VW = A_COL_VS + KV_PAIRS * LANES
A_COL_KCVC = A_COL_VW + KV_PAIRS * LANES
A_COL_QM = A_COL_KCVC + NSA_KV_HEADS * LANES
A_COL_G = A_COL_QM + MEM_W
A_COLS = A_COL_G + NSA_KV_HEADS * LANES


def _prep_a_w_in(w):
    o = 0
    q = w[:, o:o + A_Q]; o += A_Q
    kc = w[:, o:o + A_KV]; o += A_KV
    vc = w[:, o:o + A_KV]; o += A_KV
    ks = w[:, o:o + A_KV]; o += A_KV
    vs = w[:, o:o + A_KV]; o += A_KV
    kw = w[:, o:o + A_KV]; o += A_KV
    vw = w[:, o:o + A_KV]; o += A_KV
    qm = w[:, o:o + MEM_W]; o += MEM_W
    gl = w[:, o:]
    gl = gl.reshape(D_MODEL, NSA_KV_HEADS, NSA_GQA, 3).transpose(0, 1, 3, 2)
    gl = gl.reshape(D_MODEL, NSA_KV_HEADS, 3 * NSA_GQA)
    gl = jnp.pad(gl, ((0, 0), (0, 0), (0, LANES - 3 * NSA_GQA))).reshape(D_MODEL, NSA_KV_HEADS * LANES)
    kcvc = jnp.concatenate([kc.reshape(D_MODEL, NSA_KV_HEADS, HEAD_DIM),
                            vc.reshape(D_MODEL, NSA_KV_HEADS, HEAD_DIM)], axis=-1)
    kvw = KV_PAIRS * LANES
    cols = [q, _pad_cols(ks, kvw), _pad_cols(kw, kvw), _pad_cols(vs, kvw), _pad_cols(vw, kvw),
            kcvc.reshape(D_MODEL, NSA_KV_HEADS * LANES), qm, gl]
    return jnp.concatenate(cols, axis=1).astype(BF16)


def _store_head_slots(ref, z, n_heads, pair, aux):
    lo = _lane_lo(z.shape[0])
    for half, data in enumerate((z, _swap_halves(z))):
        h = 2 * pair + half
        if h < n_heads:
            ref[:, h * LANES:(h + 1) * LANES] = jnp.where(lo, data, aux).astype(ref.dtype)


def _store_value_slots(ref, y, n_heads, pair, both):
    lo = _lane_lo(y.shape[0])
    ys = _swap_halves(y)
    for half in range(2):
        h = 2 * pair + half
        if h >= n_heads:
            continue
        even = jnp.where(lo, y if half == 0 else ys, 1.0)
        odd = jnp.where(lo, 1.0, ys if half == 0 else y)
        if both:
            ref[:, 2 * h * LANES:(2 * h + 1) * LANES] = even.astype(ref.dtype)
            ref[:, (2 * h + 1) * LANES:(2 * h + 2) * LANES] = odd.astype(ref.dtype)
        else:
            ref[:, h * LANES:(h + 1) * LANES] = (even if h % 2 == 0 else odd).astype(ref.dtype)


def _rope_args(cos_ref, sin_ref, rows):
    lane = lax.broadcasted_iota(jnp.int32, (rows, LANES), 1)
    return cos_ref[...], sin_ref[...], (lane & HALF) == 0


def _a_proj_kernel(x_ref, gain_ref, w_ref, cos_ref, sin_ref, gt_ref, ones_ref,
                   qa_ref, ks_ref, kw_ref, vs_ref, vw_ref, kcvc_ref, qm_ref, g_ref, *, seq, tm):
    hn = _rms_rows(x_ref[...], gain_ref[...]).astype(BF16)
    rope = _rope_args(cos_ref, sin_ref, tm)
    lane = lax.broadcasted_iota(jnp.int32, (tm, LANES), 1)
    row = lax.broadcasted_iota(jnp.int32, (tm, LANES), 0)
    tok = (pl.program_id(0) % (seq // tm)) * tm + row
    blk_ind = jnp.where(lane - HEAD_DIM == tok // SEL_BLOCK, 1.0, 0.0)

    yq = _dot(hn, w_ref[:, A_COL_Q:A_COL_KS])
    ms = _head_mean_sq(yq, ones_ref)
    for p in range(NSA_HEADS // 2):
        sl = slice(p * LANES, (p + 1) * LANES)
        z = _pair_norm(yq[:, sl], ms[:, sl], gt_ref[0:1, :], rope) * Q_SCALE
        _store_head_slots(qa_ref, z, NSA_HEADS, p, 0.0)

    yk = _dot(hn, w_ref[:, A_COL_KS:A_COL_VS])
    ms = _head_mean_sq(yk, ones_ref)
    for p in range(KV_PAIRS):
        sl = slice(p * LANES, (p + 1) * LANES)
        z = _pair_norm(yk[:, sl], ms[:, sl], gt_ref[1:2, :], rope)
        _store_head_slots(ks_ref, z, NSA_KV_HEADS, p, blk_ind)
        sl = slice((KV_PAIRS + p) * LANES, (KV_PAIRS + p + 1) * LANES)
        z = _pair_norm(yk[:, sl], ms[:, sl], gt_ref[2:3, :], rope)
        _store_head_slots(kw_ref, z, NSA_KV_HEADS, p, 0.0)

    yv = _dot(hn, w_ref[:, A_COL_VS:A_COL_KCVC])
    for j, ref in enumerate((vs_ref, vw_ref)):
        for p in range(KV_PAIRS):
            c = (j * KV_PAIRS + p) * LANES
            _store_value_slots(ref, yv[:, c:c + LANES], NSA_KV_HEADS, p, True)

    yr = _dot(hn, w_ref[:, A_COL_KCVC:A_COLS])
    kcvc_ref[...] = yr[:, :NSA_KV_HEADS * LANES]
    c = A_COL_QM - A_COL_KCVC
    ym = yr[:, c:c + MEM_W]
    ms = _head_mean_sq(ym, ones_ref)
    for p in range(MEM_HEADS // 2):
        sl = slice(p * LANES, (p + 1) * LANES)
        z = _pair_norm(ym[:, sl], ms[:, sl], gt_ref[3:4, :], None) * Q_SCALE
        _store_head_slots(qm_ref, z, MEM_HEADS, p, 0.0)
    c = A_COL_G - A_COL_KCVC
    g_ref[...] = jax.nn.sigmoid(yr[:, c:c + NSA_KV_HEADS * LANES])


def _head_ones():
    idx = np.arange(2 * LANES) // HEAD_DIM
    return jnp.asarray(idx[:, None] == idx[None, :], BF16)


def _a_proj(x2d, gain, w, cos, sin, gt, seq):
    n = x2d.shape[0]
    tm = PROJ_ROW_TILE
    nseq = seq // tm
    row_spec = lambda width: pl.BlockSpec((tm, width), lambda i: (i, 0))
    tab_spec = pl.BlockSpec((tm, LANES), lambda i: (i % nseq, 0))
    kvw = NSA_KV_HEADS * LANES
    widths = (NSA_HEADS * LANES, kvw, kvw, 2 * kvw, 2 * kvw, kvw, MEM_HEADS * LANES, kvw)
    dtypes = (BF16, BF16, BF16, BF16, BF16, F32, BF16, F32)
    return pl.pallas_call(
        functools.partial(_a_proj_kernel, seq=seq, tm=tm),
        grid=(n // tm,),
        in_specs=[row_spec(D_MODEL), _const_spec((1, D_MODEL)), _const_spec((D_MODEL, A_COLS)),
                  tab_spec, tab_spec, _const_spec((SUBLANES, LANES)), _const_spec((2 * LANES, 2 * LANES))],
        out_specs=[row_spec(wd) for wd in widths],
        out_shape=[jax.ShapeDtypeStruct((n, wd), dt) for wd, dt in zip(widths, dtypes)],
        compiler_params=_cparams(1), name="a_proj",
    )(x2d, gain, w, cos, sin, gt, _head_ones())


def _compress_kernel(x_ref, pos_ref, w1_ref, b1_ref, w2_ref, b2_ref, tab_ref, gt_ref, kc_ref, vc_ref):
    nc = kc_ref.shape[0]
    lo = _lane_lo(nc)
    h = jnp.concatenate([x_ref[pl.ds(j, nc, stride=CMP_STRIDE), :] for j in range(CMP_STRIDE)], axis=1)
    top = _dot((h + pos_ref[0:1, :]).astype(BF16), w1_ref[0])
    bot = _dot((h + pos_ref[1:2, :]).astype(BF16), w1_ref[1])
    pre = top + pltpu.roll(bot, nc - 1, 0) + b1_ref[...]
    hid = jax.nn.gelu(pre).astype(BF16)
    yk = _dot(hid[:, :CMP_HIDDEN], w2_ref[0]) + b2_ref[0:1, :]
    yv = _dot(hid[:, CMP_HIDDEN:], w2_ref[1]) + b2_ref[1:2, :]
    kc_ref[...] = jnp.where(lo, _slot_norm(yk, gt_ref[...], tab_ref[...]), 0.0).astype(BF16)
    vc_ref[:, :LANES] = jnp.where(lo, yv, 1.0).astype(BF16)
    vc_ref[:, LANES:] = jnp.where(lo, 1.0, yv).astype(BF16)


def _compress(kcvc, pos, w1, b1, w2, b2, tab_c, gt):
    b, s, _ = kcvc.shape
    nc = s // CMP_STRIDE
    ospec = lambda width: pl.BlockSpec((None, None, nc, width), lambda b_, g: (b_, g, 0, 0))
    return pl.pallas_call(
        _compress_kernel,
        grid=(b, NSA_KV_HEADS),
        in_specs=[pl.BlockSpec((None, s, LANES), lambda b_, g: (b_, 0, g)),
                  _const_spec(pos.shape), _const_spec(w1.shape), _const_spec(b1.shape),
                  _const_spec(w2.shape), _const_spec(b2.shape), _const_spec(tab_c.shape),
                  _const_spec(gt.shape)],
        out_specs=[ospec(LANES), ospec(2 * LANES)],
        out_shape=[jax.ShapeDtypeStruct((b, NSA_KV_HEADS, nc, LANES), BF16),
                   jax.ShapeDtypeStruct((b, NSA_KV_HEADS, nc, 2 * LANES), BF16)],
        compiler_params=_cparams(2), name="compress",
    )(kcvc, pos, w1, b1, w2, b2, tab_c, gt)


def _prep_compress(cmp_pos, cmp_w1, cmp_b1, cmp_w2, cmp_b2):
    p = cmp_pos.reshape(2, 2, CMP_STRIDE, HEAD_DIM).transpose(1, 2, 0, 3).reshape(2, CMP_STRIDE * LANES)
    w = cmp_w1.reshape(2, 2, CMP_STRIDE, HEAD_DIM, CMP_HIDDEN)
    z = jnp.zeros_like(w[0])
    wk = jnp.concatenate([w[0], z], axis=-1)
    wv = jnp.concatenate([z, w[1]], axis=-1)
    w1 = jnp.concatenate([wk, wv], axis=2).reshape(2, CMP_STRIDE * LANES, 2 * CMP_HIDDEN)
    b1 = jnp.concatenate([cmp_b1[0], cmp_b1[1]])[None]
    w2 = jnp.stack([_slots(cmp_w2[0], 1, "rot"), _slots(cmp_w2[1], 1, "dup")])
    b2 = jnp.stack([_slots(cmp_b2[0][None], 1, "rot")[0], _slots(cmp_b2[1][None], 1, "dup")[0]])
    return p, w1.astype(BF16), b1, w2.astype(BF16), b2


STACK_ORDER = (0, 2, 1, 3)


def _stack_q(q_ref, extra=None):
    parts = []
    for r in STACK_ORDER:
        q = q_ref[:, r * LANES:(r + 1) * LANES]
        parts.append(q if extra is None else q + extra)
    return jnp.concatenate(parts, axis=0)


def _gate_pairs(g_ref, branch, tq):
    lo = _lane_lo(tq)
    col = lambda r: jnp.broadcast_to(g_ref[:, branch * NSA_GQA + r:branch * NSA_GQA + r + 1], (tq, LANES))
    return jnp.concatenate([jnp.where(lo, col(0), col(1)), jnp.where(lo, col(2), col(3))], axis=0)


def _store_pairs(o_ref, out, tq):
    o_ref[:, :LANES] = out[:tq]
    o_ref[:, LANES:] = out[tq:]


def _add_tile_mask(s, mask, tq):
    nk = s.shape[1]
    return (s.reshape(NSA_GQA, tq, nk) + mask).reshape(NSA_GQA * tq, nk)


def _window_branch(i, q_ref, kw_ref, vw_ref, g_ref, wm_ref, tq, nk):
    k0 = pl.multiple_of(jnp.maximum(i * tq + tq - nk, 0), tq)
    s = _add_tile_mask(_dot_nt(_stack_q(q_ref), kw_ref[pl.ds(k0, nk), :]), wm_ref[...], tq)
    m = jnp.max(s, axis=-1, keepdims=True)
    p = jnp.exp2(s - m).astype(BF16)
    num, den = _pair_num_den(_dot(p[:2 * tq], vw_ref[pl.ds(k0, nk), :LANES]),
                             _dot(p[2 * tq:], vw_ref[pl.ds(k0, nk), LANES:]), _lane_lo(2 * tq))
    return num / den * _gate_pairs(g_ref, 2, tq)


def _compressed_branch(i, q_ref, kc_ref, vc_ref, g_ref, ov_ref, tq, n_cmp, n_blk, n_sel):
    lo = _lane_lo(tq)
    kc = kc_ref[...]
    nc = kc.shape[0]
    t_row = i * tq + lax.broadcasted_iota(jnp.int32, (1, tq), 1)
    c_col = lax.broadcasted_iota(jnp.int32, (nc, 1), 0)
    cend_col = jnp.where(c_col < n_cmp, c_col * CMP_STRIDE + (CMP_BLOCK - 1), jnp.int32(2 ** 30))
    mask_t = cend_col <= t_row
    psum = jnp.zeros((nc, tq), F32)
    xs = []
    for r in range(NSA_GQA):
        st = jnp.where(mask_t, _dot_nt(kc, q_ref[:, r * LANES:(r + 1) * LANES]), NEG)
        mt = jnp.max(st, axis=0, keepdims=True)
        pt = jnp.exp2(st - mt)
        lt = jnp.sum(pt, axis=0, keepdims=True)
        pt = pt * jnp.where(mt > NEG / 2, 1.0 / lt, 0.0)
        psum = psum + pt
        vsl = slice(0, LANES) if r % 2 == 0 else slice(LANES, 2 * LANES)
        xs.append(_dot(pt.T.astype(BF16), vc_ref[:, vsl]))
    out_c = jnp.concatenate([jnp.where(lo, xs[0], xs[1]), jnp.where(lo, xs[2], xs[3])], axis=0)
    out_c = out_c * _gate_pairs(g_ref, 0, tq)

    ps_hi = psum.astype(BF16)
    ps_lo = (psum - ps_hi.astype(F32)).astype(BF16)
    ov = ov_ref[...]
    imp = _dot(ov, ps_hi) + _dot(ov, ps_lo)

    cur = t_row // SEL_BLOCK
    ngrp = n_blk // SUBLANES
    jsub = lax.broadcasted_iota(jnp.int32, (SUBLANES, tq), 0)
    vals = []
    for a in range(ngrp):
        j = jsub + a * SUBLANES
        forced = (j == 0) | (j == cur) | (j == cur - 1)
        v = jnp.where(j <= cur, imp[a * SUBLANES:(a + 1) * SUBLANES], NEG)
        vals.append(jnp.where(forced, SEL_FORCE, v))
    ranks = [jnp.zeros((SUBLANES, tq), F32) for _ in range(ngrp)]
    for jj in range(n_blk):
        a0, s0 = divmod(jj, SUBLANES)
        rowb = jnp.broadcast_to(vals[a0][s0:s0 + 1, :], (SUBLANES, tq))
        for a in range(ngrp):
            if a > a0:
                beats = jnp.where(rowb >= vals[a], 1.0, 0.0)
            elif a < a0:
                beats = jnp.where(rowb > vals[a], 1.0, 0.0)
            else:
                beats = jnp.where(jsub > s0, jnp.where(rowb >= vals[a], 1.0, 0.0),
                                  jnp.where(rowb > vals[a], 1.0, 0.0))
            ranks[a] = ranks[a] + beats
    parts = [jnp.zeros((HEAD_DIM, tq), F32)]
    for a in range(ngrp):
        live = jnp.where(vals[a] > NEG / 2, 0.0, NEG)
        parts.append(jnp.where(ranks[a] < n_sel, live, NEG))
    if n_blk < HEAD_DIM:
        parts.append(jnp.zeros((HEAD_DIM - n_blk, tq), F32))
    bias_t = jnp.concatenate(parts, axis=0)
    return out_c, bias_t.T.astype(BF16)


def _band_masks(n_var, tq, nk, k0_of, lo_dist, hi_dist):
    r = np.arange(tq)[None, :, None]
    c = np.arange(nk)[None, None, :]
    v = np.arange(n_var)[:, None, None]
    dist = v * tq + r - (np.asarray([k0_of(x) for x in range(n_var)])[:, None, None] + c)
    return jnp.asarray(np.where((dist >= lo_dist) & (dist <= hi_dist), 0.0, NEG), F32)


def _selected_branch(i, q_ref, bias, k_ref, v_ref, g_ref, dm_ref, m_ref, acc_ref, tq, tk):
    rows = NSA_GQA * tq
    half = 2 * tq
    q4 = _stack_q(q_ref, bias)
    m_ref[...] = jnp.full((rows, LANES), NEG, F32)
    acc_ref[...] = jnp.zeros((rows, LANES), F32)
    rep = tk // LANES

    def step(kt, masked):
        k0 = pl.multiple_of(kt * tk, tk)
        s = _dot_nt(q4, k_ref[pl.ds(k0, tk), :])
        if masked:
            s = _add_tile_mask(s, dm_ref[...], tq)
        m_prev = m_ref[...]
        m_next = jnp.maximum(m_prev, jnp.max(s, axis=-1, keepdims=True))
        p = jnp.exp2(s - jnp.tile(m_next, (1, rep))).astype(BF16)
        alpha = jnp.exp2(m_prev - m_next)
        m_ref[...] = m_next
        acc_ref[:half] = acc_ref[:half] * alpha[:half] + _dot(p[:half], v_ref[pl.ds(k0, tk), :LANES])
        acc_ref[half:] = acc_ref[half:] * alpha[half:] + _dot(p[half:], v_ref[pl.ds(k0, tk), LANES:])

    kt_diag = (i * tq) // tk
    lax.fori_loop(0, kt_diag, lambda kt, c: (step(kt, False), c)[1], 0)
    step(kt_diag, True)
    num, den = _pair_num_den(acc_ref[:half], acc_ref[half:], _lane_lo(half))
    return num / den * _gate_pairs(g_ref, 1, tq)


def _nsa_kernel(q_ref, kc_ref, vc_ref, kw_ref, vw_ref, ks_ref, vs_ref, g_ref, ov_ref, wm_ref, dm_ref,
                o_ref, m_ref, acc_ref, *, tq, tk, nk, n_cmp, n_blk, n_sel):
    i = pl.program_id(2)
    out_w = _window_branch(i, q_ref, kw_ref, vw_ref, g_ref, wm_ref, tq, nk)
    out_c, bias = _compressed_branch(i, q_ref, kc_ref, vc_ref, g_ref, ov_ref, tq, n_cmp, n_blk, n_sel)
    _store_pairs(o_ref, out_c + out_w, tq)
    out_s = _selected_branch(i, q_ref, bias, ks_ref, vs_ref, g_ref, dm_ref, m_ref, acc_ref, tq, tk)
    o_ref[:, :LANES] += out_s[:tq]
    o_ref[:, LANES:] += out_s[tq:]


def _nsa_attention(qa, kc, vc, kw, vw, ks, vs, gates, ov, n_cmp, n_sel):
    b, s, _ = qa.shape
    tq, tk = Q_TILE, min(SEL_KEY_TILE, s)
    nk = min(WINDOW + tq, s)
    nc = kc.shape[2]
    n_blk = s // SEL_BLOCK
    gw = NSA_GQA * LANES
    n_early = (nk - tq) // tq
    wmask = _band_masks(n_early + 1, tq, nk, lambda v: max(v * tq + tq - nk, 0), 0, WINDOW - 1)
    n_var = tk // tq
    dmask = _band_masks(n_var, tq, tk, lambda v: 0, 0, tk)
    kern = functools.partial(_nsa_kernel, tq=tq, tk=tk, nk=nk, n_cmp=n_cmp, n_blk=n_blk, n_sel=n_sel)
    k_spec = pl.BlockSpec((None, s, LANES), lambda b_, g, i: (b_, 0, g))
    v_spec = pl.BlockSpec((None, s, 2 * LANES), lambda b_, g, i: (b_, 0, g))
    return pl.pallas_call(
        kern,
        grid=(b, NSA_KV_HEADS, s // tq),
        in_specs=[pl.BlockSpec((None, tq, gw), lambda b_, g, i: (b_, i, g)),
                  pl.BlockSpec((None, None, nc, LANES), lambda b_, g, i: (b_, g, 0, 0)),
                  pl.BlockSpec((None, None, nc, 2 * LANES), lambda b_, g, i: (b_, g, 0, 0)),
                  k_spec, v_spec, k_spec, v_spec,
                  pl.BlockSpec((None, tq, LANES), lambda b_, g, i: (b_, i, g)),
                  _const_spec(ov.shape),
                  pl.BlockSpec((None, tq, nk), lambda b_, g, i: (jnp.minimum(i, n_early), 0, 0)),
                  pl.BlockSpec((None, tq, tk), lambda b_, g, i: (i % n_var, 0, 0))],
        out_specs=pl.BlockSpec((None, tq, 2 * LANES), lambda b_, g, i: (b_, i, g)),
        out_shape=jax.ShapeDtypeStruct((b, s, NSA_KV_HEADS * 2 * LANES), F32),
        scratch_shapes=[pltpu.VMEM((NSA_GQA * tq, LANES), F32), pltpu.VMEM((NSA_GQA * tq, LANES), F32)],
        compiler_params=_cparams(3), name="nsa_attention",
    )(qa, kc, vc, kw, vw, ks, vs, gates, ov, wmask, dmask)


def _mem_kv_kernel(mem_ref, gain_ref, w_ref, gt_ref, ones_ref, k_ref, v_ref):
    hn = _rms_rows(mem_ref[...], gain_ref[...]).astype(BF16)
    y = _dot(hn, w_ref[...])
    ms = _head_mean_sq(y[:, :MEM_W], ones_ref)
    for p in range(MEM_HEADS // 2):
        sl = slice(p * LANES, (p + 1) * LANES)
        z = _pair_norm(y[:, sl], ms[:, sl], gt_ref[...], None)
        _store_head_slots(k_ref, z, MEM_HEADS, p, 0.0)
        c = MEM_W + p * LANES
        _store_value_slots(v_ref, y[:, c:c + LANES], MEM_HEADS, p, False)


def _mem_kv(mem, gains, w, gt):
    b, nm, _ = mem.shape
    nl = gains.shape[0]
    width = MEM_HEADS * LANES
    out_spec = pl.BlockSpec((None, None, nm, width), lambda l, b_: (l, b_, 0, 0))
    return pl.pallas_call(
        _mem_kv_kernel,
        grid=(nl, b),
        in_specs=[pl.BlockSpec((None, nm, D_MODEL), lambda l, b_: (b_, 0, 0)),
                  pl.BlockSpec((None, 1, D_MODEL), lambda l, b_: (l, 0, 0)),
                  pl.BlockSpec((None, D_MODEL, 2 * MEM_W), lambda l, b_: (l, 0, 0)),
                  pl.BlockSpec((None, 1, LANES), lambda l, b_: (l, 0, 0)),
                  _const_spec((2 * LANES, 2 * LANES))],
        out_specs=[out_spec, out_spec],
        out_shape=[jax.ShapeDtypeStruct((nl, b, nm, width), BF16)] * 2,
        compiler_params=_cparams(2), name="mem_kv",
    )(mem, gains, w, gt, _head_ones())


def _mem_attend(q_ref, k_ref, v_ref):
    lo = _lane_lo(q_ref.shape[0])
    accs = []
    for h in range(MEM_HEADS):
        sl = slice(h * LANES, (h + 1) * LANES)
        s = _dot_nt(q_ref[:, sl], k_ref[:, sl])
        m = jnp.max(s, axis=-1, keepdims=True)
        accs.append(_dot(jnp.exp2(s - m).astype(BF16), v_ref[:, sl]))
    outs = []
    for pair in range(MEM_HEADS // 2):
        num, den = _pair_num_den(accs[2 * pair], accs[2 * pair + 1], lo)
        outs.append(num / den)
    return jnp.concatenate(outs, axis=1)


def _out_mlp_kernel(o_ref, qm_ref, mk_ref, mv_ref, x_ref, wo_ref, gain_ref, wup_ref, wdn_ref, out_ref):
    km = o_ref.shape[1]
    o_mem = _mem_attend(qm_ref, mk_ref, mv_ref).astype(BF16)
    x1 = x_ref[...] + _dot(o_ref[...].astype(BF16), wo_ref[:km, :]) + _dot(o_mem, wo_ref[km:, :])
    hn = _rms_rows(x1, gain_ref[...]).astype(BF16)
    out_ref[...] = x1
    for c in range(D_FF // FF_CHUNK):
        u = _dot(hn, wup_ref[:, c * FF_CHUNK:(c + 1) * FF_CHUNK])
        u = jnp.square(jnp.maximum(u, 0.0)).astype(BF16)
        out_ref[...] += _dot(u, wdn_ref[c * FF_CHUNK:(c + 1) * FF_CHUNK, :])


def _out_mlp(o_main, qm, mk, mv, layer, seq, x2d, wo, gain, wup, wdn, name):
    n = x2d.shape[0]
    tm = ROW_TILE
    nseq = seq // tm
    nm, mw = mk.shape[2], mk.shape[3]
    row_spec = lambda width: pl.BlockSpec((tm, width), lambda i: (i, 0))
    kv_spec = pl.BlockSpec((None, None, nm, mw), lambda i: (layer, i // nseq, 0, 0))
    return pl.pallas_call(
        _out_mlp_kernel,
        grid=(n // tm,),
        in_specs=[row_spec(o_main.shape[1]), row_spec(qm.shape[1]), kv_spec, kv_spec, row_spec(D_MODEL),
                  _const_spec(wo.shape), _const_spec((1, D_MODEL)), _const_spec(wup.shape),
                  _const_spec(wdn.shape)],
        out_specs=row_spec(D_MODEL),
        out_shape=jax.ShapeDtypeStruct((n, D_MODEL), F32),
        compiler_params=_cparams(1), name=name,
    )(o_main, qm, mk, mv, x2d, wo, gain, wup, wdn)


def _b_proj_kernel(x_ref, gains_ref, wq_ref, wkv_ref, cos_ref, sin_ref, gt_ref, ones_ref,
                   q0_ref, q1_ref, q2_ref, qm_ref, k_ref, v_ref, *, tm):
    x = x_ref[...]
    xn = x * lax.rsqrt(jnp.mean(x * x, axis=-1, keepdims=True) + EPS)
    rope = _rope_args(cos_ref, sin_ref, tm)
    hq = (xn * gains_ref[0:1, :]).astype(BF16)
    width = DIL_SLOTS * HEAD_DIM
    for gi, ref in enumerate((q0_ref, q1_ref, q2_ref)):
        y = _dot(hq, wq_ref[:, gi * width:(gi + 1) * width])
        ms = _head_mean_sq(y, ones_ref)
        for p in range(DIL_SLOTS // 2):
            sl = slice(p * LANES, (p + 1) * LANES)
            z = _pair_norm(y[:, sl], ms[:, sl], gt_ref[gi:gi + 1, :], rope) * Q_SCALE
            _store_head_slots(ref, z, DIL_SLOTS, p, 0.0)
    y = _dot(hq, wq_ref[:, B_Q:B_Q + MEM_W])
    ms = _head_mean_sq(y, ones_ref)
    for p in range(MEM_HEADS // 2):
        sl = slice(p * LANES, (p + 1) * LANES)
        z = _pair_norm(y[:, sl], ms[:, sl], gt_ref[3:4, :], None) * Q_SCALE
        _store_head_slots(qm_ref, z, MEM_HEADS, p, 0.0)
    hk = (xn * gains_ref[1:2, :]).astype(BF16)
    y = _dot(hk, wkv_ref[...])
    ms = _head_mean_sq(y[:, :width], ones_ref)
    for p in range(DIL_SLOTS // 2):
        sl = slice(p * LANES, (p + 1) * LANES)
        z = _pair_norm(y[:, sl], ms[:, sl], gt_ref[4:5, :], rope)
        _store_head_slots(k_ref, z, DIL_SLOTS, p, 0.0)
        c = width + p * LANES
        _store_value_slots(v_ref, y[:, c:c + LANES], DIL_SLOTS, p, False)


def _b_proj(x2d, gains, wq, wkv, cos, sin, gt, seq):
    n = x2d.shape[0]
    tm = PROJ_ROW_TILE
    nseq = seq // tm
    row_spec = lambda width: pl.BlockSpec((tm, width), lambda i: (i, 0))
    tab_spec = pl.BlockSpec((tm, LANES), lambda i: (i % nseq, 0))
    width = DIL_SLOTS * LANES
    widths = (width, width, width, MEM_HEADS * LANES, width, width)
    return pl.pallas_call(
        functools.partial(_b_proj_kernel, tm=tm),
        grid=(n // tm,),
        in_specs=[row_spec(D_MODEL), _const_spec(gains.shape), _const_spec(wq.shape),
                  _const_spec(wkv.shape), tab_spec, tab_spec, _const_spec(gt.shape),
                  _const_spec((2 * LANES, 2 * LANES))],
        out_specs=[row_spec(wd) for wd in widths],
        out_shape=[jax.ShapeDtypeStruct((n, wd), BF16) for wd in widths],
        compiler_params=_cparams(1), name="b_proj",
    )(x2d, gains, wq, wkv, cos, sin, gt, _head_ones())


def _dil_kernel(q0_ref, q1_ref, q2_ref, k_ref, v_ref, o_ref, qf_ref, kf_ref, vf_ref, lse_ref, *, seq):
    tq = DIL_Q_TILE
    heads = (slice(0, LANES), slice(LANES, 2 * LANES))
    for hd, sl in enumerate(heads):
        kf_ref[hd] = k_ref[:, sl].astype(F32)
        vf_ref[hd] = v_ref[:, sl].astype(F32)

    def attend(qs, ks, vs, qpos0, kpos0):
        nq, nk = qs[0].shape[0], ks[0].shape[0]
        lo = _lane_lo(nq)
        dist = (qpos0 + lax.broadcasted_iota(jnp.int32, (nq, 1), 0)
                - (kpos0 + lax.broadcasted_iota(jnp.int32, (1, nk), 1)))
        mask = (dist >= 0) & (dist <= DIL_BAND)
        accs, ms = [], []
        for q, k, v in zip(qs, ks, vs):
            s = jnp.where(mask, _dot_nt(q, k), NEG)
            m = jnp.max(s, axis=-1, keepdims=True)
            accs.append(_dot(jnp.exp2(s - m).astype(BF16), v))
            ms.append(jnp.broadcast_to(m, (nq, LANES)))
        num, den = _pair_num_den(accs[0], accs[1], lo)
        return num / den, jnp.where(lo, ms[0], ms[1]) + jnp.log2(den)

    def merge(o_old, l_old, o_new, l_new):
        mx = jnp.maximum(l_old, l_new)
        a = jnp.exp2(l_old - mx)
        b = jnp.exp2(l_new - mx)
        den = a + b
        return (a * o_old + b * o_new) / den, mx + jnp.log2(den)

    nk = min(tq + DIL_BAND, seq)

    def body0(i, c):
        q0 = pl.multiple_of(i * tq, tq)
        k0 = pl.multiple_of(jnp.clip(i * tq - DIL_BAND, 0, seq - nk), DIL_BAND)
        o, l = attend([q0_ref[pl.ds(q0, tq), sl] for sl in heads],
                      [k_ref[pl.ds(k0, nk), sl] for sl in heads],
                      [v_ref[pl.ds(k0, nk), sl] for sl in heads], q0, k0)
        o_ref[pl.ds(q0, tq), :] = o
        lse_ref[pl.ds(q0, tq), :] = l
        return c

    lax.fori_loop(0, seq // tq, body0, 0, unroll=DIL_UNROLL)

    for q_ref, (_, dil) in zip((q1_ref, q2_ref), DIL_PATTERNS[1:]):
        for hd, sl in enumerate(heads):
            qf_ref[hd] = q_ref[:, sl].astype(F32)
        length = seq // dil
        tqd = min(tq if dil < 2 * SUBLANES else DIL_Q_TILE_WIDE, length)
        nkd = min(tqd + DIL_BAND, length)
        ntile = length // tqd

        def body(it, c, dil=dil, length=length, tqd=tqd, nkd=nkd, ntile=ntile):
            r = it // ntile
            i = it - r * ntile
            qp = i * tqd
            kp = jnp.clip(qp - DIL_BAND, 0, length - nkd)
            qrows = pl.ds(r + dil * qp, tqd, stride=dil)
            krows = pl.ds(r + dil * kp, nkd, stride=dil)
            o, l = attend([qf_ref[hd, qrows, :].astype(BF16) for hd in range(2)],
                          [kf_ref[hd, krows, :].astype(BF16) for hd in range(2)],
                          [vf_ref[hd, krows, :].astype(BF16) for hd in range(2)], qp, kp)
            o, l = merge(o_ref[qrows, :], lse_ref[qrows, :], o, l)
            o_ref[qrows, :] = o
            lse_ref[qrows, :] = l
            return c

        lax.fori_loop(0, dil * ntile, body, 0, unroll=DIL_UNROLL)


def _dil_attention(q0, q1, q2, k, v):
    b, s, _ = k.shape
    npair = DIL_SLOTS // 2
    pw = 2 * LANES
    in_spec = pl.BlockSpec((None, s, pw), lambda b_, hp: (b_, 0, hp))
    return pl.pallas_call(
        functools.partial(_dil_kernel, seq=s),
        grid=(b, npair),
        in_specs=[in_spec] * 5,
        out_specs=pl.BlockSpec((None, s, LANES), lambda b_, hp: (b_, 0, hp)),
        out_shape=jax.ShapeDtypeStruct((b, s, DIL_SLOTS * HEAD_DIM), F32),
        scratch_shapes=[pltpu.VMEM((2, s, LANES), F32)] * 3 + [pltpu.VMEM((s, LANES), F32)],
        compiler_params=_cparams(2), name="dil_attention",
    )(q0, q1, q2, k, v)


def kernel(x, mem, attn_norm, mlp_norm, w_up, w_down, mem_norm, w_mem_kv, mem_q_norm, mem_k_norm,
           a_w_in, a_w_out, a_q_norm, a_k_norm, a_cmp_pos, a_cmp_w1, a_cmp_b1, a_cmp_w2, a_cmp_b2,
           kv_norm, w_kv_shared, kv_k_norm, b_w_in, b_w_out, b_q_norm):
    b, s, _ = x.shape
    n = b * s
    assert s % PROJ_ROW_TILE == 0 and s % ROW_TILE == 0 and s % SEL_KEY_TILE == 0
    assert s // SEL_BLOCK <= HEAD_DIM
    assert s >= WINDOW + Q_TILE and (s // DIL_PATTERNS[-1][1]) % DIL_BAND == 0
    x2d = x.reshape(n, D_MODEL)
    cos, sin = _rope_pair_tables(jnp.arange(s))

    mk, mv = _mem_kv(mem, mem_norm[:, None, :], w_mem_kv.astype(BF16), _gain_pair(mem_k_norm)[:, None, :])

    gt_a = jnp.zeros((SUBLANES, LANES), F32)
    gt_a = gt_a.at[0].set(_gain_pair(a_q_norm[0])).at[1].set(_gain_pair(a_k_norm[0, 1]))
    gt_a = gt_a.at[2].set(_gain_pair(a_k_norm[0, 2])).at[3].set(_gain_pair(mem_q_norm[0]))
    qa, ks, kw, vs, vw, kcvc, qm, gates = _a_proj(
        x2d, attn_norm[0:1], _prep_a_w_in(a_w_in[0]), cos, sin, gt_a, s)

    n_cmp = (s - CMP_BLOCK) // CMP_STRIDE + 1
    nc = s // CMP_STRIDE
    tab_c = _rope_table(jnp.arange(nc) * CMP_STRIDE + (CMP_BLOCK - 1))
    kc, vc = _compress(kcvc.reshape(b, s, -1),
                       *_prep_compress(a_cmp_pos[0], a_cmp_w1[0], a_cmp_b1[0], a_cmp_w2[0], a_cmp_b2[0]),
                       tab_c, _gain_slot(a_k_norm[0, 0], "rot")[None])

    n_blk = s // SEL_BLOCK
    c_start = np.arange(nc)[None, :] * CMP_STRIDE
    b_start = np.arange(n_blk)[:, None] * SEL_BLOCK
    overlap_t = ((c_start < b_start + SEL_BLOCK) & (c_start + CMP_BLOCK > b_start)
                 & (np.arange(nc)[None, :] < n_cmp))
    ov = jnp.asarray(overlap_t, BF16)

    qa3 = qa.reshape(b, s, -1)
    gates3 = gates.reshape(b, s, -1)
    o_a = _nsa_attention(qa3, kc, vc, kw.reshape(b, s, -1), vw.reshape(b, s, -1), ks.reshape(b, s, -1),
                         vs.reshape(b, s, -1), gates3, ov, n_cmp, min(SEL_TOPK, n_blk))
    x2d = _out_mlp(o_a.reshape(n, -1), qm, mk, mv, 0, s, x2d, a_w_out[0].astype(BF16), mlp_norm[0:1],
                   w_up[0].astype(BF16), w_down[0].astype(BF16), "a_out_mlp")

    gt_b = jnp.zeros((SUBLANES, LANES), F32)
    for gi in range(N_DIL_GROUPS):
        gt_b = gt_b.at[gi].set(_gain_pair(b_q_norm[0, gi]))
    gt_b = gt_b.at[3].set(_gain_pair(mem_q_norm[1])).at[4].set(_gain_pair(kv_k_norm))
    q0, q1, q2, qm1, kb, vb = _b_proj(x2d, jnp.stack([attn_norm[1], kv_norm]), b_w_in[0].astype(BF16),
                                      w_kv_shared.astype(BF16), cos, sin, gt_b, s)
    o_d = _dil_attention(*(t.reshape(b, s, -1) for t in (q0, q1, q2, kb, vb)))
    x2d = _out_mlp(o_d.reshape(n, -1), qm1, mk, mv, 1, s, x2d, b_w_out[0].astype(BF16), mlp_norm[1:2],
                   w_up[1].astype(BF16), w_down[1].astype(BF16), "b_out_mlp")
    return x2d.reshape(b, s, D_MODEL)
```

```python
import functools

import numpy as np
import jax
import jax.numpy as jnp
from jax import lax
from jax.experimental import pallas as pl
from jax.experimental.pallas import tpu as pltpu

D_MODEL = 1024
HEAD_DIM = 64
HALF = HEAD_DIM // 2
ROPE_THETA = 10000.0
EPS = 1e-6
NEG = -1e30
D_FF = 4 * D_MODEL
MEM_HEADS = 4
NSA_HEADS = 12
NSA_KV_HEADS = 3
NSA_GQA = NSA_HEADS // NSA_KV_HEADS
CMP_BLOCK = 32
CMP_STRIDE = 16
CMP_HIDDEN = 256
SEL_BLOCK = 64
SEL_TOPK = 16
WINDOW = 512
SEL_FORCE = 1e9
DIL_PATTERNS = ((128, 1), (512, 4), (2048, 16))
N_DIL_GROUPS = 3
DIL_SLOTS = 8
MEM_W = MEM_HEADS * HEAD_DIM
A_Q = NSA_HEADS * HEAD_DIM
A_KV = NSA_KV_HEADS * HEAD_DIM
B_Q = N_DIL_GROUPS * DIL_SLOTS * HEAD_DIM
LOG2E = 1.4426950408889634
Q_SCALE = HEAD_DIM ** -0.5 * LOG2E

LANES = 128
SUBLANES = 8
VMEM_LIMIT = 56 * 1024 * 1024
ROW_TILE = 512
PROJ_ROW_TILE = 1024
Q_TILE = 512
WIN_SUB_TILE = 256
DIL_UNROLL = 4
SEL_KEY_TILE = 512
DIL_Q_TILE = 128
DIL_Q_TILE_WIDE = 256
DIL_BAND = 128
FF_CHUNK = 1024

BF16 = jnp.bfloat16
F32 = jnp.float32


def _cparams(n_grid):
    return pltpu.CompilerParams(dimension_semantics=("arbitrary",) * n_grid,
                                vmem_limit_bytes=VMEM_LIMIT)


def _const_spec(shape):
    nd = len(shape)
    return pl.BlockSpec(shape, lambda *_: (0,) * nd, pipeline_mode=pl.Buffered(1))


def _dot(a, b):
    return jnp.dot(a, b, preferred_element_type=F32)


def _dot_nt(a, b):
    return lax.dot_general(a, b, (((1,), (1,)), ((), ())), preferred_element_type=F32)


def _rms_rows(x, gain):
    return x * lax.rsqrt(jnp.mean(x * x, axis=-1, keepdims=True) + EPS) * gain


def _slot_norm(y, gt, tab):
    z = y * lax.rsqrt(jnp.mean(y * y, axis=-1, keepdims=True) + EPS) * gt
    if tab is not None:
        z = z * tab
        z = z + pltpu.roll(z, HEAD_DIM, 1)
    return z


def _head_mean_sq(y, ones_ref):
    sq = (y * y).astype(BF16)
    width = y.shape[1]
    step = ones_ref.shape[0]
    parts = []
    for c in range(0, width, step):
        cw = min(step, width - c)
        parts.append(_dot(sq[:, c:c + cw], ones_ref[:cw, :cw]))
    return jnp.concatenate(parts, axis=1) * (1.0 / HEAD_DIM)


def _pair_norm(y, ms, gain, rope):
    z = y * lax.rsqrt(ms + EPS) * gain
    if rope is not None:
        cos, sin_signed, first = rope
        partner = jnp.where(first, pltpu.roll(z, LANES - HALF, 1), pltpu.roll(z, HALF, 1))
        z = z * cos + partner * sin_signed
    return z


def _lane_lo(rows):
    return lax.broadcasted_iota(jnp.int32, (rows, LANES), 1) < HEAD_DIM


def _swap_halves(x):
    return pltpu.roll(x, HEAD_DIM, 1)


def _pair_num_den(acc_even, acc_odd, lo):
    return jnp.where(lo, acc_even, acc_odd), _swap_halves(jnp.where(lo, acc_odd, acc_even))


def _rot_half_cols(w):
    return jnp.concatenate([-w[..., HALF:], w[..., :HALF]], axis=-1)


def _swap_half(g):
    return jnp.concatenate([g[..., HALF:], g[..., :HALF]], axis=-1)


def _slots(w, n, kind):
    k = w.shape[0]
    w = w.reshape(k, n, HEAD_DIM)
    other = _rot_half_cols(w) if kind == "rot" else w
    return jnp.concatenate([w, other], axis=-1).reshape(k, n * LANES)


def _gain_slot(g, kind):
    other = _swap_half(g) if kind == "rot" else g
    return jnp.concatenate([g, other], axis=-1)


def _gain_pair(g):
    return jnp.concatenate([g, g], axis=-1)


def _rope_cos_sin(pos):
    inv_freq = ROPE_THETA ** (-jnp.arange(HALF, dtype=F32) / HALF)
    ang = jnp.asarray(pos, F32)[:, None] * inv_freq[None, :]
    return jnp.cos(ang), jnp.sin(ang)


def _rope_table(pos):
    cos, sin = _rope_cos_sin(pos)
    return jnp.concatenate([cos, cos, sin, sin], axis=-1)


def _rope_pair_tables(pos):
    cos, sin = _rope_cos_sin(pos)
    return (jnp.concatenate([cos, cos, cos, cos], axis=-1),
            jnp.concatenate([-sin, sin, -sin, sin], axis=-1))


def _pad_cols(w, width):
    return jnp.pad(w, ((0, 0), (0, width - w.shape[1])))


KV_PAIRS = (NSA_KV_HEADS + 1) // 2
A_COL_Q = 0
A_COL_KS = A_COL_Q + A_Q
A_COL_KW = A_COL_KS + KV_PAIRS * LANES
A_COL_VS = A_COL_KW + KV_PAIRS * LANES
A_COL_VW = A_COL_VS + KV_PAIRS * LANES
A_COL_KCVC = A_COL_VW + KV_PAIRS * LANES
A_COL_QM = A_COL_KCVC + NSA_KV_HEADS * LANES
A_COL_G = A_COL_QM + MEM_W
A_COLS = A_COL_G + NSA_KV_HEADS * LANES


def _prep_a_w_in(w):
    o = 0
    q = w[:, o:o + A_Q]; o += A_Q
    kc = w[:, o:o + A_KV]; o += A_KV
    vc = w[:, o:o + A_KV]; o += A_KV
    ks = w[:, o:o + A_KV]; o += A_KV
    vs = w[:, o:o + A_KV]; o += A_KV
    kw = w[:, o:o + A_KV]; o += A_KV
    vw = w[:, o:o + A_KV]; o += A_KV
    qm = w[:, o:o + MEM_W]; o += MEM_W
    gl = w[:, o:]
    gl = gl.reshape(D_MODEL, NSA_KV_HEADS, NSA_GQA, 3).transpose(0, 1, 3, 2)
    gl = gl.reshape(D_MODEL, NSA_KV_HEADS, 3 * NSA_GQA)
    gl = jnp.pad(gl, ((0, 0), (0, 0), (0, LANES - 3 * NSA_GQA))).reshape(D_MODEL, NSA_KV_HEADS * LANES)
    kcvc = jnp.concatenate([kc.reshape(D_MODEL, NSA_KV_HEADS, HEAD_DIM),
                            vc.reshape(D_MODEL, NSA_KV_HEADS, HEAD_DIM)], axis=-1)
    kvw = KV_PAIRS * LANES
    cols = [q, _pad_cols(ks, kvw), _pad_cols(kw, kvw), _pad_cols(vs, kvw), _pad_cols(vw, kvw),
            kcvc.reshape(D_MODEL, NSA_KV_HEADS * LANES), qm, gl]
    return jnp.concatenate(cols, axis=1).astype(BF16)


def _store_head_slots(ref, z, n_heads, pair, aux):
    lo = _lane_lo(z.shape[0])
    for half, data in enumerate((z, _swap_halves(z))):
        h = 2 * pair + half
        if h < n_heads:
            ref[:, h * LANES:(h + 1) * LANES] = jnp.where(lo, data, aux).astype(ref.dtype)


def _store_value_slots(ref, y, n_heads, pair, both):
    lo = _lane_lo(y.shape[0])
    ys = _swap_halves(y)
    for half in range(2):
        h = 2 * pair + half
        if h >= n_heads:
            continue
        even = jnp.where(lo, y if half == 0 else ys, 1.0)
        odd = jnp.where(lo, 1.0, ys if half == 0 else y)
        if both:
            ref[:, 2 * h * LANES:(2 * h + 1) * LANES] = even.astype(ref.dtype)
            ref[:, (2 * h + 1) * LANES:(2 * h + 2) * LANES] = odd.astype(ref.dtype)
        else:
            ref[:, h * LANES:(h + 1) * LANES] = (even if h % 2 == 0 else odd).astype(ref.dtype)


def _rope_args(cos_ref, sin_ref, rows):
    lane = lax.broadcasted_iota(jnp.int32, (rows, LANES), 1)
    return cos_ref[...], sin_ref[...], (lane & HALF) == 0


def _a_proj_kernel(x_ref, gain_ref, w_ref, cos_ref, sin_ref, gt_ref, ones_ref,
                   qa_ref, ks_ref, kw_ref, vs_ref, vw_ref, kcvc_ref, qm_ref, g_ref, *, seq, tm):
    hn = _rms_rows(x_ref[...], gain_ref[...]).astype(BF16)
    rope = _rope_args(cos_ref, sin_ref, tm)
    lane = lax.broadcasted_iota(jnp.int32, (tm, LANES), 1)
    row = lax.broadcasted_iota(jnp.int32, (tm, LANES), 0)
    tok = (pl.program_id(0) % (seq // tm)) * tm + row
    blk_ind = jnp.where(lane - HEAD_DIM == tok // SEL_BLOCK, 1.0, 0.0)

    yq = _dot(hn, w_ref[:, A_COL_Q:A_COL_KS])
    ms = _head_mean_sq(yq, ones_ref)
    for p in range(NSA_HEADS // 2):
        sl = slice(p * LANES, (p + 1) * LANES)
        z = _pair_norm(yq[:, sl], ms[:, sl], gt_ref[0:1, :], rope) * Q_SCALE
        _store_head_slots(qa_ref, z, NSA_HEADS, p, 0.0)

    yk = _dot(hn, w_ref[:, A_COL_KS:A_COL_VS])
    ms = _head_mean_sq(yk, ones_ref)
    for p in range(KV_PAIRS):
        sl = slice(p * LANES, (p + 1) * LANES)
        z = _pair_norm(yk[:, sl], ms[:, sl], gt_ref[1:2, :], rope)
        _store_head_slots(ks_ref, z, NSA_KV_HEADS, p, blk_ind)
        sl = slice((KV_PAIRS + p) * LANES, (KV_PAIRS + p + 1) * LANES)
        z = _pair_norm(yk[:, sl], ms[:, sl], gt_ref[2:3, :], rope)
        _store_head_slots(kw_ref, z, NSA_KV_HEADS, p, 0.0)

    yv = _dot(hn, w_ref[:, A_COL_VS:A_COL_KCVC])
    for j, ref in enumerate((vs_ref, vw_ref)):
        for p in range(KV_PAIRS):
            c = (j * KV_PAIRS + p) * LANES
            _store_value_slots(ref, yv[:, c:c + LANES], NSA_KV_HEADS, p, True)

    yr = _dot(hn, w_ref[:, A_COL_KCVC:A_COLS])
    kcvc_ref[...] = yr[:, :NSA_KV_HEADS * LANES]
    c = A_COL_QM - A_COL_KCVC
    ym = yr[:, c:c + MEM_W]
    ms = _head_mean_sq(ym, ones_ref)
    for p in range(MEM_HEADS // 2):
        sl = slice(p * LANES, (p + 1) * LANES)
        z = _pair_norm(ym[:, sl], ms[:, sl], gt_ref[3:4, :], None) * Q_SCALE
        _store_head_slots(qm_ref, z, MEM_HEADS, p, 0.0)
    c = A_COL_G - A_COL_KCVC
    g_ref[...] = jax.nn.sigmoid(yr[:, c:c + NSA_KV_HEADS * LANES])


def _head_ones():
    idx = np.arange(2 * LANES) // HEAD_DIM
    return jnp.asarray(idx[:, None] == idx[None, :], BF16)


def _a_proj(x2d, gain, w, cos, sin, gt, seq):
    n = x2d.shape[0]
    tm = PROJ_ROW_TILE
    nseq = seq // tm
    row_spec = lambda width: pl.BlockSpec((tm, width), lambda i: (i, 0))
    tab_spec = pl.BlockSpec((tm, LANES), lambda i: (i % nseq, 0))
    kvw = NSA_KV_HEADS * LANES
    widths = (NSA_HEADS * LANES, kvw, kvw, 2 * kvw, 2 * kvw, kvw, MEM_HEADS * LANES, kvw)
    dtypes = (BF16, BF16, BF16, BF16, BF16, F32, BF16, F32)
    return pl.pallas_call(
        functools.partial(_a_proj_kernel, seq=seq, tm=tm),
        grid=(n // tm,),
        in_specs=[row_spec(D_MODEL), _const_spec((1, D_MODEL)), _const_spec((D_MODEL, A_COLS)),
                  tab_spec, tab_spec, _const_spec((SUBLANES, LANES)), _const_spec((2 * LANES, 2 * LANES))],
        out_specs=[row_spec(wd) for wd in widths],
        out_shape=[jax.ShapeDtypeStruct((n, wd), dt) for wd, dt in zip(widths, dtypes)],
        compiler_params=_cparams(1), name="a_proj",
    )(x2d, gain, w, cos, sin, gt, _head_ones())


def _compress_kernel(x_ref, pos_ref, w1_ref, b1_ref, w2_ref, b2_ref, tab_ref, gt_ref, kc_ref, vc_ref):
    nc = kc_ref.shape[0]
    lo = _lane_lo(nc)
    h = jnp.concatenate([x_ref[pl.ds(j, nc, stride=CMP_STRIDE), :] for j in range(CMP_STRIDE)], axis=1)
    top = _dot((h + pos_ref[0:1, :]).astype(BF16), w1_ref[0])
    bot = _dot((h + pos_ref[1:2, :]).astype(BF16), w1_ref[1])
    pre = top + pltpu.roll(bot, nc - 1, 0) + b1_ref[...]
    hid = jax.nn.gelu(pre).astype(BF16)
    yk = _dot(hid[:, :CMP_HIDDEN], w2_ref[0]) + b2_ref[0:1, :]
    yv = _dot(hid[:, CMP_HIDDEN:], w2_ref[1]) + b2_ref[1:2, :]
    kc_ref[...] = jnp.where(lo, _slot_norm(yk, gt_ref[...], tab_ref[...]), 0.0).astype(BF16)
    vc_ref[:, :LANES] = jnp.where(lo, yv, 1.0).astype(BF16)
    vc_ref[:, LANES:] = jnp.where(lo, 1.0, yv).astype(BF16)


def _compress(kcvc, pos, w1, b1, w2, b2, tab_c, gt):
    b, s, _ = kcvc.shape
    nc = s // CMP_STRIDE
    ospec = lambda width: pl.BlockSpec((None, None, nc, width), lambda b_, g: (b_, g, 0, 0))
    return pl.pallas_call(
        _compress_kernel,
        grid=(b, NSA_KV_HEADS),
        in_specs=[pl.BlockSpec((None, s, LANES), lambda b_, g: (b_, 0, g)),
                  _const_spec(pos.shape), _const_spec(w1.shape), _const_spec(b1.shape),
                  _const_spec(w2.shape), _const_spec(b2.shape), _const_spec(tab_c.shape),
                  _const_spec(gt.shape)],
        out_specs=[ospec(LANES), ospec(2 * LANES)],
        out_shape=[jax.ShapeDtypeStruct((b, NSA_KV_HEADS, nc, LANES), BF16),
                   jax.ShapeDtypeStruct((b, NSA_KV_HEADS, nc, 2 * LANES), BF16)],
        compiler_params=_cparams(2), name="compress",
    )(kcvc, pos, w1, b1, w2, b2, tab_c, gt)


def _prep_compress(cmp_pos, cmp_w1, cmp_b1, cmp_w2, cmp_b2):
    p = cmp_pos.reshape(2, 2, CMP_STRIDE, HEAD_DIM).transpose(1, 2, 0, 3).reshape(2, CMP_STRIDE * LANES)
    w = cmp_w1.reshape(2, 2, CMP_STRIDE, HEAD_DIM, CMP_HIDDEN)
    z = jnp.zeros_like(w[0])
    wk = jnp.concatenate([w[0], z], axis=-1)
    wv = jnp.concatenate([z, w[1]], axis=-1)
    w1 = jnp.concatenate([wk, wv], axis=2).reshape(2, CMP_STRIDE * LANES, 2 * CMP_HIDDEN)
    b1 = jnp.concatenate([cmp_b1[0], cmp_b1[1]])[None]
    w2 = jnp.stack([_slots(cmp_w2[0], 1, "rot"), _slots(cmp_w2[1], 1, "dup")])
    b2 = jnp.stack([_slots(cmp_b2[0][None], 1, "rot")[0], _slots(cmp_b2[1][None], 1, "dup")[0]])
    return p, w1.astype(BF16), b1, w2.astype(BF16), b2


STACK_ORDER = (0, 2, 1, 3)


def _stack_q(q_ref, extra=None):
    parts = []
    for r in STACK_ORDER:
        q = q_ref[:, r * LANES:(r + 1) * LANES]
        parts.append(q if extra is None else q + extra)
    return jnp.concatenate(parts, axis=0)


def _gate_pairs(g_ref, branch, tq):
    lo = _lane_lo(tq)
    col = lambda r: jnp.broadcast_to(g_ref[:, branch * NSA_GQA + r:branch * NSA_GQA + r + 1], (tq, LANES))
    return jnp.concatenate([jnp.where(lo, col(0), col(1)), jnp.where(lo, col(2), col(3))], axis=0)


def _store_pairs(o_ref, out, tq):
    o_ref[:, :LANES] = out[:tq]
    o_ref[:, LANES:] = out[tq:]


def _add_tile_mask(s, mask, tq):
    nk = s.shape[1]
    return (s.reshape(NSA_GQA, tq, nk) + mask).reshape(NSA_GQA * tq, nk)


def _window_branch(i, q_ref, kw_ref, vw_ref, g_ref, wm_ref, tq, ws, nk):
    n_sub = tq // ws
    lo = _lane_lo(2 * ws)
    outs = []
    for j in range(n_sub):
        t = i * n_sub + j
        k0 = pl.multiple_of(jnp.maximum(t * ws + ws - nk, 0), ws)
        q4 = jnp.concatenate([q_ref[j * ws:(j + 1) * ws, r * LANES:(r + 1) * LANES] for r in STACK_ORDER],
                             axis=0)
        mask = wm_ref[jnp.minimum(t, wm_ref.shape[0] - 1)]
        s = _add_tile_mask(_dot_nt(q4, kw_ref[pl.ds(k0, nk), :]), mask, ws)
        m = jnp.max(s, axis=-1, keepdims=True)
        p = jnp.exp2(s - m).astype(BF16)
        num, den = _pair_num_den(_dot(p[:2 * ws], vw_ref[pl.ds(k0, nk), :LANES]),
                                 _dot(p[2 * ws:], vw_ref[pl.ds(k0, nk), LANES:]), lo)
        outs.append(num / den)
    out = jnp.concatenate([x[:ws] for x in outs] + [x[ws:] for x in outs], axis=0)
    return out * _gate_pairs(g_ref, 2, tq)


def _compressed_branch(i, q_ref, kc_ref, vc_ref, g_ref, ov_ref, tq, n_cmp, n_blk, n_sel):
    lo = _lane_lo(tq)
    kc = kc_ref[...]
    nc = kc.shape[0]
    t_row = i * tq + lax.broadcasted_iota(jnp.int32, (1, tq), 1)
    c_col = lax.broadcasted_iota(jnp.int32, (nc, 1), 0)
    cend_col = jnp.where(c_col < n_cmp, c_col * CMP_STRIDE + (CMP_BLOCK - 1), jnp.int32(2 ** 30))
    mask_t = cend_col <= t_row
    psum = jnp.zeros((nc, tq), F32)
    xs = []
    for r in range(NSA_GQA):
        st = jnp.where(mask_t, _dot_nt(kc, q_ref[:, r * LANES:(r + 1) * LANES]), NEG)
        mt = jnp.max(st, axis=0, keepdims=True)
        pt = jnp.exp2(st - mt)
        lt = jnp.sum(pt, axis=0, keepdims=True)
        pt = pt * jnp.where(mt > NEG / 2, 1.0 / lt, 0.0)
        psum = psum + pt
        vsl = slice(0, LANES) if r % 2 == 0 else slice(LANES, 2 * LANES)
        xs.append(_dot(pt.T.astype(BF16), vc_ref[:, vsl]))
    out_c = jnp.concatenate([jnp.where(lo, xs[0], xs[1]), jnp.where(lo, xs[2], xs[3])], axis=0)
    out_c = out_c * _gate_pairs(g_ref, 0, tq)

    ps_hi = psum.astype(BF16)
    ps_lo = (psum - ps_hi.astype(F32)).astype(BF16)
    ov = ov_ref[...]
    imp = _dot(ov, ps_hi) + _dot(ov, ps_lo)

    cur = t_row // SEL_BLOCK
    ngrp = n_blk // SUBLANES
    jsub = lax.broadcasted_iota(jnp.int32, (SUBLANES, tq), 0)
    vals = []
    for a in range(ngrp):
        j = jsub + a * SUBLANES
        forced = (j == 0) | (j == cur) | (j == cur - 1)
        v = jnp.where(j <= cur, imp[a * SUBLANES:(a + 1) * SUBLANES], NEG)
        vals.append(jnp.where(forced, SEL_FORCE, v))
    ranks = [jnp.zeros((SUBLANES, tq), F32) for _ in range(ngrp)]
    for jj in range(n_blk):
        a0, s0 = divmod(jj, SUBLANES)
        rowb = jnp.broadcast_to(vals[a0][s0:s0 + 1, :], (SUBLANES, tq))
        for a in range(ngrp):
            if a > a0:
                beats = jnp.where(rowb >= vals[a], 1.0, 0.0)
            elif a < a0:
                beats = jnp.where(rowb > vals[a], 1.0, 0.0)
            else:
                beats = jnp.where(jsub > s0, jnp.where(rowb >= vals[a], 1.0, 0.0),
                                  jnp.where(rowb > vals[a], 1.0, 0.0))
            ranks[a] = ranks[a] + beats
    parts = [jnp.zeros((HEAD_DIM, tq), F32)]
    for a in range(ngrp):
        live = jnp.where(vals[a] > NEG / 2, 0.0, NEG)
        parts.append(jnp.where(ranks[a] < n_sel, live, NEG))
    if n_blk < HEAD_DIM:
        parts.append(jnp.zeros((HEAD_DIM - n_blk, tq), F32))
    bias_t = jnp.concatenate(parts, axis=0)
    return out_c, bias_t.T.astype(BF16)


def _band_masks(n_var, tq, nk, k0_of, lo_dist, hi_dist):
    r = np.arange(tq)[None, :, None]
    c = np.arange(nk)[None, None, :]
    v = np.arange(n_var)[:, None, None]
    dist = v * tq + r - (np.asarray([k0_of(x) for x in range(n_var)])[:, None, None] + c)
    return jnp.asarray(np.where((dist >= lo_dist) & (dist <= hi_dist), 0.0, NEG), F32)


def _selected_branch(i, q_ref, bias, k_ref, v_ref, g_ref, dm_ref, m_ref, acc_ref, tq, tk):
    rows = NSA_GQA * tq
    half = 2 * tq
    q4 = _stack_q(q_ref, bias)
    m_ref[...] = jnp.full((rows, LANES), NEG, F32)
    acc_ref[...] = jnp.zeros((rows, LANES), F32)
    rep = tk // LANES

    def step(kt, masked):
        k0 = pl.multiple_of(kt * tk, tk)
        s = _dot_nt(q4, k_ref[pl.ds(k0, tk), :])
        if masked:
            s = _add_tile_mask(s, dm_ref[...], tq)
        m_prev = m_ref[...]
        m_next = jnp.maximum(m_prev, jnp.max(s, axis=-1, keepdims=True))
        p = jnp.exp2(s - jnp.tile(m_next, (1, rep))).astype(BF16)
        alpha = jnp.exp2(m_prev - m_next)
        m_ref[...] = m_next
        acc_ref[:half] = acc_ref[:half] * alpha[:half] + _dot(p[:half], v_ref[pl.ds(k0, tk), :LANES])
        acc_ref[half:] = acc_ref[half:] * alpha[half:] + _dot(p[half:], v_ref[pl.ds(k0, tk), LANES:])

    kt_diag = (i * tq) // tk
    lax.fori_loop(0, kt_diag, lambda kt, c: (step(kt, False), c)[1], 0)
    step(kt_diag, True)
    num, den = _pair_num_den(acc_ref[:half], acc_ref[half:], _lane_lo(half))
    return num / den * _gate_pairs(g_ref, 1, tq)


def _nsa_kernel(q_ref, kc_ref, vc_ref, kw_ref, vw_ref, ks_ref, vs_ref, g_ref, ov_ref, wm_ref, dm_ref,
                o_ref, m_ref, acc_ref, *, tq, tk, ws, nk, n_cmp, n_blk, n_sel):
    i = pl.program_id(2)
    out_w = _window_branch(i, q_ref, kw_ref, vw_ref, g_ref, wm_ref, tq, ws, nk)
    out_c, bias = _compressed_branch(i, q_ref, kc_ref, vc_ref, g_ref, ov_ref, tq, n_cmp, n_blk, n_sel)
    _store_pairs(o_ref, out_c + out_w, tq)
    out_s = _selected_branch(i, q_ref, bias, ks_ref, vs_ref, g_ref, dm_ref, m_ref, acc_ref, tq, tk)
    o_ref[:, :LANES] += out_s[:tq]
    o_ref[:, LANES:] += out_s[tq:]


def _nsa_attention(qa, kc, vc, kw, vw, ks, vs, gates, ov, n_cmp, n_sel):
    b, s, _ = qa.shape
    tq, tk = Q_TILE, min(SEL_KEY_TILE, s)
    ws = WIN_SUB_TILE
    nk = min(WINDOW + ws, s)
    nc = kc.shape[2]
    n_blk = s // SEL_BLOCK
    gw = NSA_GQA * LANES
    n_early = (nk - ws) // ws
    wmask = _band_masks(n_early + 1, ws, nk, lambda v: max(v * ws + ws - nk, 0), 0, WINDOW - 1)
    n_var = tk // tq
    dmask = _band_masks(n_var, tq, tk, lambda v: 0, 0, tk)
    kern = functools.partial(_nsa_kernel, tq=tq, tk=tk, ws=ws, nk=nk, n_cmp=n_cmp, n_blk=n_blk,
                             n_sel=n_sel)
    k_spec = pl.BlockSpec((None, s, LANES), lambda b_, g, i: (b_, 0, g))
    v_spec = pl.BlockSpec((None, s, 2 * LANES), lambda b_, g, i: (b_, 0, g))
    return pl.pallas_call(
        kern,
        grid=(b, NSA_KV_HEADS, s // tq),
        in_specs=[pl.BlockSpec((None, tq, gw), lambda b_, g, i: (b_, i, g)),
                  pl.BlockSpec((None, None, nc, LANES), lambda b_, g, i: (b_, g, 0, 0)),
                  pl.BlockSpec((None, None, nc, 2 * LANES), lambda b_, g, i: (b_, g, 0, 0)),
                  k_spec, v_spec, k_spec, v_spec,
                  pl.BlockSpec((None, tq, LANES), lambda b_, g, i: (b_, i, g)),
                  _const_spec(ov.shape),
                  _const_spec(wmask.shape),
                  pl.BlockSpec((None, tq, tk), lambda b_, g, i: (i % n_var, 0, 0))],
        out_specs=pl.BlockSpec((None, tq, 2 * LANES), lambda b_, g, i: (b_, i, g)),
        out_shape=jax.ShapeDtypeStruct((b, s, NSA_KV_HEADS * 2 * LANES), F32),
        scratch_shapes=[pltpu.VMEM((NSA_GQA * tq, LANES), F32), pltpu.VMEM((NSA_GQA * tq, LANES), F32)],
        compiler_params=_cparams(3), name="nsa_attention",
    )(qa, kc, vc, kw, vw, ks, vs, gates, ov, wmask, dmask)


def _mem_kv_kernel(mem_ref, gain_ref, w_ref, gt_ref, ones_ref, k_ref, v_ref):
    hn = _rms_rows(mem_ref[...], gain_ref[...]).astype(BF16)
    y = _dot(hn, w_ref[...])
    ms = _head_mean_sq(y[:, :MEM_W], ones_ref)
    for p in range(MEM_HEADS // 2):
        sl = slice(p * LANES, (p + 1) * LANES)
        z = _pair_norm(y[:, sl], ms[:, sl], gt_ref[...], None)
        _store_head_slots(k_ref, z, MEM_HEADS, p, 0.0)
        c = MEM_W + p * LANES
        _store_value_slots(v_ref, y[:, c:c + LANES], MEM_HEADS, p, False)


def _mem_kv(mem, gains, w, gt):
    b, nm, _ = mem.shape
    nl = gains.shape[0]
    width = MEM_HEADS * LANES
    out_spec = pl.BlockSpec((None, None, nm, width), lambda l, b_: (l, b_, 0, 0))
    return pl.pallas_call(
        _mem_kv_kernel,
        grid=(nl, b),
        in_specs=[pl.BlockSpec((None, nm, D_MODEL), lambda l, b_: (b_, 0, 0)),
                  pl.BlockSpec((None, 1, D_MODEL), lambda l, b_: (l, 0, 0)),
                  pl.BlockSpec((None, D_MODEL, 2 * MEM_W), lambda l, b_: (l, 0, 0)),
                  pl.BlockSpec((None, 1, LANES), lambda l, b_: (l, 0, 0)),
                  _const_spec((2 * LANES, 2 * LANES))],
        out_specs=[out_spec, out_spec],
        out_shape=[jax.ShapeDtypeStruct((nl, b, nm, width), BF16)] * 2,
        compiler_params=_cparams(2), name="mem_kv",
    )(mem, gains, w, gt, _head_ones())


def _mem_attend(q_ref, k_ref, v_ref):
    lo = _lane_lo(q_ref.shape[0])
    accs = []
    for h in range(MEM_HEADS):
        sl = slice(h * LANES, (h + 1) * LANES)
        s = _dot_nt(q_ref[:, sl], k_ref[:, sl])
        m = jnp.max(s, axis=-1, keepdims=True)
        accs.append(_dot(jnp.exp2(s - m).astype(BF16), v_ref[:, sl]))
    outs = []
    for pair in range(MEM_HEADS // 2):
        num, den = _pair_num_den(accs[2 * pair], accs[2 * pair + 1], lo)
        outs.append(num / den)
    return jnp.concatenate(outs, axis=1)


def _out_mlp_kernel(o_ref, qm_ref, mk_ref, mv_ref, x_ref, wo_ref, gain_ref, wup_ref, wdn_ref, out_ref):
    km = o_ref.shape[1]
    o_mem = _mem_attend(qm_ref, mk_ref, mv_ref).astype(BF16)
    x1 = x_ref[...] + _dot(o_ref[...].astype(BF16), wo_ref[:km, :]) + _dot(o_mem, wo_ref[km:, :])
    hn = _rms_rows(x1, gain_ref[...]).astype(BF16)
    out_ref[...] = x1
    for c in range(D_FF // FF_CHUNK):
        u = _dot(hn, wup_ref[:, c * FF_CHUNK:(c + 1) * FF_CHUNK])
        u = jnp.square(jnp.maximum(u, 0.0)).astype(BF16)
        out_ref[...] += _dot(u, wdn_ref[c * FF_CHUNK:(c + 1) * FF_CHUNK, :])


def _out_mlp(o_main, qm, mk, mv, layer, seq, x2d, wo, gain, wup, wdn, name):
    n = x2d.shape[0]
    tm = ROW_TILE
    nseq = seq // tm
    nm, mw = mk.shape[2], mk.shape[3]
    row_spec = lambda width: pl.BlockSpec((tm, width), lambda i: (i, 0))
    kv_spec = pl.BlockSpec((None, None, nm, mw), lambda i: (layer, i // nseq, 0, 0))
    return pl.pallas_call(
        _out_mlp_kernel,
        grid=(n // tm,),
        in_specs=[row_spec(o_main.shape[1]), row_spec(qm.shape[1]), kv_spec, kv_spec, row_spec(D_MODEL),
                  _const_spec(wo.shape), _const_spec((1, D_MODEL)), _const_spec(wup.shape),
                  _const_spec(wdn.shape)],
        out_specs=row_spec(D_MODEL),
        out_shape=jax.ShapeDtypeStruct((n, D_MODEL), F32),
        compiler_params=_cparams(1), name=name,
    )(o_main, qm, mk, mv, x2d, wo, gain, wup, wdn)


def _b_proj_kernel(x_ref, gains_ref, wq_ref, wkv_ref, cos_ref, sin_ref, gt_ref, ones_ref,
                   q0_ref, q1_ref, q2_ref, qm_ref, k_ref, v_ref, *, tm):
    x = x_ref[...]
    xn = x * lax.rsqrt(jnp.mean(x * x, axis=-1, keepdims=True) + EPS)
    rope = _rope_args(cos_ref, sin_ref, tm)
    hq = (xn * gains_ref[0:1, :]).astype(BF16)
    width = DIL_SLOTS * HEAD_DIM
    for gi, ref in enumerate((q0_ref, q1_ref, q2_ref)):
        y = _dot(hq, wq_ref[:, gi * width:(gi + 1) * width])
        ms = _head_mean_sq(y, ones_ref)
        for p in range(DIL_SLOTS // 2):
            sl = slice(p * LANES, (p + 1) * LANES)
            z = _pair_norm(y[:, sl], ms[:, sl], gt_ref[gi:gi + 1, :], rope) * Q_SCALE
            _store_head_slots(ref, z, DIL_SLOTS, p, 0.0)
    y = _dot(hq, wq_ref[:, B_Q:B_Q + MEM_W])
    ms = _head_mean_sq(y, ones_ref)
    for p in range(MEM_HEADS // 2):
        sl = slice(p * LANES, (p + 1) * LANES)
        z = _pair_norm(y[:, sl], ms[:, sl], gt_ref[3:4, :], None) * Q_SCALE
        _store_head_slots(qm_ref, z, MEM_HEADS, p, 0.0)
    hk = (xn * gains_ref[1:2, :]).astype(BF16)
    y = _dot(hk, wkv_ref[...])
    ms = _head_mean_sq(y[:, :width], ones_ref)
    for p in range(DIL_SLOTS // 2):
        sl = slice(p * LANES, (p + 1) * LANES)
        z = _pair_norm(y[:, sl], ms[:, sl], gt_ref[4:5, :], rope)
        _store_head_slots(k_ref, z, DIL_SLOTS, p, 0.0)
        c = width + p * LANES
        _store_value_slots(v_ref, y[:, c:c + LANES], DIL_SLOTS, p, False)


def _b_proj(x2d, gains, wq, wkv, cos, sin, gt, seq):
    n = x2d.shape[0]
    tm = PROJ_ROW_TILE
    nseq = seq // tm
    row_spec = lambda width: pl.BlockSpec((tm, width), lambda i: (i, 0))
    tab_spec = pl.BlockSpec((tm, LANES), lambda i: (i % nseq, 0))
    width = DIL_SLOTS * LANES
    widths = (width, width, width, MEM_HEADS * LANES, width, width)
    return pl.pallas_call(
        functools.partial(_b_proj_kernel, tm=tm),
        grid=(n // tm,),
        in_specs=[row_spec(D_MODEL), _const_spec(gains.shape), _const_spec(wq.shape),
                  _const_spec(wkv.shape), tab_spec, tab_spec, _const_spec(gt.shape),
                  _const_spec((2 * LANES, 2 * LANES))],
        out_specs=[row_spec(wd) for wd in widths],
        out_shape=[jax.ShapeDtypeStruct((n, wd), BF16) for wd in widths],
        compiler_params=_cparams(1), name="b_proj",
    )(x2d, gains, wq, wkv, cos, sin, gt, _head_ones())


def _dil_kernel(q0_ref, q1_ref, q2_ref, k_ref, v_ref, o_ref, qf_ref, kf_ref, vf_ref, lse_ref, *, seq):
    tq = DIL_Q_TILE
    heads = (slice(0, LANES), slice(LANES, 2 * LANES))
    for hd, sl in enumerate(heads):
        kf_ref[hd] = k_ref[:, sl].astype(F32)
        vf_ref[hd] = v_ref[:, sl].astype(F32)

    def attend(qs, ks, vs, qpos0, kpos0):
        nq, nk = qs[0].shape[0], ks[0].shape[0]
        lo = _lane_lo(nq)
        dist = (qpos0 + lax.broadcasted_iota(jnp.int32, (nq, 1), 0)
                - (kpos0 + lax.broadcasted_iota(jnp.int32, (1, nk), 1)))
        mask = (dist >= 0) & (dist <= DIL_BAND)
        accs, ms = [], []
        for q, k, v in zip(qs, ks, vs):
            s = jnp.where(mask, _dot_nt(q, k), NEG)
            m = jnp.max(s, axis=-1, keepdims=True)
            accs.append(_dot(jnp.exp2(s - m).astype(BF16), v))
            ms.append(jnp.broadcast_to(m, (nq, LANES)))
        num, den = _pair_num_den(accs[0], accs[1], lo)
        return num / den, jnp.where(lo, ms[0], ms[1]) + jnp.log2(den)

    def merge(o_old, l_old, o_new, l_new):
        mx = jnp.maximum(l_old, l_new)
        a = jnp.exp2(l_old - mx)
        b = jnp.exp2(l_new - mx)
        den = a + b
        return (a * o_old + b * o_new) / den, mx + jnp.log2(den)

    nk = min(tq + DIL_BAND, seq)

    def body0(i, c):
        q0 = pl.multiple_of(i * tq, tq)
        k0 = pl.multiple_of(jnp.clip(i * tq - DIL_BAND, 0, seq - nk), DIL_BAND)
        o, l = attend([q0_ref[pl.ds(q0, tq), sl] for sl in heads],
                      [k_ref[pl.ds(k0, nk), sl] for sl in heads],
                      [v_ref[pl.ds(k0, nk), sl] for sl in heads], q0, k0)
        o_ref[pl.ds(q0, tq), :] = o
        lse_ref[pl.ds(q0, tq), :] = l
        return c

    lax.fori_loop(0, seq // tq, body0, 0, unroll=DIL_UNROLL)

    for q_ref, (_, dil) in zip((q1_ref, q2_ref), DIL_PATTERNS[1:]):
        for hd, sl in enumerate(heads):
            qf_ref[hd] = q_ref[:, sl].astype(F32)
        length = seq // dil
        tqd = min(tq if dil < 2 * SUBLANES else DIL_Q_TILE_WIDE, length)
        nkd = min(tqd + DIL_BAND, length)
        ntile = length // tqd

        def body(it, c, dil=dil, length=length, tqd=tqd, nkd=nkd, ntile=ntile):
            r = it // ntile
            i = it - r * ntile
            qp = i * tqd
            kp = jnp.clip(qp - DIL_BAND, 0, length - nkd)
            qrows = pl.ds(r + dil * qp, tqd, stride=dil)
            krows = pl.ds(r + dil * kp, nkd, stride=dil)
            o, l = attend([qf_ref[hd, qrows, :].astype(BF16) for hd in range(2)],
                          [kf_ref[hd, krows, :].astype(BF16) for hd in range(2)],
                          [vf_ref[hd, krows, :].astype(BF16) for hd in range(2)], qp, kp)
            o, l = merge(o_ref[qrows, :], lse_ref[qrows, :], o, l)
            o_ref[qrows, :] = o
            lse_ref[qrows, :] = l
            return c

        lax.fori_loop(0, dil * ntile, body, 0, unroll=DIL_UNROLL)


def _dil_attention(q0, q1, q2, k, v):
    b, s, _ = k.shape
    npair = DIL_SLOTS // 2
    pw = 2 * LANES
    in_spec = pl.BlockSpec((None, s, pw), lambda b_, hp: (b_, 0, hp))
    return pl.pallas_call(
        functools.partial(_dil_kernel, seq=s),
        grid=(b, npair),
        in_specs=[in_spec] * 5,
        out_specs=pl.BlockSpec((None, s, LANES), lambda b_, hp: (b_, 0, hp)),
        out_shape=jax.ShapeDtypeStruct((b, s, DIL_SLOTS * HEAD_DIM), F32),
        scratch_shapes=[pltpu.VMEM((2, s, LANES), F32)] * 3 + [pltpu.VMEM((s, LANES), F32)],
        compiler_params=_cparams(2), name="dil_attention",
    )(q0, q1, q2, k, v)


def kernel(x, mem, attn_norm, mlp_norm, w_up, w_down, mem_norm, w_mem_kv, mem_q_norm, mem_k_norm,
           a_w_in, a_w_out, a_q_norm, a_k_norm, a_cmp_pos, a_cmp_w1, a_cmp_b1, a_cmp_w2, a_cmp_b2,
           kv_norm, w_kv_shared, kv_k_norm, b_w_in, b_w_out, b_q_norm):
    b, s, _ = x.shape
    n = b * s
    assert s % PROJ_ROW_TILE == 0 and s % ROW_TILE == 0 and s % SEL_KEY_TILE == 0
    assert s // SEL_BLOCK <= HEAD_DIM
    assert s >= WINDOW + WIN_SUB_TILE and s % Q_TILE == 0 and Q_TILE % WIN_SUB_TILE == 0
    assert (s // DIL_PATTERNS[-1][1]) % DIL_BAND == 0
    x2d = x.reshape(n, D_MODEL)
    cos, sin = _rope_pair_tables(jnp.arange(s))

    mk, mv = _mem_kv(mem, mem_norm[:, None, :], w_mem_kv.astype(BF16), _gain_pair(mem_k_norm)[:, None, :])

    gt_a = jnp.zeros((SUBLANES, LANES), F32)
    gt_a = gt_a.at[0].set(_gain_pair(a_q_norm[0])).at[1].set(_gain_pair(a_k_norm[0, 1]))
    gt_a = gt_a.at[2].set(_gain_pair(a_k_norm[0, 2])).at[3].set(_gain_pair(mem_q_norm[0]))
    qa, ks, kw, vs, vw, kcvc, qm, gates = _a_proj(
        x2d, attn_norm[0:1], _prep_a_w_in(a_w_in[0]), cos, sin, gt_a, s)

    n_cmp = (s - CMP_BLOCK) // CMP_STRIDE + 1
    nc = s // CMP_STRIDE
    tab_c = _rope_table(jnp.arange(nc) * CMP_STRIDE + (CMP_BLOCK - 1))
    kc, vc = _compress(kcvc.reshape(b, s, -1),
                       *_prep_compress(a_cmp_pos[0], a_cmp_w1[0], a_cmp_b1[0], a_cmp_w2[0], a_cmp_b2[0]),
                       tab_c, _gain_slot(a_k_norm[0, 0], "rot")[None])

    n_blk = s // SEL_BLOCK
    c_start = np.arange(nc)[None, :] * CMP_STRIDE
    b_start = np.arange(n_blk)[:, None] * SEL_BLOCK
    overlap_t = ((c_start < b_start + SEL_BLOCK) & (c_start + CMP_BLOCK > b_start)
                 & (np.arange(nc)[None, :] < n_cmp))
    ov = jnp.asarray(overlap_t, BF16)

    qa3 = qa.reshape(b, s, -1)
    gates3 = gates.reshape(b, s, -1)
    o_a = _nsa_attention(qa3, kc, vc, kw.reshape(b, s, -1), vw.reshape(b, s, -1), ks.reshape(b, s, -1),
                         vs.reshape(b, s, -1), gates3, ov, n_cmp, min(SEL_TOPK, n_blk))
    x2d = _out_mlp(o_a.reshape(n, -1), qm, mk, mv, 0, s, x2d, a_w_out[0].astype(BF16), mlp_norm[0:1],
                   w_up[0].astype(BF16), w_down[0].astype(BF16), "a_out_mlp")

    gt_b = jnp.zeros((SUBLANES, LANES), F32)
    for gi in range(N_DIL_GROUPS):
        gt_b = gt_b.at[gi].set(_gain_pair(b_q_norm[0, gi]))
    gt_b = gt_b.at[3].set(_gain_pair(mem_q_norm[1])).at[4].set(_gain_pair(kv_k_norm))
    q0, q1, q2, qm1, kb, vb = _b_proj(x2d, jnp.stack([attn_norm[1], kv_norm]), b_w_in[0].astype(BF16),
                                      w_kv_shared.astype(BF16), cos, sin, gt_b, s)
    o_d = _dil_attention(*(t.reshape(b, s, -1) for t in (q0, q1, q2, kb, vb)))
    x2d = _out_mlp(o_d.reshape(n, -1), qm1, mk, mv, 1, s, x2d, b_w_out[0].astype(BF16), mlp_norm[1:2],
                   w_up[1].astype(BF16), w_down[1].astype(BF16), "b_out_mlp")
    return x2d.reshape(b, s, D_MODEL)
```

```python
import functools

import numpy as np
import jax
import jax.numpy as jnp
from jax import lax
from jax.experimental import pallas as pl
from jax.experimental.pallas import tpu as pltpu

D_MODEL = 1024
HEAD_DIM = 64
HALF = HEAD_DIM // 2
ROPE_THETA = 10000.0
EPS = 1e-6
NEG = -1e30
D_FF = 4 * D_MODEL
MEM_HEADS = 4
NSA_HEADS = 12
NSA_KV_HEADS = 3
NSA_GQA = NSA_HEADS // NSA_KV_HEADS
CMP_BLOCK = 32
CMP_STRIDE = 16
CMP_HIDDEN = 256
SEL_BLOCK = 64
SEL_TOPK = 16
WINDOW = 512
SEL_FORCE = 1e9
DIL_PATTERNS = ((128, 1), (512, 4), (2048, 16))
N_DIL_GROUPS = 3
DIL_SLOTS = 8
MEM_W = MEM_HEADS * HEAD_DIM
A_Q = NSA_HEADS * HEAD_DIM
A_KV = NSA_KV_HEADS * HEAD_DIM
B_Q = N_DIL_GROUPS * DIL_SLOTS * HEAD_DIM
LOG2E = 1.4426950408889634
Q_SCALE = HEAD_DIM ** -0.5 * LOG2E

LANES = 128
SUBLANES = 8
VMEM_LIMIT = 56 * 1024 * 1024
ROW_TILE = 512
PROJ_ROW_TILE = 1024
Q_TILE = 512
WIN_SUB_TILE = 256
DIL_UNROLL = 4
SEL_KEY_TILE = 512
DIL_Q_TILE = 128
DIL_Q_TILE_WIDE = 256
DIL_BAND = 128
FF_CHUNK = 1024

BF16 = jnp.bfloat16
F32 = jnp.float32


def _cparams(n_grid):
    return pltpu.CompilerParams(dimension_semantics=("arbitrary",) * n_grid,
                                vmem_limit_bytes=VMEM_LIMIT)


def _const_spec(shape):
    nd = len(shape)
    return pl.BlockSpec(shape, lambda *_: (0,) * nd, pipeline_mode=pl.Buffered(1))


def _dot(a, b):
    return jnp.dot(a, b, preferred_element_type=F32)


def _dot_nt(a, b):
    return lax.dot_general(a, b, (((1,), (1,)), ((), ())), preferred_element_type=F32)


def _rms_rows(x, gain):
    return x * lax.rsqrt(jnp.mean(x * x, axis=-1, keepdims=True) + EPS) * gain


def _slot_norm(y, gt, tab):
    z = y * lax.rsqrt(jnp.mean(y * y, axis=-1, keepdims=True) + EPS) * gt
    if tab is not None:
        z = z * tab
        z = z + pltpu.roll(z, HEAD_DIM, 1)
    return z


def _head_mean_sq(y, ones_ref):
    sq = (y * y).astype(BF16)
    width = y.shape[1]
    step = ones_ref.shape[0]
    parts = []
    for c in range(0, width, step):
        cw = min(step, width - c)
        parts.append(_dot(sq[:, c:c + cw], ones_ref[:cw, :cw]))
    return jnp.concatenate(parts, axis=1) * (1.0 / HEAD_DIM)


def _pair_norm(y, ms, gain, rope):
    z = y * lax.rsqrt(ms + EPS) * gain
    if rope is not None:
        cos, sin_signed, first = rope
        partner = jnp.where(first, pltpu.roll(z, LANES - HALF, 1), pltpu.roll(z, HALF, 1))
        z = z * cos + partner * sin_signed
    return z


def _lane_lo(rows):
    return lax.broadcasted_iota(jnp.int32, (rows, LANES), 1) < HEAD_DIM


def _swap_halves(x):
    return pltpu.roll(x, HEAD_DIM, 1)


def _pair_num_den(acc_even, acc_odd, lo):
    return jnp.where(lo, acc_even, acc_odd), _swap_halves(jnp.where(lo, acc_odd, acc_even))


def _rot_half_cols(w):
    return jnp.concatenate([-w[..., HALF:], w[..., :HALF]], axis=-1)


def _swap_half(g):
    return jnp.concatenate([g[..., HALF:], g[..., :HALF]], axis=-1)


def _slots(w, n, kind):
    k = w.shape[0]
    w = w.reshape(k, n, HEAD_DIM)
    other = _rot_half_cols(w) if kind == "rot" else w
    return jnp.concatenate([w, other], axis=-1).reshape(k, n * LANES)


def _gain_slot(g, kind):
    other = _swap_half(g) if kind == "rot" else g
    return jnp.concatenate([g, other], axis=-1)


def _gain_pair(g):
    return jnp.concatenate([g, g], axis=-1)


def _rope_cos_sin(pos):
    inv_freq = ROPE_THETA ** (-jnp.arange(HALF, dtype=F32) / HALF)
    ang = jnp.asarray(pos, F32)[:, None] * inv_freq[None, :]
    return jnp.cos(ang), jnp.sin(ang)


def _rope_table(pos):
    cos, sin = _rope_cos_sin(pos)
    return jnp.concatenate([cos, cos, sin, sin], axis=-1)


def _rope_pair_tables(pos):
    cos, sin = _rope_cos_sin(pos)
    return (jnp.concatenate([cos, cos, cos, cos], axis=-1),
            jnp.concatenate([-sin, sin, -sin, sin], axis=-1))


def _pad_cols(w, width):
    return jnp.pad(w, ((0, 0), (0, width - w.shape[1])))


KV_PAIRS = (NSA_KV_HEADS + 1) // 2
A_COL_Q = 0
A_COL_KS = A_COL_Q + A_Q
A_COL_KW = A_COL_KS + KV_PAIRS * LANES
A_COL_VS = A_COL_KW + KV_PAIRS * LANES
A_COL_VW = A_COL_VS + KV_PAIRS * LANES
A_COL_KCVC = A_COL_VW + KV_PAIRS * LANES
A_COL_QM = A_COL_KCVC + NSA_KV_HEADS * LANES
A_COL_G = A_COL_QM + MEM_W
A_COLS = A_COL_G + NSA_KV_HEADS * LANES


def _prep_a_w_in(w):
    o = 0
    q = w[:, o:o + A_Q]; o += A_Q
    kc = w[:, o:o + A_KV]; o += A_KV
    vc = w[:, o:o + A_KV]; o += A_KV
    ks = w[:, o:o + A_KV]; o += A_KV
    vs = w[:, o:o + A_KV]; o += A_KV
    kw = w[:, o:o + A_KV]; o += A_KV
    vw = w[:, o:o + A_KV]; o += A_KV
    qm = w[:, o:o + MEM_W]; o += MEM_W
    gl = w[:, o:]
    gl = gl.reshape(D_MODEL, NSA_KV_HEADS, NSA_GQA, 3).transpose(0, 1, 3, 2)
    gl = gl.reshape(D_MODEL, NSA_KV_HEADS, 3 * NSA_GQA)
    gl = jnp.pad(gl, ((0, 0), (0, 0), (0, LANES - 3 * NSA_GQA))).reshape(D_MODEL, NSA_KV_HEADS * LANES)
    kcvc = jnp.concatenate([kc.reshape(D_MODEL, NSA_KV_HEADS, HEAD_DIM),
                            vc.reshape(D_MODEL, NSA_KV_HEADS, HEAD_DIM)], axis=-1)
    kvw = KV_PAIRS * LANES
    cols = [q, _pad_cols(ks, kvw), _pad_cols(kw, kvw), _pad_cols(vs, kvw), _pad_cols(vw, kvw),
            kcvc.reshape(D_MODEL, NSA_KV_HEADS * LANES), qm, gl]
    return jnp.concatenate(cols, axis=1).astype(BF16)


def _store_head_slots(ref, z, n_heads, pair, aux):
    lo = _lane_lo(z.shape[0])
    for half, data in enumerate((z, _swap_halves(z))):
        h = 2 * pair + half
        if h < n_heads:
            ref[:, h * LANES:(h + 1) * LANES] = jnp.where(lo, data, aux).astype(ref.dtype)


def _store_value_slots(ref, y, n_heads, pair, both):
    lo = _lane_lo(y.shape[0])
    ys = _swap_halves(y)
    for half in range(2):
        h = 2 * pair + half
        if h >= n_heads:
            continue
        even = jnp.where(lo, y if half == 0 else ys, 1.0)
        odd = jnp.where(lo, 1.0, ys if half == 0 else y)
        if both:
            ref[:, 2 * h * LANES:(2 * h + 1) * LANES] = even.astype(ref.dtype)
            ref[:, (2 * h + 1) * LANES:(2 * h + 2) * LANES] = odd.astype(ref.dtype)
        else:
            ref[:, h * LANES:(h + 1) * LANES] = (even if h % 2 == 0 else odd).astype(ref.dtype)


def _rope_args(cos_ref, sin_ref, rows):
    lane = lax.broadcasted_iota(jnp.int32, (rows, LANES), 1)
    return cos_ref[...], sin_ref[...], (lane & HALF) == 0


def _a_proj_kernel(x_ref, gain_ref, w_ref, cos_ref, sin_ref, gt_ref, ones_ref,
                   qa_ref, ks_ref, kw_ref, vs_ref, vw_ref, kcvc_ref, qm_ref, g_ref, *, seq, tm):
    hn = _rms_rows(x_ref[...], gain_ref[...]).astype(BF16)
    rope = _rope_args(cos_ref, sin_ref, tm)
    lane = lax.broadcasted_iota(jnp.int32, (tm, LANES), 1)
    row = lax.broadcasted_iota(jnp.int32, (tm, LANES), 0)
    tok = (pl.program_id(0) % (seq // tm)) * tm + row
    blk_ind = jnp.where(lane - HEAD_DIM == tok // SEL_BLOCK, 1.0, 0.0)

    yq = _dot(hn, w_ref[:, A_COL_Q:A_COL_KS])
    ms = _head_mean_sq(yq, ones_ref)
    for p in range(NSA_HEADS // 2):
        sl = slice(p * LANES, (p + 1) * LANES)
        z = _pair_norm(yq[:, sl], ms[:, sl], gt_ref[0:1, :], rope) * Q_SCALE
        _store_head_slots(qa_ref, z, NSA_HEADS, p, 0.0)

    yk = _dot(hn, w_ref[:, A_COL_KS:A_COL_VS])
    ms = _head_mean_sq(yk, ones_ref)
    for p in range(KV_PAIRS):
        sl = slice(p * LANES, (p + 1) * LANES)
        z = _pair_norm(yk[:, sl], ms[:, sl], gt_ref[1:2, :], rope)
        _store_head_slots(ks_ref, z, NSA_KV_HEADS, p, blk_ind)
        sl = slice((KV_PAIRS + p) * LANES, (KV_PAIRS + p + 1) * LANES)
        z = _pair_norm(yk[:, sl], ms[:, sl], gt_ref[2:3, :], rope)
        _store_head_slots(kw_ref, z, NSA_KV_HEADS, p, 0.0)

    yv = _dot(hn, w_ref[:, A_COL_VS:A_COL_KCVC])
    for j, ref in enumerate((vs_ref, vw_ref)):
        for p in range(KV_PAIRS):
            c = (j * KV_PAIRS + p) * LANES
            _store_value_slots(ref, yv[:, c:c + LANES], NSA_KV_HEADS, p, True)

    yr = _dot(hn, w_ref[:, A_COL_KCVC:A_COLS])
    kcvc_ref[...] = yr[:, :NSA_KV_HEADS * LANES]
    c = A_COL_QM - A_COL_KCVC
    ym = yr[:, c:c + MEM_W]
    ms = _head_mean_sq(ym, ones_ref)
    for p in range(MEM_HEADS // 2):
        sl = slice(p * LANES, (p + 1) * LANES)
        z = _pair_norm(ym[:, sl], ms[:, sl], gt_ref[3:4, :], None) * Q_SCALE
        _store_head_slots(qm_ref, z, MEM_HEADS, p, 0.0)
    c = A_COL_G - A_COL_KCVC
    g_ref[...] = jax.nn.sigmoid(yr[:, c:c + NSA_KV_HEADS * LANES])


def _head_ones():
    idx = np.arange(2 * LANES) // HEAD_DIM
    return jnp.asarray(idx[:, None] == idx[None, :], BF16)


def _a_proj(x2d, gain, w, cos, sin, gt, seq):
    n = x2d.shape[0]
    tm = PROJ_ROW_TILE
    nseq = seq // tm
    row_spec = lambda width: pl.BlockSpec((tm, width), lambda i: (i, 0))
    tab_spec = pl.BlockSpec((tm, LANES), lambda i: (i % nseq, 0))
    kvw = NSA_KV_HEADS * LANES
    widths = (NSA_HEADS * LANES, kvw, kvw, 2 * kvw, 2 * kvw, kvw, MEM_HEADS * LANES, kvw)
    dtypes = (BF16, BF16, BF16, BF16, BF16, F32, BF16, F32)
    return pl.pallas_call(
        functools.partial(_a_proj_kernel, seq=seq, tm=tm),
        grid=(n // tm,),
        in_specs=[row_spec(D_MODEL), _const_spec((1, D_MODEL)), _const_spec((D_MODEL, A_COLS)),
                  tab_spec, tab_spec, _const_spec((SUBLANES, LANES)), _const_spec((2 * LANES, 2 * LANES))],
        out_specs=[row_spec(wd) for wd in widths],
        out_shape=[jax.ShapeDtypeStruct((n, wd), dt) for wd, dt in zip(widths, dtypes)],
        compiler_params=_cparams(1), name="a_proj",
    )(x2d, gain, w, cos, sin, gt, _head_ones())


def _compress_kernel(x_ref, pos_ref, w1_ref, b1_ref, w2_ref, b2_ref, tab_ref, gt_ref, kc_ref, vc_ref):
    nc = kc_ref.shape[0]
    lo = _lane_lo(nc)
    h = jnp.concatenate([x_ref[pl.ds(j, nc, stride=CMP_STRIDE), :] for j in range(CMP_STRIDE)], axis=1)
    top = _dot((h + pos_ref[0:1, :]).astype(BF16), w1_ref[0])
    bot = _dot((h + pos_ref[1:2, :]).astype(BF16), w1_ref[1])
    pre = top + pltpu.roll(bot, nc - 1, 0) + b1_ref[...]
    hid = jax.nn.gelu(pre).astype(BF16)
    yk = _dot(hid[:, :CMP_HIDDEN], w2_ref[0]) + b2_ref[0:1, :]
    yv = _dot(hid[:, CMP_HIDDEN:], w2_ref[1]) + b2_ref[1:2, :]
    kc_ref[...] = jnp.where(lo, _slot_norm(yk, gt_ref[...], tab_ref[...]), 0.0).astype(BF16)
    vc_ref[:, :LANES] = jnp.where(lo, yv, 1.0).astype(BF16)
    vc_ref[:, LANES:] = jnp.where(lo, 1.0, yv).astype(BF16)


def _compress(kcvc, pos, w1, b1, w2, b2, tab_c, gt):
    b, s, _ = kcvc.shape
    nc = s // CMP_STRIDE
    ospec = lambda width: pl.BlockSpec((None, None, nc, width), lambda b_, g: (b_, g, 0, 0))
    return pl.pallas_call(
        _compress_kernel,
        grid=(b, NSA_KV_HEADS),
        in_specs=[pl.BlockSpec((None, s, LANES), lambda b_, g: (b_, 0, g)),
                  _const_spec(pos.shape), _const_spec(w1.shape), _const_spec(b1.shape),
                  _const_spec(w2.shape), _const_spec(b2.shape), _const_spec(tab_c.shape),
                  _const_spec(gt.shape)],
        out_specs=[ospec(LANES), ospec(2 * LANES)],
        out_shape=[jax.ShapeDtypeStruct((b, NSA_KV_HEADS, nc, LANES), BF16),
                   jax.ShapeDtypeStruct((b, NSA_KV_HEADS, nc, 2 * LANES), BF16)],
        compiler_params=_cparams(2), name="compress",
    )(kcvc, pos, w1, b1, w2, b2, tab_c, gt)


def _prep_compress(cmp_pos, cmp_w1, cmp_b1, cmp_w2, cmp_b2):
    p = cmp_pos.reshape(2, 2, CMP_STRIDE, HEAD_DIM).transpose(1, 2, 0, 3).reshape(2, CMP_STRIDE * LANES)
    w = cmp_w1.reshape(2, 2, CMP_STRIDE, HEAD_DIM, CMP_HIDDEN)
    z = jnp.zeros_like(w[0])
    wk = jnp.concatenate([w[0], z], axis=-1)
    wv = jnp.concatenate([z, w[1]], axis=-1)
    w1 = jnp.concatenate([wk, wv], axis=2).reshape(2, CMP_STRIDE * LANES, 2 * CMP_HIDDEN)
    b1 = jnp.concatenate([cmp_b1[0], cmp_b1[1]])[None]
    w2 = jnp.stack([_slots(cmp_w2[0], 1, "rot"), _slots(cmp_w2[1], 1, "dup")])
    b2 = jnp.stack([_slots(cmp_b2[0][None], 1, "rot")[0], _slots(cmp_b2[1][None], 1, "dup")[0]])
    return p, w1.astype(BF16), b1, w2.astype(BF16), b2


STACK_ORDER = (0, 2, 1, 3)


def _stack_q(q_ref, extra=None):
    parts = []
    for r in STACK_ORDER:
        q = q_ref[:, r * LANES:(r + 1) * LANES]
        parts.append(q if extra is None else q + extra)
    return jnp.concatenate(parts, axis=0)


def _gate_pairs(g_ref, branch, tq):
    lo = _lane_lo(tq)
    col = lambda r: jnp.broadcast_to(g_ref[:, branch * NSA_GQA + r:branch * NSA_GQA + r + 1], (tq, LANES))
    return jnp.concatenate([jnp.where(lo, col(0), col(1)), jnp.where(lo, col(2), col(3))], axis=0)


def _store_pairs(o_ref, out, tq):
    o_ref[:, :LANES] = out[:tq]
    o_ref[:, LANES:] = out[tq:]


def _add_tile_mask(s, mask, tq):
    nk = s.shape[1]
    return (s.reshape(NSA_GQA, tq, nk) + mask).reshape(NSA_GQA * tq, nk)


def _window_branch(i, q_ref, kw_ref, vw_ref, g_ref, wm_ref, tq, ws, nk):
    n_sub = tq // ws
    lo = _lane_lo(2 * ws)
    outs = []
    for j in range(n_sub):
        t = i * n_sub + j
        k0 = pl.multiple_of(jnp.maximum(t * ws + ws - nk, 0), ws)
        q4 = jnp.concatenate([q_ref[j * ws:(j + 1) * ws, r * LANES:(r + 1) * LANES] for r in STACK_ORDER],
                             axis=0)
        mask = wm_ref[jnp.minimum(t, wm_ref.shape[0] - 1)]
        s = _add_tile_mask(_dot_nt(q4, kw_ref[pl.ds(k0, nk), :]), mask, ws)
        m = jnp.max(s, axis=-1, keepdims=True)
        p = jnp.exp2(s - m).astype(BF16)
        num, den = _pair_num_den(_dot(p[:2 * ws], vw_ref[pl.ds(k0, nk), :LANES]),
                                 _dot(p[2 * ws:], vw_ref[pl.ds(k0, nk), LANES:]), lo)
        outs.append(num / den)
    out = jnp.concatenate([x[:ws] for x in outs] + [x[ws:] for x in outs], axis=0)
    return out * _gate_pairs(g_ref, 2, tq)


def _compressed_branch(i, q_ref, kc_ref, vc_ref, g_ref, ov_ref, tq, n_cmp, n_blk, n_sel):
    lo = _lane_lo(tq)
    kc = kc_ref[...]
    nc = kc.shape[0]
    t_row = i * tq + lax.broadcasted_iota(jnp.int32, (1, tq), 1)
    c_col = lax.broadcasted_iota(jnp.int32, (nc, 1), 0)
    cend_col = jnp.where(c_col < n_cmp, c_col * CMP_STRIDE + (CMP_BLOCK - 1), jnp.int32(2 ** 30))
    mask_t = cend_col <= t_row
    psum = jnp.zeros((nc, tq), F32)
    xs = []
    for r in range(NSA_GQA):
        st = jnp.where(mask_t, _dot_nt(kc, q_ref[:, r * LANES:(r + 1) * LANES]), NEG)
        mt = jnp.max(st, axis=0, keepdims=True)
        pt = jnp.exp2(st - mt)
        lt = jnp.sum(pt, axis=0, keepdims=True)
        pt = pt * jnp.where(mt > NEG / 2, 1.0 / lt, 0.0)
        psum = psum + pt
        vsl = slice(0, LANES) if r % 2 == 0 else slice(LANES, 2 * LANES)
        xs.append(_dot(pt.T.astype(BF16), vc_ref[:, vsl]))
    out_c = jnp.concatenate([jnp.where(lo, xs[0], xs[1]), jnp.where(lo, xs[2], xs[3])], axis=0)
    out_c = out_c * _gate_pairs(g_ref, 0, tq)

    ps_hi = psum.astype(BF16)
    ps_lo = (psum - ps_hi.astype(F32)).astype(BF16)
    ov = ov_ref[...]
    imp = _dot(ov, ps_hi) + _dot(ov, ps_lo)

    cur = t_row // SEL_BLOCK
    ngrp = n_blk // SUBLANES
    jsub = lax.broadcasted_iota(jnp.int32, (SUBLANES, tq), 0)
    vals = []
    for a in range(ngrp):
        j = jsub + a * SUBLANES
        forced = (j == 0) | (j == cur) | (j == cur - 1)
        v = jnp.where(j <= cur, imp[a * SUBLANES:(a + 1) * SUBLANES], NEG)
        vals.append(jnp.where(forced, SEL_FORCE, v))
    ranks = [jnp.zeros((SUBLANES, tq), F32) for _ in range(ngrp)]
    for jj in range(n_blk):
        a0, s0 = divmod(jj, SUBLANES)
        rowb = jnp.broadcast_to(vals[a0][s0:s0 + 1, :], (SUBLANES, tq))
        for a in range(ngrp):
            if a > a0:
                beats = jnp.where(rowb >= vals[a], 1.0, 0.0)
            elif a < a0:
                beats = jnp.where(rowb > vals[a], 1.0, 0.0)
            else:
                beats = jnp.where(jsub > s0, jnp.where(rowb >= vals[a], 1.0, 0.0),
                                  jnp.where(rowb > vals[a], 1.0, 0.0))
            ranks[a] = ranks[a] + beats
    parts = [jnp.zeros((HEAD_DIM, tq), F32)]
    for a in range(ngrp):
        live = jnp.where(vals[a] > NEG / 2, 0.0, NEG)
        parts.append(jnp.where(ranks[a] < n_sel, live, NEG))
    if n_blk < HEAD_DIM:
        parts.append(jnp.zeros((HEAD_DIM - n_blk, tq), F32))
    bias_t = jnp.concatenate(parts, axis=0)
    return out_c, bias_t.T.astype(BF16)


def _band_masks(n_var, tq, nk, k0_of, lo_dist, hi_dist):
    r = np.arange(tq)[None, :, None]
    c = np.arange(nk)[None, None, :]
    v = np.arange(n_var)[:, None, None]
    dist = v * tq + r - (np.asarray([k0_of(x) for x in range(n_var)])[:, None, None] + c)
    return jnp.asarray(np.where((dist >= lo_dist) & (dist <= hi_dist), 0.0, NEG), F32)


def _selected_branch(i, q_ref, bias, k_ref, v_ref, g_ref, dm_ref, m_ref, acc_ref, tq, tk, ws):
    rows = NSA_GQA * tq
    half = 2 * tq
    q4 = _stack_q(q_ref, bias)
    m_ref[...] = jnp.full((rows, LANES), NEG, F32)
    acc_ref[...] = jnp.zeros((rows, LANES), F32)
    rep = tk // LANES

    def step(kt, c):
        k0 = pl.multiple_of(kt * tk, tk)
        s = _dot_nt(q4, k_ref[pl.ds(k0, tk), :])
        m_prev = m_ref[...]
        m_next = jnp.maximum(m_prev, jnp.max(s, axis=-1, keepdims=True))
        p = jnp.exp2(s - jnp.tile(m_next, (1, rep))).astype(BF16)
        alpha = jnp.exp2(m_prev - m_next)
        m_ref[...] = m_next
        acc_ref[:half] = acc_ref[:half] * alpha[:half] + _dot(p[:half], v_ref[pl.ds(k0, tk), :LANES])
        acc_ref[half:] = acc_ref[half:] * alpha[half:] + _dot(p[half:], v_ref[pl.ds(k0, tk), LANES:])
        return c

    lax.fori_loop(0, i, step, 0)
    k0 = pl.multiple_of(i * tk, tk)
    for j in range(tq // ws):
        width = (j + 1) * ws
        rsl = [slice(p_ * tq + j * ws, p_ * tq + (j + 1) * ws) for p_ in range(NSA_GQA)]
        s = _dot_nt(jnp.concatenate([q4[sl] for sl in rsl], axis=0), k_ref[pl.ds(k0, width), :])
        s = _add_tile_mask(s, dm_ref[j * ws:(j + 1) * ws, :width], ws)
        m_prev = jnp.concatenate([m_ref[sl] for sl in rsl], axis=0)
        m_next = jnp.maximum(m_prev, jnp.max(s, axis=-1, keepdims=True))
        p = jnp.exp2(s - jnp.tile(m_next, (1, width // LANES))).astype(BF16)
        alpha = jnp.exp2(m_prev - m_next)
        upd = jnp.concatenate([_dot(p[:2 * ws], v_ref[pl.ds(k0, width), :LANES]),
                               _dot(p[2 * ws:], v_ref[pl.ds(k0, width), LANES:])], axis=0)
        for idx, sl in enumerate(rsl):
            blk = slice(idx * ws, (idx + 1) * ws)
            acc_ref[sl] = acc_ref[sl] * alpha[blk] + upd[blk]
    num, den = _pair_num_den(acc_ref[:half], acc_ref[half:], _lane_lo(half))
    return num / den * _gate_pairs(g_ref, 1, tq)


def _nsa_kernel(q_ref, kc_ref, vc_ref, kw_ref, vw_ref, ks_ref, vs_ref, g_ref, ov_ref, wm_ref, dm_ref,
                o_ref, m_ref, acc_ref, *, tq, tk, ws, nk, n_cmp, n_blk, n_sel):
    i = pl.program_id(2)
    out_w = _window_branch(i, q_ref, kw_ref, vw_ref, g_ref, wm_ref, tq, ws, nk)
    out_c, bias = _compressed_branch(i, q_ref, kc_ref, vc_ref, g_ref, ov_ref, tq, n_cmp, n_blk, n_sel)
    _store_pairs(o_ref, out_c + out_w, tq)
    out_s = _selected_branch(i, q_ref, bias, ks_ref, vs_ref, g_ref, dm_ref, m_ref, acc_ref, tq, tk, ws)
    o_ref[:, :LANES] += out_s[:tq]
    o_ref[:, LANES:] += out_s[tq:]


def _nsa_attention(qa, kc, vc, kw, vw, ks, vs, gates, ov, n_cmp, n_sel):
    b, s, _ = qa.shape
    tq, tk = Q_TILE, min(SEL_KEY_TILE, s)
    ws = WIN_SUB_TILE
    nk = min(WINDOW + ws, s)
    nc = kc.shape[2]
    n_blk = s // SEL_BLOCK
    gw = NSA_GQA * LANES
    n_early = (nk - ws) // ws
    wmask = _band_masks(n_early + 1, ws, nk, lambda v: max(v * ws + ws - nk, 0), 0, WINDOW - 1)
    assert tq == tk
    dmask = _band_masks(1, tq, tk, lambda v: 0, 0, tk)[0]
    kern = functools.partial(_nsa_kernel, tq=tq, tk=tk, ws=ws, nk=nk, n_cmp=n_cmp, n_blk=n_blk,
                             n_sel=n_sel)
    k_spec = pl.BlockSpec((None, s, LANES), lambda b_, g, i: (b_, 0, g))
    v_spec = pl.BlockSpec((None, s, 2 * LANES), lambda b_, g, i: (b_, 0, g))
    return pl.pallas_call(
        kern,
        grid=(b, NSA_KV_HEADS, s // tq),
        in_specs=[pl.BlockSpec((None, tq, gw), lambda b_, g, i: (b_, i, g)),
                  pl.BlockSpec((None, None, nc, LANES), lambda b_, g, i: (b_, g, 0, 0)),
                  pl.BlockSpec((None, None, nc, 2 * LANES), lambda b_, g, i: (b_, g, 0, 0)),
                  k_spec, v_spec, k_spec, v_spec,
                  pl.BlockSpec((None, tq, LANES), lambda b_, g, i: (b_, i, g)),
                  _const_spec(ov.shape),
                  _const_spec(wmask.shape), _const_spec(dmask.shape)],
        out_specs=pl.BlockSpec((None, tq, 2 * LANES), lambda b_, g, i: (b_, i, g)),
        out_shape=jax.ShapeDtypeStruct((b, s, NSA_KV_HEADS * 2 * LANES), F32),
        scratch_shapes=[pltpu.VMEM((NSA_GQA * tq, LANES), F32), pltpu.VMEM((NSA_GQA * tq, LANES), F32)],
        compiler_params=_cparams(3), name="nsa_attention",
    )(qa, kc, vc, kw, vw, ks, vs, gates, ov, wmask, dmask)


def _mem_kv_kernel(mem_ref, gain_ref, w_ref, gt_ref, ones_ref, k_ref, v_ref):
    hn = _rms_rows(mem_ref[...], gain_ref[...]).astype(BF16)
    y = _dot(hn, w_ref[...])
    ms = _head_mean_sq(y[:, :MEM_W], ones_ref)
    for p in range(MEM_HEADS // 2):
        sl = slice(p * LANES, (p + 1) * LANES)
        z = _pair_norm(y[:, sl], ms[:, sl], gt_ref[...], None)
        _store_head_slots(k_ref, z, MEM_HEADS, p, 0.0)
        c = MEM_W + p * LANES
        _store_value_slots(v_ref, y[:, c:c + LANES], MEM_HEADS, p, False)


def _mem_kv(mem, gains, w, gt):
    b, nm, _ = mem.shape
    nl = gains.shape[0]
    width = MEM_HEADS * LANES
    out_spec = pl.BlockSpec((None, None, nm, width), lambda l, b_: (l, b_, 0, 0))
    return pl.pallas_call(
        _mem_kv_kernel,
        grid=(nl, b),
        in_specs=[pl.BlockSpec((None, nm, D_MODEL), lambda l, b_: (b_, 0, 0)),
                  pl.BlockSpec((None, 1, D_MODEL), lambda l, b_: (l, 0, 0)),
                  pl.BlockSpec((None, D_MODEL, 2 * MEM_W), lambda l, b_: (l, 0, 0)),
                  pl.BlockSpec((None, 1, LANES), lambda l, b_: (l, 0, 0)),
                  _const_spec((2 * LANES, 2 * LANES))],
        out_specs=[out_spec, out_spec],
        out_shape=[jax.ShapeDtypeStruct((nl, b, nm, width), BF16)] * 2,
        compiler_params=_cparams(2), name="mem_kv",
    )(mem, gains, w, gt, _head_ones())


def _mem_attend(q_ref, k_ref, v_ref):
    lo = _lane_lo(q_ref.shape[0])
    accs = []
    for h in range(MEM_HEADS):
        sl = slice(h * LANES, (h + 1) * LANES)
        s = _dot_nt(q_ref[:, sl], k_ref[:, sl])
        m = jnp.max(s, axis=-1, keepdims=True)
        accs.append(_dot(jnp.exp2(s - m).astype(BF16), v_ref[:, sl]))
    outs = []
    for pair in range(MEM_HEADS // 2):
        num, den = _pair_num_den(accs[2 * pair], accs[2 * pair + 1], lo)
        outs.append(num / den)
    return jnp.concatenate(outs, axis=1)


def _out_mlp_kernel(o_ref, qm_ref, mk_ref, mv_ref, x_ref, wo_ref, gain_ref, wup_ref, wdn_ref, out_ref):
    km = o_ref.shape[1]
    o_mem = _mem_attend(qm_ref, mk_ref, mv_ref).astype(BF16)
    x1 = x_ref[...] + _dot(o_ref[...].astype(BF16), wo_ref[:km, :]) + _dot(o_mem, wo_ref[km:, :])
    hn = _rms_rows(x1, gain_ref[...]).astype(BF16)
    out_ref[...] = x1
    for c in range(D_FF // FF_CHUNK):
        u = _dot(hn, wup_ref[:, c * FF_CHUNK:(c + 1) * FF_CHUNK])
        u = jnp.square(jnp.maximum(u, 0.0)).astype(BF16)
        out_ref[...] += _dot(u, wdn_ref[c * FF_CHUNK:(c + 1) * FF_CHUNK, :])


def _out_mlp(o_main, qm, mk, mv, layer, seq, x2d, wo, gain, wup, wdn, name):
    n = x2d.shape[0]
    tm = ROW_TILE
    nseq = seq // tm
    nm, mw = mk.shape[2], mk.shape[3]
    row_spec = lambda width: pl.BlockSpec((tm, width), lambda i: (i, 0))
    kv_spec = pl.BlockSpec((None, None, nm, mw), lambda i: (layer, i // nseq, 0, 0))
    return pl.pallas_call(
        _out_mlp_kernel,
        grid=(n // tm,),
        in_specs=[row_spec(o_main.shape[1]), row_spec(qm.shape[1]), kv_spec, kv_spec, row_spec(D_MODEL),
                  _const_spec(wo.shape), _const_spec((1, D_MODEL)), _const_spec(wup.shape),
                  _const_spec(wdn.shape)],
        out_specs=row_spec(D_MODEL),
        out_shape=jax.ShapeDtypeStruct((n, D_MODEL), F32),
        compiler_params=_cparams(1), name=name,
    )(o_main, qm, mk, mv, x2d, wo, gain, wup, wdn)


def _b_proj_kernel(x_ref, gains_ref, wq_ref, wkv_ref, cos_ref, sin_ref, gt_ref, ones_ref,
                   q0_ref, q1_ref, q2_ref, qm_ref, k_ref, v_ref, *, tm):
    x = x_ref[...]
    xn = x * lax.rsqrt(jnp.mean(x * x, axis=-1, keepdims=True) + EPS)
    rope = _rope_args(cos_ref, sin_ref, tm)
    hq = (xn * gains_ref[0:1, :]).astype(BF16)
    width = DIL_SLOTS * HEAD_DIM
    for gi, ref in enumerate((q0_ref, q1_ref, q2_ref)):
        y = _dot(hq, wq_ref[:, gi * width:(gi + 1) * width])
        ms = _head_mean_sq(y, ones_ref)
        for p in range(DIL_SLOTS // 2):
            sl = slice(p * LANES, (p + 1) * LANES)
            z = _pair_norm(y[:, sl], ms[:, sl], gt_ref[gi:gi + 1, :], rope) * Q_SCALE
            _store_head_slots(ref, z, DIL_SLOTS, p, 0.0)
    y = _dot(hq, wq_ref[:, B_Q:B_Q + MEM_W])
    ms = _head_mean_sq(y, ones_ref)
    for p in range(MEM_HEADS // 2):
        sl = slice(p * LANES, (p + 1) * LANES)
        z = _pair_norm(y[:, sl], ms[:, sl], gt_ref[3:4, :], None) * Q_SCALE
        _store_head_slots(qm_ref, z, MEM_HEADS, p, 0.0)
    hk = (xn * gains_ref[1:2, :]).astype(BF16)
    y = _dot(hk, wkv_ref[...])
    ms = _head_mean_sq(y[:, :width], ones_ref)
    for p in range(DIL_SLOTS // 2):
        sl = slice(p * LANES, (p + 1) * LANES)
        z = _pair_norm(y[:, sl], ms[:, sl], gt_ref[4:5, :], rope)
        _store_head_slots(k_ref, z, DIL_SLOTS, p, 0.0)
        c = width + p * LANES
        _store_value_slots(v_ref, y[:, c:c + LANES], DIL_SLOTS, p, False)


def _b_proj(x2d, gains, wq, wkv, cos, sin, gt, seq):
    n = x2d.shape[0]
    tm = PROJ_ROW_TILE
    nseq = seq // tm
    row_spec = lambda width: pl.BlockSpec((tm, width), lambda i: (i, 0))
    tab_spec = pl.BlockSpec((tm, LANES), lambda i: (i % nseq, 0))
    width = DIL_SLOTS * LANES
    widths = (width, width, width, MEM_HEADS * LANES, width, width)
    return pl.pallas_call(
        functools.partial(_b_proj_kernel, tm=tm),
        grid=(n // tm,),
        in_specs=[row_spec(D_MODEL), _const_spec(gains.shape), _const_spec(wq.shape),
                  _const_spec(wkv.shape), tab_spec, tab_spec, _const_spec(gt.shape),
                  _const_spec((2 * LANES, 2 * LANES))],
        out_specs=[row_spec(wd) for wd in widths],
        out_shape=[jax.ShapeDtypeStruct((n, wd), BF16) for wd in widths],
        compiler_params=_cparams(1), name="b_proj",
    )(x2d, gains, wq, wkv, cos, sin, gt, _head_ones())


def _dil_kernel(q0_ref, q1_ref, q2_ref, k_ref, v_ref, o_ref, qf_ref, kf_ref, vf_ref, lse_ref, *, seq):
    tq = DIL_Q_TILE
    heads = (slice(0, LANES), slice(LANES, 2 * LANES))
    for hd, sl in enumerate(heads):
        kf_ref[hd] = k_ref[:, sl].astype(F32)
        vf_ref[hd] = v_ref[:, sl].astype(F32)

    def attend(qs, ks, vs, qpos0, kpos0):
        nq, nk = qs[0].shape[0], ks[0].shape[0]
        lo = _lane_lo(nq)
        dist = (qpos0 + lax.broadcasted_iota(jnp.int32, (nq, 1), 0)
                - (kpos0 + lax.broadcasted_iota(jnp.int32, (1, nk), 1)))
        mask = (dist >= 0) & (dist <= DIL_BAND)
        accs, ms = [], []
        for q, k, v in zip(qs, ks, vs):
            s = jnp.where(mask, _dot_nt(q, k), NEG)
            m = jnp.max(s, axis=-1, keepdims=True)
            accs.append(_dot(jnp.exp2(s - m).astype(BF16), v))
            ms.append(jnp.broadcast_to(m, (nq, LANES)))
        num, den = _pair_num_den(accs[0], accs[1], lo)
        return num / den, jnp.where(lo, ms[0], ms[1]) + jnp.log2(den)

    def merge(o_old, l_old, o_new, l_new):
        mx = jnp.maximum(l_old, l_new)
        a = jnp.exp2(l_old - mx)
        b = jnp.exp2(l_new - mx)
        den = a + b
        return (a * o_old + b * o_new) / den, mx + jnp.log2(den)

    nk = min(tq + DIL_BAND, seq)

    def body0(i, c):
        q0 = pl.multiple_of(i * tq, tq)
        k0 = pl.multiple_of(jnp.clip(i * tq - DIL_BAND, 0, seq - nk), DIL_BAND)
        o, l = attend([q0_ref[pl.ds(q0, tq), sl] for sl in heads],
                      [k_ref[pl.ds(k0, nk), sl] for sl in heads],
                      [v_ref[pl.ds(k0, nk), sl] for sl in heads], q0, k0)
        o_ref[pl.ds(q0, tq), :] = o
        lse_ref[pl.ds(q0, tq), :] = l
        return c

    lax.fori_loop(0, seq // tq, body0, 0, unroll=DIL_UNROLL)

    for q_ref, (_, dil) in zip((q1_ref, q2_ref), DIL_PATTERNS[1:]):
        for hd, sl in enumerate(heads):
            qf_ref[hd] = q_ref[:, sl].astype(F32)
        length = seq // dil
        tqd = min(tq if dil < 2 * SUBLANES else DIL_Q_TILE_WIDE, length)
        nkd = min(tqd + DIL_BAND, length)
        ntile = length // tqd

        def body(it, c, dil=dil, length=length, tqd=tqd, nkd=nkd, ntile=ntile):
            r = it // ntile
            i = it - r * ntile
            qp = i * tqd
            kp = jnp.clip(qp - DIL_BAND, 0, length - nkd)
            qrows = pl.ds(r + dil * qp, tqd, stride=dil)
            krows = pl.ds(r + dil * kp, nkd, stride=dil)
            o, l = attend([qf_ref[hd, qrows, :].astype(BF16) for hd in range(2)],
                          [kf_ref[hd, krows, :].astype(BF16) for hd in range(2)],
                          [vf_ref[hd, krows, :].astype(BF16) for hd in range(2)], qp, kp)
            o, l = merge(o_ref[qrows, :], lse_ref[qrows, :], o, l)
            o_ref[qrows, :] = o
            lse_ref[qrows, :] = l
            return c

        lax.fori_loop(0, dil * ntile, body, 0, unroll=DIL_UNROLL)


def _dil_attention(q0, q1, q2, k, v):
    b, s, _ = k.shape
    npair = DIL_SLOTS // 2
    pw = 2 * LANES
    in_spec = pl.BlockSpec((None, s, pw), lambda b_, hp: (b_, 0, hp))
    return pl.pallas_call(
        functools.partial(_dil_kernel, seq=s),
        grid=(b, npair),
        in_specs=[in_spec] * 5,
        out_specs=pl.BlockSpec((None, s, LANES), lambda b_, hp: (b_, 0, hp)),
        out_shape=jax.ShapeDtypeStruct((b, s, DIL_SLOTS * HEAD_DIM), F32),
        scratch_shapes=[pltpu.VMEM((2, s, LANES), F32)] * 3 + [pltpu.VMEM((s, LANES), F32)],
        compiler_params=_cparams(2), name="dil_attention",
    )(q0, q1, q2, k, v)


def kernel(x, mem, attn_norm, mlp_norm, w_up, w_down, mem_norm, w_mem_kv, mem_q_norm, mem_k_norm,
           a_w_in, a_w_out, a_q_norm, a_k_norm, a_cmp_pos, a_cmp_w1, a_cmp_b1, a_cmp_w2, a_cmp_b2,
           kv_norm, w_kv_shared, kv_k_norm, b_w_in, b_w_out, b_q_norm):
    b, s, _ = x.shape
    n = b * s
    assert s % PROJ_ROW_TILE == 0 and s % ROW_TILE == 0 and s % SEL_KEY_TILE == 0
    assert s // SEL_BLOCK <= HEAD_DIM
    assert s >= WINDOW + WIN_SUB_TILE and s % Q_TILE == 0 and Q_TILE % WIN_SUB_TILE == 0
    assert (s // DIL_PATTERNS[-1][1]) % DIL_BAND == 0
    x2d = x.reshape(n, D_MODEL)
    cos, sin = _rope_pair_tables(jnp.arange(s))

    mk, mv = _mem_kv(mem, mem_norm[:, None, :], w_mem_kv.astype(BF16), _gain_pair(mem_k_norm)[:, None, :])

    gt_a = jnp.zeros((SUBLANES, LANES), F32)
    gt_a = gt_a.at[0].set(_gain_pair(a_q_norm[0])).at[1].set(_gain_pair(a_k_norm[0, 1]))
    gt_a = gt_a.at[2].set(_gain_pair(a_k_norm[0, 2])).at[3].set(_gain_pair(mem_q_norm[0]))
    qa, ks, kw, vs, vw, kcvc, qm, gates = _a_proj(
        x2d, attn_norm[0:1], _prep_a_w_in(a_w_in[0]), cos, sin, gt_a, s)

    n_cmp = (s - CMP_BLOCK) // CMP_STRIDE + 1
    nc = s // CMP_STRIDE
    tab_c = _rope_table(jnp.arange(nc) * CMP_STRIDE + (CMP_BLOCK - 1))
    kc, vc = _compress(kcvc.reshape(b, s, -1),
                       *_prep_compress(a_cmp_pos[0], a_cmp_w1[0], a_cmp_b1[0], a_cmp_w2[0], a_cmp_b2[0]),
                       tab_c, _gain_slot(a_k_norm[0, 0], "rot")[None])

    n_blk = s // SEL_BLOCK
    c_start = np.arange(nc)[None, :] * CMP_STRIDE
    b_start = np.arange(n_blk)[:, None] * SEL_BLOCK
    overlap_t = ((c_start < b_start + SEL_BLOCK) & (c_start + CMP_BLOCK > b_start)
                 & (np.arange(nc)[None, :] < n_cmp))
    ov = jnp.asarray(overlap_t, BF16)

    qa3 = qa.reshape(b, s, -1)
    gates3 = gates.reshape(b, s, -1)
    o_a = _nsa_attention(qa3, kc, vc, kw.reshape(b, s, -1), vw.reshape(b, s, -1), ks.reshape(b, s, -1),
                         vs.reshape(b, s, -1), gates3, ov, n_cmp, min(SEL_TOPK, n_blk))
    x2d = _out_mlp(o_a.reshape(n, -1), qm, mk, mv, 0, s, x2d, a_w_out[0].astype(BF16), mlp_norm[0:1],
                   w_up[0].astype(BF16), w_down[0].astype(BF16), "a_out_mlp")

    gt_b = jnp.zeros((SUBLANES, LANES), F32)
    for gi in range(N_DIL_GROUPS):
        gt_b = gt_b.at[gi].set(_gain_pair(b_q_norm[0, gi]))
    gt_b = gt_b.at[3].set(_gain_pair(mem_q_norm[1])).at[4].set(_gain_pair(kv_k_norm))
    q0, q1, q2, qm1, kb, vb = _b_proj(x2d, jnp.stack([attn_norm[1], kv_norm]), b_w_in[0].astype(BF16),
                                      w_kv_shared.astype(BF16), cos, sin, gt_b, s)
    o_d = _dil_attention(*(t.reshape(b, s, -1) for t in (q0, q1, q2, kb, vb)))
    x2d = _out_mlp(o_d.reshape(n, -1), qm1, mk, mv, 1, s, x2d, b_w_out[0].astype(BF16), mlp_norm[1:2],
                   w_up[1].astype(BF16), w_down[1].astype(BF16), "b_out_mlp")
    return x2d.reshape(b, s, D_MODEL)
```
